```python
import jax
import jax.numpy as jnp
from jax import lax
import numpy as np

D_MODEL = 1024
BATCH = 2
SEQ = 8192
DEPTH = 1
DEC_BATCH = 128
DEC_SEQ = 8
PAST_LEN = 8192
PAGE_SIZE = 128

MIX_W = D_MODEL
ML_W = MIX_W // 2
MLA_W = MIX_W - ML_W
ML_DK = 128
ML_DV = 128
ML_H = ML_W // ML_DV
MLA_VD = 64
MLA_H = MLA_W // MLA_VD
MLA_NOPE = 64
MLA_ROPE = 32
MLA_SCALE = (MLA_NOPE + MLA_ROPE) ** -0.5
Q_LORA = 3 * D_MODEL // 8
KV_LORA = D_MODEL // 4
ROPE_THETA = 10000.0
ML_CHUNK = 128
Q_BLOCK = 128
EPS = 1e-6
ALPHA = (2.0 * DEPTH) ** 0.25
BETA = (8.0 * DEPTH) ** -0.25
IN_SIZES = (ML_H * ML_DK, ML_H * ML_DK, ML_W, ML_H, ML_H, ML_W, ML_W, Q_LORA, KV_LORA, MLA_ROPE, MLA_W)
IN_OFFS = tuple(int(o) for o in np.cumsum(IN_SIZES)[:-1])
N_IN = int(sum(IN_SIZES))

kernel_name = 'hymba_mlstm_mla_deepnorm_adaln_step'


def _layernorm(x, g, b):
    xf = x.astype(jnp.float32)
    mu = jnp.mean(xf, -1, keepdims=True)
    var = jnp.mean(jnp.square(xf - mu), -1, keepdims=True)
    return ((xf - mu) * lax.rsqrt(var + EPS) * g + b).astype(x.dtype)


def _rmsnorm(x, g):
    xf = x.astype(jnp.float32)
    return (xf * lax.rsqrt(jnp.mean(xf * xf, -1, keepdims=True) + EPS) * g).astype(x.dtype)


def _rope(x, pos):
    half = MLA_ROPE // 2
    inv = ROPE_THETA ** (-jnp.arange(half, dtype=jnp.float32) / half)
    ang = pos.astype(jnp.float32)[:, None] * inv[None, :]
    shape = (x.shape[1],) + (1,) * (x.ndim - 3) + (half,)
    cos = jnp.cos(ang).reshape(shape)
    sin = jnp.sin(ang).reshape(shape)
    xf = x.astype(jnp.float32)
    x1, x2 = xf[..., :half], xf[..., half:]
    return jnp.concatenate([x1 * cos - x2 * sin, x2 * cos + x1 * sin], -1).astype(x.dtype)


def _front(x, c, pos, w_ada, b_ada, w_in, ml_b_i, ml_b_f, mla_q_norm, mla_kv_norm, mla_w_uq, mla_w_uk):
    B, S, _ = x.shape
    f32 = jnp.float32
    shift, scale, gate = jnp.split(c @ w_ada + b_ada, 3, axis=-1)
    h = x * (1.0 + scale[:, None, :]) + shift[:, None, :]
    mq, mk, mv, mi, mf, mo, mz, cq, ckv, kr, az = jnp.split(h @ w_in, IN_OFFS, axis=-1)

    def heads(t, d):
        return jnp.moveaxis(t.reshape(B, S, ML_H, d), 2, 1).astype(f32)

    q = heads(mq, ML_DK)
    k = heads(mk, ML_DK) * (ML_DK ** -0.5)
    v = heads(mv, ML_DV)
    ig = jnp.moveaxis((mi + ml_b_i).astype(f32), 2, 1)
    lf = jax.nn.log_sigmoid(jnp.moveaxis((mf + ml_b_f).astype(f32), 2, 1))
    cq = _rmsnorm(cq, mla_q_norm)
    qh = (cq @ mla_w_uq).reshape(B, S, MLA_H, MLA_NOPE + MLA_ROPE)
    q_rope = _rope(qh[..., MLA_NOPE:], pos)
    q_lat = jnp.einsum('bshn,chn->bshc', qh[..., :MLA_NOPE],
                       mla_w_uk.reshape(KV_LORA, MLA_H, MLA_NOPE))
    ckv = _rmsnorm(ckv, mla_kv_norm)
    kr = _rope(kr, pos)
    return gate, (q, k, v, ig, lf), mo, mz, q_lat, q_rope, ckv, kr, az


def _mlstm_chunk(carry, inp):
    C, n, m = carry
    q, k, v, ig, lf = inp
    L = q.shape[2]
    F = jnp.cumsum(lf, axis=-1)
    causal = jnp.tril(jnp.ones((L, L), bool))
    D = jnp.where(causal, F[..., :, None] - F[..., None, :] + ig[..., None, :], -jnp.inf)
    m_inter = F + m[..., None]
    m_t = jnp.maximum(m_inter, jnp.max(D, -1))
    W = jnp.exp(D - m_t[..., None])
    a = jnp.exp(m_inter - m_t)
    Sq = jnp.einsum('bhtd,bhsd->bhts', q, k) * W
    num = jnp.einsum('bhts,bhsv->bhtv', Sq, v) + a[..., None] * jnp.einsum('bhtd,bhdv->bhtv', q, C)
    den = jnp.sum(Sq, -1) + a * jnp.einsum('bhtd,bhd->bht', q, n)
    h = num / jnp.maximum(jnp.abs(den), jnp.exp(-m_t))[..., None]
    wl = W[..., -1, :]
    al = a[..., -1]
    C_new = al[..., None, None] * C + jnp.einsum('bhs,bhsd,bhsv->bhdv', wl, k, v)
    n_new = al[..., None] * n + jnp.einsum('bhs,bhsd->bhd', wl, k)
    return (C_new, n_new, m_t[..., -1]), h


def _mlstm(mlin, C0, n0, m0, chunk):
    q = mlin[0]
    B, H, S, _ = q.shape
    nc = S // chunk

    def to_chunks(t):
        return jnp.moveaxis(t.reshape((B, H, nc, chunk) + t.shape[3:]), 2, 0)

    (C, n, m), h = lax.scan(_mlstm_chunk, (C0, n0, m0), tuple(to_chunks(t) for t in mlin))
    h = jnp.moveaxis(h, 0, 2).reshape(B, H, S, ML_DV)
    return h, C, n, m


def _mla_scores(q_lat, q_rope, ckv, kr):
    s = jnp.einsum('bqhc,bkc->bhqk', q_lat, ckv) + jnp.einsum('bqhr,bkr->bhqk', q_rope, kr)
    return s.astype(jnp.float32) * MLA_SCALE


def _mla_prompt(q_lat, q_rope, ckv, kr):
    B, S = q_lat.shape[:2]
    qb = min(Q_BLOCK, S)
    nb = S // qb

    def blk(t):
        return jnp.moveaxis(t.reshape((B, nb, qb) + t.shape[2:]), 1, 0)

    kpos = jnp.arange(S)

    def one(args):
        ql, qr, i = args
        s = _mla_scores(ql, qr, ckv, kr)
        qpos = i * qb + jnp.arange(qb)
        s = jnp.where(kpos[None, :] <= qpos[:, None], s, -jnp.inf)
        p = jax.nn.softmax(s, axis=-1).astype(ckv.dtype)
        return jnp.einsum('bhqk,bkc->bqhc', p, ckv)

    o = lax.map(one, (blk(q_lat), blk(q_rope), jnp.arange(nb)))
    return jnp.moveaxis(o, 0, 1).reshape(B, S, MLA_H, KV_LORA)


def _mla_sample(q_lat, q_rope, ckv, kr, pool_ckv, pool_kr, page_table):
    B, S = q_lat.shape[:2]
    past_ckv = pool_ckv[page_table].reshape(B, -1, KV_LORA)
    past_kr = pool_kr[page_table].reshape(B, -1, MLA_ROPE)
    P = past_ckv.shape[1]
    s_past = _mla_scores(q_lat, q_rope, past_ckv, past_kr)
    s_new = _mla_scores(q_lat, q_rope, ckv, kr)
    s_new = jnp.where(jnp.tril(jnp.ones((S, S), bool)), s_new, -jnp.inf)
    p = jax.nn.softmax(jnp.concatenate([s_past, s_new], -1), axis=-1)
    o = (jnp.einsum('bhqk,bkc->bqhc', p[..., :P].astype(past_ckv.dtype), past_ckv)
         + jnp.einsum('bhqk,bkc->bqhc', p[..., P:].astype(ckv.dtype), ckv))
    return o


def _back(x, gate, h_ml, mo, mz, o_lat, az, ml_gn, mla_w_uv, w_out, ln_g, ln_b):
    B, S, _ = x.shape
    hm = jnp.moveaxis(h_ml, 1, 2)
    hm = hm * jax.nn.sigmoid(mo.astype(jnp.float32)).reshape(B, S, ML_H, ML_DV)
    mu = jnp.mean(hm, -1, keepdims=True)
    var = jnp.mean(jnp.square(hm - mu), -1, keepdims=True)
    hm = ((hm - mu) * lax.rsqrt(var + EPS)).reshape(B, S, ML_W) * ml_gn
    y_ml = hm.astype(x.dtype) * jax.nn.silu(mz)
    o_mla = jnp.einsum('bshc,chv->bshv', o_lat, mla_w_uv.reshape(KV_LORA, MLA_H, MLA_VD)).reshape(B, S, MLA_W)
    y_mla = o_mla * jax.nn.silu(az)
    out = jnp.concatenate([y_ml, y_mla], -1) @ w_out
    return _layernorm(ALPHA * x + gate[:, None, :] * out, ln_g, ln_b)


def setup_inputs(seed: int = 0) -> dict:
    key = jax.random.key(seed)
    ks = jax.random.split(key, 32)
    f32 = jnp.float32

    def nrm(k, shape, s=1.0):
        return jax.random.normal(k, shape, f32) * s

    n_pages = PAST_LEN // PAGE_SIZE
    n_used = DEC_BATCH * n_pages
    n_pool = n_used + max(1, n_used // 4)
    page_table = jax.random.permutation(ks[0], n_pool)[:n_used].reshape(DEC_BATCH, n_pages).astype(jnp.int32)
    Dm = D_MODEL
    return {
        'x_prompt': nrm(ks[1], (BATCH, SEQ, Dm)),
        'x_sample': nrm(ks[2], (DEC_BATCH, DEC_SEQ, Dm)),
        'c_prompt': nrm(ks[3], (BATCH, Dm)),
        'c_sample': nrm(ks[4], (DEC_BATCH, Dm)),
        'cache_ckv': nrm(ks[5], (DEPTH, n_pool, PAGE_SIZE, KV_LORA)),
        'cache_krope': nrm(ks[6], (DEPTH, n_pool, PAGE_SIZE, MLA_ROPE)),
        'state_C': nrm(ks[7], (DEPTH, DEC_BATCH, ML_H, ML_DK, ML_DV), 0.1),
        'state_n': jnp.abs(nrm(ks[8], (DEPTH, DEC_BATCH, ML_H, ML_DK), 0.5)),
        'state_m': nrm(ks[9], (DEPTH, DEC_BATCH, ML_H)),
        'page_table': page_table,
        'w_ada': nrm(ks[10], (DEPTH, Dm, 3 * Dm), 0.5 * Dm ** -0.5),
        'b_ada': nrm(ks[11], (DEPTH, 3 * Dm), 0.02),
        'w_in': nrm(ks[12], (DEPTH, Dm, N_IN), Dm ** -0.5),
        'ml_b_i': nrm(ks[13], (DEPTH, ML_H), 0.1),
        'ml_b_f': jnp.broadcast_to(jnp.linspace(3.0, 6.0, ML_H, dtype=f32), (DEPTH, ML_H)) + nrm(ks[14], (DEPTH, ML_H), 0.01),
        'ml_gn': 1.0 + nrm(ks[15], (DEPTH, ML_W), 0.02),
        'mla_q_norm': 1.0 + nrm(ks[16], (DEPTH, Q_LORA), 0.02),
        'mla_kv_norm': 1.0 + nrm(ks[17], (DEPTH, KV_LORA), 0.02),
        'mla_w_uq': nrm(ks[18], (DEPTH, Q_LORA, MLA_H * (MLA_NOPE + MLA_ROPE)), Q_LORA ** -0.5),
        'mla_w_uk': nrm(ks[19], (DEPTH, KV_LORA, MLA_H * MLA_NOPE), KV_LORA ** -0.5),
        'mla_w_uv': nrm(ks[20], (DEPTH, KV_LORA, MLA_H * MLA_VD), KV_LORA ** -0.5),
        'w_out': nrm(ks[21], (DEPTH, MIX_W, Dm), BETA * MIX_W ** -0.5),
        'ln_g': 1.0 + nrm(ks[22], (DEPTH, Dm), 0.02),
        'ln_b': nrm(ks[23], (DEPTH, Dm), 0.02),
    }


def reference(x_prompt, x_sample, c_prompt, c_sample, cache_ckv, cache_krope, state_C, state_n, state_m,
              page_table, w_ada, b_ada, w_in, ml_b_i, ml_b_f, ml_gn, mla_q_norm, mla_kv_norm,
              mla_w_uq, mla_w_uk, mla_w_uv, w_out, ln_g, ln_b):
    f32 = jnp.float32
    B, S, _ = x_prompt.shape
    DB, DS, _ = x_sample.shape
    past = page_table.shape[1] * PAGE_SIZE
    pos_p = jnp.arange(S)
    pos_s = past + jnp.arange(DS)
    xp, xs = x_prompt, x_sample
    ckv_p_l, kr_p_l, Cp_l, np_l, mp_l = [], [], [], [], []
    ckv_s_l, kr_s_l, Cs_l, ns_l, ms_l = [], [], [], [], []
    for l in range(DEPTH):
        fw = (w_ada[l], b_ada[l], w_in[l], ml_b_i[l], ml_b_f[l], mla_q_norm[l], mla_kv_norm[l], mla_w_uq[l], mla_w_uk[l])
        bw = (ml_gn[l], mla_w_uv[l], w_out[l], ln_g[l], ln_b[l])
        gate, mlin, mo, mz, q_lat, q_rope, ckv, kr, az = _front(xp, c_prompt, pos_p, *fw)
        C0 = jnp.zeros((B, ML_H, ML_DK, ML_DV), f32)
        n0 = jnp.zeros((B, ML_H, ML_DK), f32)
        m0 = jnp.full((B, ML_H), -jnp.inf, f32)
        h_ml, Cp, n_p, mp = _mlstm(mlin, C0, n0, m0, min(ML_CHUNK, S))
        o_lat = _mla_prompt(q_lat, q_rope, ckv, kr)
        yp = _back(xp, gate, h_ml, mo, mz, o_lat, az, *bw)
        ckv_p_l.append(ckv)
        kr_p_l.append(kr)
        Cp_l.append(Cp.astype(state_C.dtype))
        np_l.append(n_p.astype(state_n.dtype))
        mp_l.append(mp.astype(state_m.dtype))
        gate, mlin, mo, mz, q_lat, q_rope, ckv, kr, az = _front(xs, c_sample, pos_s, *fw)
        h_ml, Cs, n_s, ms = _mlstm(mlin, state_C[l].astype(f32), state_n[l].astype(f32), state_m[l].astype(f32), DS)
        o_lat = _mla_sample(q_lat, q_rope, ckv, kr, cache_ckv[l], cache_krope[l], page_table)
        ys = _back(xs, gate, h_ml, mo, mz, o_lat, az, *bw)
        ckv_s_l.append(ckv.astype(cache_ckv.dtype))
        kr_s_l.append(kr.astype(cache_krope.dtype))
        Cs_l.append(Cs.astype(state_C.dtype))
        ns_l.append(n_s.astype(state_n.dtype))
        ms_l.append(ms.astype(state_m.dtype))
        xp, xs = yp, ys
    ckv_prompt = jnp.stack(ckv_p_l)
    krope_prompt = jnp.stack(kr_p_l)
    C_prompt = jnp.stack(Cp_l)
    n_prompt = jnp.stack(np_l)
    m_prompt = jnp.stack(mp_l)
    ckv_sample = jnp.stack(ckv_s_l)
    krope_sample = jnp.stack(kr_s_l)
    C_sample = jnp.stack(Cs_l)
    n_sample = jnp.stack(ns_l)
    m_sample = jnp.stack(ms_l)
    return (xp, xs, ckv_prompt, krope_prompt, C_prompt, n_prompt, m_prompt,
            ckv_sample, krope_sample, C_sample, n_sample, m_sample)
```

```python
import functools

import numpy as np
import jax
import jax.numpy as jnp
from jax import lax
from jax.experimental import pallas as pl
from jax.experimental.pallas import tpu as pltpu

F32 = jnp.float32
BF16 = jnp.bfloat16

D_MODEL = 1024
DEPTH = 1
PAGE_SIZE = 128
ML_W = 512
MLA_W = 512
ML_DK = 128
ML_DV = 128
ML_H = 4
MLA_VD = 64
MLA_H = 8
MLA_NOPE = 64
MLA_ROPE = 32
ROPE_HALF = MLA_ROPE // 2
MLA_SCALE = (MLA_NOPE + MLA_ROPE) ** -0.5
Q_LORA = 384
KV_LORA = 256
ROPE_THETA = 10000.0
ML_CHUNK = 128
EPS = 1e-6
ALPHA = (2.0 * DEPTH) ** 0.25

LANES = 128
HEAD_PAD = 128
VMEM_LIMIT = 48 * 1024 * 1024

_SEG = {}
_off = 0
for _name, _w in (("mq", 512), ("mk", 512), ("mv", 512), ("gate", 128), ("mo", 512), ("mz", 512),
                  ("cq", Q_LORA), ("ckv", KV_LORA), ("kra", 128), ("krr", 128), ("az", 512)):
    _SEG[_name] = (_off, _off + _w)
    _off += _w
N_IN_PAD = _off

FRONT_TM = 256
ATT_TQ = 512
ATT_TK = 512
DEC_PG = 16


def _nt(a, b):
    return lax.dot_general(a, b, (((1,), (1,)), ((), ())), preferred_element_type=F32)


def _tn(a, b):
    return lax.dot_general(a, b, (((0,), (0,)), ((), ())), preferred_element_type=F32)


def _mm(a, b):
    return jnp.dot(a, b, preferred_element_type=F32)


def _mm_exact(a, b):
    return jnp.dot(a, b, preferred_element_type=F32, precision=lax.Precision.HIGHEST)


def _nt_exact(a, b):
    return lax.dot_general(a, b, (((1,), (1,)), ((), ())), preferred_element_type=F32,
                           precision=lax.Precision.HIGHEST)


def _ada_kernel(c_ref, w_ref, b_ref, o_ref):
    o_ref[...] = _mm(c_ref[...].astype(BF16), w_ref[...].astype(BF16)) + b_ref[...]


def _ada(c_all, w_ada, b_ada):
    m = c_all.shape[0]
    tn = 512
    return pl.pallas_call(
        _ada_kernel,
        grid=(3 * D_MODEL // tn,),
        in_specs=[pl.BlockSpec((m, D_MODEL), lambda j: (0, 0)),
                  pl.BlockSpec((D_MODEL, tn), lambda j: (0, j)),
                  pl.BlockSpec((1, tn), lambda j: (0, j))],
        out_specs=pl.BlockSpec((m, tn), lambda j: (0, j)),
        out_shape=jax.ShapeDtypeStruct((m, 3 * D_MODEL), F32),
        name="ada",
    )(c_all, w_ada, b_ada.reshape(1, -1))


def _rms(x, g):
    return x * lax.rsqrt(jnp.mean(x * x, axis=-1, keepdims=True) + EPS) * g


def _log_sigmoid(x):
    return jnp.minimum(x, 0.0) - jnp.log1p(jnp.exp(-jnp.abs(x)))


def _front_kernel(absorb, x_ref, sh_ref, sc_ref, cm_ref, sm_ref, win_ref, bg_ref, qn_ref, kvn_ref,
                  wqa_ref, wqr_ref, wk_ref, wv_ref,
                  mq_ref, mk_ref, mv_ref, g_ref, mo_ref, mz_ref, az_ref, ckv_ref, kr_ref,
                  o1_ref, o2_ref, o3_ref):
    tb, ts, d = x_ref.shape
    tm = tb * ts
    h = x_ref[...] * (1.0 + sc_ref[...]) + sh_ref[...]
    h = h.reshape(tm, d).astype(BF16)

    def seg(name):
        lo, hi = _SEG[name]
        return _mm(h, win_ref[:, lo:hi])

    mq_ref[...] = seg("mq")
    mk_ref[...] = seg("mk") * (ML_DK ** -0.5)
    mv_ref[...] = seg("mv")
    gz = seg("gate") + bg_ref[...]
    lane = lax.broadcasted_iota(jnp.int32, gz.shape, 1)
    g_ref[...] = jnp.where(lane < ML_H, gz, jnp.where(lane < 2 * ML_H, _log_sigmoid(gz), 0.0))
    mo_ref[...] = seg("mo")
    mz_ref[...] = seg("mz")
    az_ref[...] = seg("az")

    cm = cm_ref[...]
    sm = sm_ref[...]
    ckvn = _rms(seg("ckv"), kvn_ref[...])
    ckv_ref[...] = ckvn
    ckvn_b = ckvn.astype(BF16)
    krp = seg("kra") * cm + seg("krr") * sm
    kr_ref[...] = krp[:, MLA_NOPE:MLA_NOPE + MLA_ROPE]

    cqn = _rms(seg("cq"), qn_ref[...]).astype(BF16)
    qa = _mm(cqn, wqa_ref[...])
    qr = _mm(cqn, wqr_ref[...])
    for hh in range(MLA_H):
        sl = slice(hh * HEAD_PAD, (hh + 1) * HEAD_PAD)
        qh = ((qa[:, sl] * cm + qr[:, sl] * sm) * MLA_SCALE).astype(BF16)
        if absorb:
            o1_ref[:, hh * KV_LORA:(hh + 1) * KV_LORA] = _mm(qh, wk_ref[hh]).astype(BF16)
            o2_ref[:, sl] = qh
        else:
            o1_ref[0, hh] = qh
            o2_ref[0, hh] = (_mm(ckvn_b, wk_ref[:, sl]) + krp).astype(BF16)
    if absorb:
        o3_ref[...] = krp.astype(BF16)
    else:
        o3_ref[...] = _mm(ckvn_b, wv_ref[...]).astype(BF16)


def _front(absorb, x, mod, cm, sm, wts):
    b, s, d = x.shape
    n = b * s
    tm = FRONT_TM
    if absorb:
        tb, ts = tm // s, s
        x_map = lambda i: (i, 0, 0)
        mod_map = lambda k: (lambda i: (i, 0, k))
        tab_map = lambda i: (0, 0)
    else:
        tb, ts = 1, tm
        spb = s // tm
        x_map = lambda i: (i // spb, i % spb, 0)
        mod_map = lambda k: (lambda i: (i // spb, 0, k))
        tab_map = lambda i: (i % spb, 0)
    const2 = lambda i: (0, 0)
    tok = lambda w: pl.BlockSpec((tm, w), lambda i: (i, 0))
    in_specs = [
        pl.BlockSpec((tb, ts, d), x_map),
        pl.BlockSpec((tb, 1, d), mod_map(0)),
        pl.BlockSpec((tb, 1, d), mod_map(1)),
        pl.BlockSpec((tm, LANES), tab_map),
        pl.BlockSpec((tm, LANES), tab_map),
        pl.BlockSpec((d, N_IN_PAD), const2),
        pl.BlockSpec((1, LANES), const2),
        pl.BlockSpec((1, Q_LORA), const2),
        pl.BlockSpec((1, KV_LORA), const2),
        pl.BlockSpec((Q_LORA, MLA_H * HEAD_PAD), const2),
        pl.BlockSpec((Q_LORA, MLA_H * HEAD_PAD), const2),
    ]
    out_specs = [tok(512), tok(512), tok(512), tok(LANES), tok(512), tok(512), tok(512),
                 tok(KV_LORA), tok(MLA_ROPE)]
    out_shape = [jax.ShapeDtypeStruct((n, w), F32)
                 for w in (512, 512, 512, LANES, 512, 512, 512, KV_LORA, MLA_ROPE)]
    if absorb:
        in_specs += [pl.BlockSpec((MLA_H, HEAD_PAD, KV_LORA), lambda i: (0, 0, 0)),
                     pl.BlockSpec((KV_LORA, MLA_W), const2)]
        out_specs += [tok(MLA_H * KV_LORA), tok(MLA_H * HEAD_PAD), tok(HEAD_PAD)]
        out_shape += [jax.ShapeDtypeStruct((n, MLA_H * KV_LORA), BF16),
                      jax.ShapeDtypeStruct((n, MLA_H * HEAD_PAD), BF16),
                      jax.ShapeDtypeStruct((n, HEAD_PAD), BF16)]
        wk = wts["wukt"]
    else:
        head_map = lambda i: (i // spb, 0, i % spb, 0)
        in_specs += [pl.BlockSpec((KV_LORA, MLA_H * HEAD_PAD), const2),
                     pl.BlockSpec((KV_LORA, MLA_W), const2)]
        out_specs += [pl.BlockSpec((1, MLA_H, tm, HEAD_PAD), head_map),
                      pl.BlockSpec((1, MLA_H, tm, HEAD_PAD), head_map),
                      tok(MLA_W)]
        out_shape += [jax.ShapeDtypeStruct((b, MLA_H, s, HEAD_PAD), BF16),
                      jax.ShapeDtypeStruct((b, MLA_H, s, HEAD_PAD), BF16),
                      jax.ShapeDtypeStruct((n, MLA_W), BF16)]
        wk = wts["wk_pad"]
    return pl.pallas_call(
        functools.partial(_front_kernel, absorb),
        grid=(n // tm,),
        in_specs=in_specs,
        out_specs=out_specs,
        out_shape=out_shape,
        compiler_params=pltpu.CompilerParams(dimension_semantics=("arbitrary",),
                                             vmem_limit_bytes=VMEM_LIMIT),
        name="front_sample" if absorb else "front_prompt",
    )(x, mod, mod, cm, sm, wts["w_in"], wts["b_gate"], wts["q_norm"], wts["kv_norm"],
      wts["wqa"], wts["wqr"], wk, wts["w_uv"])


def _mlstm_kernel(q_ref, k_ref, v_ref, g_ref, c0_ref, n0_ref, m0_ref,
                  h_ref, c_ref, n_ref, m_ref):
    tb, L, _ = q_ref.shape
    ci = pl.program_id(1)

    @pl.when(ci == 0)
    def _():
        c_ref[...] = c0_ref[...]
        n_ref[...] = n0_ref[...]
        m_ref[...] = m0_ref[...]

    row = lax.broadcasted_iota(jnp.int32, (L, L), 0)
    col = lax.broadcasted_iota(jnp.int32, (L, L), 1)
    causal = col <= row
    tril = causal.astype(F32)
    sel = (lax.broadcasted_iota(jnp.int32, (8, LANES), 0)
           == lax.broadcasted_iota(jnp.int32, (8, LANES), 1)).astype(F32)

    for t in range(tb):
        g = g_ref[t]
        fcum = _mm_exact(tril, g)
        g_rows = _nt_exact(sel, g)
        f_rows = _nt_exact(sel, fcum)
        for hh in range(ML_H):
            sl = slice(hh * ML_DK, (hh + 1) * ML_DK)
            q = q_ref[t, :, sl]
            k = k_ref[t, :, sl]
            v = v_ref[t, :, sl]
            qb = q.astype(BF16)
            c_prev = c_ref[t, hh]
            n_prev = n_ref[t, hh]
            m_prev = m_ref[t, hh][:, :1]
            f_col = fcum[:, ML_H + hh:ML_H + hh + 1]
            ig_col = g[:, hh:hh + 1]
            f_row = f_rows[ML_H + hh:ML_H + hh + 1, :]
            ig_row = g_rows[hh:hh + 1, :]
            dmat = jnp.where(causal, f_col - f_row + ig_row, -jnp.inf)
            m_inter = f_col + m_prev
            m_t = jnp.maximum(m_inter, jnp.max(dmat, axis=-1, keepdims=True))
            w = jnp.exp(dmat - m_t)
            a = jnp.exp(m_inter - m_t)
            sq = _nt(qb, k.astype(BF16)) * w
            num = _mm(sq.astype(BF16), v.astype(BF16)) + a * _mm(qb, c_prev.astype(BF16))
            den = jnp.sum(sq, axis=-1, keepdims=True) + a * jnp.sum(q * n_prev, axis=-1, keepdims=True)
            h_ref[t, :, sl] = num / jnp.maximum(jnp.abs(den), jnp.exp(-m_t))
            m_last = m_t[L - 1:L, :]
            wl_col = jnp.exp(f_col[L - 1:L, :] - f_col + ig_col - m_last)
            al = a[L - 1:L, :]
            kw = k * wl_col
            c_ref[t, hh] = al * c_prev + _tn(kw.astype(BF16), v.astype(BF16))
            n_ref[t, hh] = al * n_prev + jnp.sum(kw, axis=0, keepdims=True)
            m_ref[t, hh] = jnp.broadcast_to(m_last, (1, LANES))


def _mlstm(q, k, v, g, c0, n0, m0, chunk, tb):
    b, s, w = q.shape
    nc = s // chunk
    seq = lambda ww: pl.BlockSpec((tb, chunk, ww), lambda i, c: (i, c, 0))
    st_c = pl.BlockSpec((tb, ML_H, ML_DK, ML_DV), lambda i, c: (i, 0, 0, 0))
    st_v = pl.BlockSpec((tb, ML_H, 1, LANES), lambda i, c: (i, 0, 0, 0))
    return pl.pallas_call(
        _mlstm_kernel,
        grid=(b // tb, nc),
        in_specs=[seq(w), seq(w), seq(w), seq(LANES), st_c, st_v, st_v],
        out_specs=[seq(w), st_c, st_v, st_v],
        out_shape=[jax.ShapeDtypeStruct((b, s, w), F32),
                   jax.ShapeDtypeStruct((b, ML_H, ML_DK, ML_DV), F32),
                   jax.ShapeDtypeStruct((b, ML_H, 1, LANES), F32),
                   jax.ShapeDtypeStruct((b, ML_H, 1, LANES), F32)],
        compiler_params=pltpu.CompilerParams(dimension_semantics=("arbitrary", "arbitrary"),
                                             vmem_limit_bytes=VMEM_LIMIT),
        name="mlstm",
    )(q, k, v, g, c0, n0, m0)


def _attn_kernel(qi_ref, ki_ref, last_ref, q_ref, k_ref, v_ref, o_ref, m_sc, l_sc, acc_sc):
    step = pl.program_id(2)
    qi = qi_ref[step]
    ki = ki_ref[step]
    tq = q_ref.shape[2]
    tk = k_ref.shape[2]

    @pl.when(ki == 0)
    def _():
        m_sc[...] = jnp.full(m_sc.shape, -jnp.inf, F32)
        l_sc[...] = jnp.zeros(l_sc.shape, F32)
        acc_sc[...] = jnp.zeros(acc_sc.shape, F32)

    rows = qi * tq + lax.broadcasted_iota(jnp.int32, (tq, tk), 0)
    cols = ki * tk + lax.broadcasted_iota(jnp.int32, (tq, tk), 1)
    visible = cols <= rows
    vpair = v_ref[0]
    first_head = lax.broadcasted_iota(jnp.int32, (tq, LANES), 1) < MLA_VD
    alphas, pvs = [], []
    for hh in range(2):
        s = jnp.where(visible, _nt(q_ref[0, hh], k_ref[0, hh]), -jnp.inf)
        m_prev = m_sc[hh]
        m_new = jnp.maximum(m_prev, jnp.max(s, axis=-1, keepdims=True))
        alpha = jnp.exp(m_prev - m_new)
        p = jnp.exp(s - m_new)
        l_sc[hh] = alpha * l_sc[hh] + jnp.sum(p, axis=-1, keepdims=True)
        m_sc[hh] = m_new
        alphas.append(alpha)
        pvs.append(_mm(p.astype(BF16), vpair))
    acc_sc[...] = (acc_sc[...] * jnp.where(first_head, alphas[0], alphas[1])
                   + jnp.where(first_head, pvs[0], pvs[1]))

    @pl.when(last_ref[step] == 1)
    def _():
        o_ref[0] = acc_sc[...] / jnp.where(first_head, l_sc[0], l_sc[1])


def _attn_prompt(qh, kh, vh):
    b, nh, s, _ = qh.shape
    tq, tk = ATT_TQ, ATT_TK
    qi_l, ki_l, last_l = [], [], []
    for qi in range(s // tq):
        nk = ((qi + 1) * tq + tk - 1) // tk
        for ki in range(nk):
            qi_l.append(qi)
            ki_l.append(ki)
            last_l.append(int(ki == nk - 1))
    nsteps = len(qi_l)
    sched = [jnp.asarray(np.asarray(a, np.int32)) for a in (qi_l, ki_l, last_l)]
    grid_spec = pltpu.PrefetchScalarGridSpec(
        num_scalar_prefetch=3,
        grid=(b, nh // 2, nsteps),
        in_specs=[pl.BlockSpec((1, 2, tq, HEAD_PAD), lambda bi, hp, st, qi, ki, la: (bi, hp, qi[st], 0)),
                  pl.BlockSpec((1, 2, tk, HEAD_PAD), lambda bi, hp, st, qi, ki, la: (bi, hp, ki[st], 0)),
                  pl.BlockSpec((1, tk, LANES), lambda bi, hp, st, qi, ki, la: (bi, ki[st], hp))],
        out_specs=pl.BlockSpec((1, tq, LANES), lambda bi, hp, st, qi, ki, la: (bi, qi[st], hp)),
        scratch_shapes=[pltpu.VMEM((2, tq, 1), F32), pltpu.VMEM((2, tq, 1), F32),
                        pltpu.VMEM((tq, LANES), F32)],
    )
    return pl.pallas_call(
        _attn_kernel,
        grid_spec=grid_spec,
        out_shape=jax.ShapeDtypeStruct((b, s, nh * MLA_VD), F32),
        compiler_params=pltpu.CompilerParams(
            dimension_semantics=("arbitrary", "arbitrary", "arbitrary"),
            vmem_limit_bytes=VMEM_LIMIT),
        name="attn_prompt",
    )(*sched, qh, kh, vh)


def _dec_kernel(pt_ref, qlat_ref, qh_ref, ckvn_ref, krn_ref, *rest):
    pg = DEC_PG
    ckv_refs = rest[:pg]
    kr_refs = rest[pg:2 * pg]
    o_ref, m_sc, l_sc, acc_sc, padc_sc, padk_sc = rest[2 * pg:]
    j = pl.program_id(1)
    qlat = qlat_ref[0]
    qh = qh_ref[0]
    nq = qlat.shape[0]
    ds = ckvn_ref.shape[1]

    @pl.when(j == 0)
    def _():
        padc_sc[...] = jnp.zeros(padc_sc.shape, BF16)
        padk_sc[...] = jnp.zeros(padk_sc.shape, BF16)
        padc_sc[0:ds, :] = ckvn_ref[0].astype(BF16)
        padk_sc[0:ds, :] = krn_ref[0]
        cn = padc_sc[...]
        s = _nt(qlat, cn) + _nt(qh, padk_sc[...])
        tok = lax.broadcasted_iota(jnp.int32, s.shape, 0) // MLA_H
        key = lax.broadcasted_iota(jnp.int32, s.shape, 1)
        s = jnp.where(key <= tok, s, -jnp.inf)
        m = jnp.max(s, axis=-1, keepdims=True)
        p = jnp.exp(s - m)
        m_sc[...] = m
        l_sc[...] = jnp.sum(p, axis=-1, keepdims=True)
        acc_sc[...] = _mm(p.astype(BF16), cn)

    qr = qh[:, MLA_NOPE:MLA_NOPE + MLA_ROPE]
    pages = [r[0].astype(BF16) for r in ckv_refs]
    s = jnp.concatenate(
        [_nt(qlat, pages[i]) + _nt(qr, kr_refs[i][0].astype(BF16)) for i in range(pg)], axis=1)
    m_prev = m_sc[...]
    m_new = jnp.maximum(m_prev, jnp.max(s, axis=-1, keepdims=True))
    alpha = jnp.exp(m_prev - m_new)
    p = jnp.exp(s - m_new)
    l_sc[...] = alpha * l_sc[...] + jnp.sum(p, axis=-1, keepdims=True)
    m_sc[...] = m_new
    pb = p.astype(BF16)
    acc = acc_sc[...] * alpha
    for i in range(pg):
        acc = acc + _mm(pb[:, i * PAGE_SIZE:(i + 1) * PAGE_SIZE], pages[i])
    acc_sc[...] = acc

    @pl.when(j == pl.num_programs(1) - 1)
    def _():
        o_ref[0] = acc_sc[...] / l_sc[...]


def _attn_sample(qlat, qh, ckvn, krn, pool_ckv, pool_kr, page_table):
    b, nq, _ = qlat.shape
    ds = ckvn.shape[1]
    n_pages = page_table.shape[1]
    pg = DEC_PG
    pt = page_table.reshape(-1)

    def page_map(i):
        return lambda bi, j, pt_ref: (pt_ref[bi * n_pages + j * pg + i], 0, 0)

    fixed = lambda w, r: pl.BlockSpec((1, r, w), lambda bi, j, pt_ref: (bi, 0, 0))
    grid_spec = pltpu.PrefetchScalarGridSpec(
        num_scalar_prefetch=1,
        grid=(b, n_pages // pg),
        in_specs=([fixed(KV_LORA, nq), fixed(HEAD_PAD, nq), fixed(KV_LORA, ds), fixed(HEAD_PAD, ds)]
                  + [pl.BlockSpec((1, PAGE_SIZE, KV_LORA), page_map(i)) for i in range(pg)]
                  + [pl.BlockSpec((1, PAGE_SIZE, MLA_ROPE), page_map(i)) for i in range(pg)]),
        out_specs=fixed(KV_LORA, nq),
        scratch_shapes=[pltpu.VMEM((nq, 1), F32), pltpu.VMEM((nq, 1), F32),
                        pltpu.VMEM((nq, KV_LORA), F32),
                        pltpu.VMEM((PAGE_SIZE, KV_LORA), BF16), pltpu.VMEM((PAGE_SIZE, HEAD_PAD), BF16)],
    )
    return pl.pallas_call(
        _dec_kernel,
        grid_spec=grid_spec,
        out_shape=jax.ShapeDtypeStruct((b, nq, KV_LORA), F32),
        compiler_params=pltpu.CompilerParams(dimension_semantics=("arbitrary", "arbitrary"),
                                             vmem_limit_bytes=VMEM_LIMIT),
        name="attn_sample",
    )(pt, qlat, qh, ckvn, krn, *([pool_ckv] * pg), *([pool_kr] * pg))


def _back_kernel(absorb, x_ref, gate_ref, hml_ref, mo_ref, mz_ref, o_ref, az_ref, gn_ref, wuv_ref,
                 wout_ref, lng_ref, lnb_ref, y_ref):
    tb, ts, d = x_ref.shape
    tm = tb * ts
    hm = hml_ref[...] * jax.nn.sigmoid(mo_ref[...])
    parts = []
    for hh in range(ML_H):
        t = hm[:, hh * ML_DV:(hh + 1) * ML_DV]
        mu = jnp.mean(t, axis=-1, keepdims=True)
        tc = t - mu
        var = jnp.mean(tc * tc, axis=-1, keepdims=True)
        parts.append(tc * lax.rsqrt(var + EPS))
    mz = mz_ref[...]
    y_ml = jnp.concatenate(parts, axis=1) * gn_ref[...] * (mz * jax.nn.sigmoid(mz))
    if absorb:
        olat = o_ref[...].astype(BF16)
        o_mla = jnp.concatenate(
            [_mm(olat[:, p * 2 * KV_LORA:(p + 1) * 2 * KV_LORA], wuv_ref[p]) for p in range(MLA_H // 2)],
            axis=1)
    else:
        o_mla = o_ref[...]
    az = az_ref[...]
    y_mla = o_mla * (az * jax.nn.sigmoid(az))
    out = _mm(y_ml.astype(BF16), wout_ref[0:ML_W, :]) + _mm(y_mla.astype(BF16), wout_ref[ML_W:, :])
    z = ALPHA * x_ref[...] + gate_ref[...] * out.reshape(tb, ts, d)
    mu = jnp.mean(z, axis=-1, keepdims=True)
    zc = z - mu
    var = jnp.mean(zc * zc, axis=-1, keepdims=True)
    y_ref[...] = zc * lax.rsqrt(var + EPS) * lng_ref[...] + lnb_ref[...]


def _back(absorb, x, mod, hml, mo, mz, o, az, wts):
    b, s, d = x.shape
    n = b * s
    tm = FRONT_TM
    if absorb:
        tb, ts = tm // s, s
        x_map = lambda i: (i, 0, 0)
        gate_map = lambda i: (i, 0, 2)
    else:
        tb, ts = 1, tm
        spb = s // tm
        x_map = lambda i: (i // spb, i % spb, 0)
        gate_map = lambda i: (i // spb, 0, 2)
    const2 = lambda i: (0, 0)
    tok = lambda w: pl.BlockSpec((tm, w), lambda i: (i, 0))
    ow = o.shape[1]
    return pl.pallas_call(
        functools.partial(_back_kernel, absorb),
        grid=(n // tm,),
        in_specs=[pl.BlockSpec((tb, ts, d), x_map),
                  pl.BlockSpec((tb, 1, d), gate_map),
                  tok(ML_W), tok(ML_W), tok(ML_W), tok(ow), tok(MLA_W),
                  pl.BlockSpec((1, ML_W), const2),
                  pl.BlockSpec((MLA_H // 2, 2 * KV_LORA, LANES), lambda i: (0, 0, 0)),
                  pl.BlockSpec((d, d), const2),
                  pl.BlockSpec((1, d), const2),
                  pl.BlockSpec((1, d), const2)],
        out_specs=pl.BlockSpec((tb, ts, d), x_map),
        out_shape=jax.ShapeDtypeStruct((b, s, d), F32),
        compiler_params=pltpu.CompilerParams(dimension_semantics=("arbitrary",),
                                             vmem_limit_bytes=VMEM_LIMIT),
        name="back_sample" if absorb else "back_prompt",
    )(x, mod, hml, mo, mz, o, az, wts["gn"], wts["wuv_pair"], wts["w_out"], wts["ln_g"], wts["ln_b"])


def _prep_weights(l, w_in, ml_b_i, ml_b_f, ml_gn, mla_q_norm, mla_kv_norm, mla_w_uq, mla_w_uk,
                  mla_w_uv, w_out, ln_g, ln_b):
    offs = np.cumsum((0, 512, 512, 512, ML_H, ML_H, 512, 512, Q_LORA, KV_LORA, MLA_ROPE, 512))
    names = ("mq", "mk", "mv", "mi", "mf", "mo", "mz", "cq", "ckv", "kr", "az")
    w = {nm: w_in[l][:, offs[i]:offs[i + 1]] for i, nm in enumerate(names)}
    d = w_in.shape[1]
    z = lambda c: jnp.zeros((d, c), F32)
    kr1, kr2 = w["kr"][:, :ROPE_HALF], w["kr"][:, ROPE_HALF:]
    tail = HEAD_PAD - MLA_NOPE - MLA_ROPE
    w_in_p = jnp.concatenate(
        [w["mq"], w["mk"], w["mv"], w["mi"], w["mf"], z(LANES - 2 * ML_H), w["mo"], w["mz"], w["cq"], w["ckv"],
         z(MLA_NOPE), kr1, kr2, z(tail), z(MLA_NOPE), -kr2, kr1, z(tail), w["az"]], axis=1).astype(BF16)
    b_gate = jnp.concatenate([ml_b_i[l], ml_b_f[l], jnp.zeros((LANES - 2 * ML_H,), F32)]).reshape(1, LANES)
    uq = mla_w_uq[l].reshape(Q_LORA, MLA_H, MLA_NOPE + MLA_ROPE)
    nope, r1, r2 = uq[..., :MLA_NOPE], uq[..., MLA_NOPE:MLA_NOPE + ROPE_HALF], uq[..., MLA_NOPE + ROPE_HALF:]
    zq = lambda c: jnp.zeros((Q_LORA, MLA_H, c), F32)
    wqa = jnp.concatenate([nope, r1, r2, zq(tail)], axis=-1).reshape(Q_LORA, MLA_H * HEAD_PAD).astype(BF16)
    wqr = jnp.concatenate([zq(MLA_NOPE), -r2, r1, zq(tail)], axis=-1).reshape(Q_LORA, MLA_H * HEAD_PAD).astype(BF16)
    uk = mla_w_uk[l].reshape(KV_LORA, MLA_H, MLA_NOPE)
    wk_pad = jnp.concatenate([uk, jnp.zeros((KV_LORA, MLA_H, HEAD_PAD - MLA_NOPE), F32)],
                             axis=-1).reshape(KV_LORA, MLA_H * HEAD_PAD).astype(BF16)
    wukt = jnp.concatenate([jnp.transpose(uk, (1, 2, 0)),
                            jnp.zeros((MLA_H, HEAD_PAD - MLA_NOPE, KV_LORA), F32)], axis=1).astype(BF16)
    uv = mla_w_uv[l].reshape(KV_LORA, MLA_H // 2, 2, MLA_VD)
    zv = jnp.zeros((KV_LORA, MLA_H // 2, MLA_VD), F32)
    wuv_pair = jnp.concatenate(
        [jnp.concatenate([uv[:, :, 0], zv], axis=-1), jnp.concatenate([zv, uv[:, :, 1]], axis=-1)],
        axis=0)
    wuv_pair = jnp.transpose(wuv_pair, (1, 0, 2)).astype(BF16)
    return dict(w_in=w_in_p, b_gate=b_gate, q_norm=mla_q_norm[l].reshape(1, -1),
                kv_norm=mla_kv_norm[l].reshape(1, -1), wqa=wqa, wqr=wqr, wk_pad=wk_pad, wukt=wukt,
                w_uv=mla_w_uv[l].astype(BF16), wuv_pair=wuv_pair, gn=ml_gn[l].reshape(1, -1),
                w_out=w_out[l].astype(BF16), ln_g=ln_g[l].reshape(1, -1), ln_b=ln_b[l].reshape(1, -1))


def _rope_tables(pos):
    inv = ROPE_THETA ** (-jnp.arange(ROPE_HALF, dtype=F32) / ROPE_HALF)
    ang = pos.astype(F32)[:, None] * inv[None, :]
    n = pos.shape[0]
    tail = HEAD_PAD - MLA_NOPE - MLA_ROPE
    cm = jnp.concatenate([jnp.ones((n, MLA_NOPE), F32), jnp.cos(ang), jnp.cos(ang), jnp.zeros((n, tail), F32)], 1)
    sm = jnp.concatenate([jnp.zeros((n, MLA_NOPE), F32), jnp.sin(ang), jnp.sin(ang), jnp.zeros((n, tail), F32)], 1)
    return cm, sm


def kernel(x_prompt, x_sample, c_prompt, c_sample, cache_ckv, cache_krope, state_C, state_n, state_m,
           page_table, w_ada, b_ada, w_in, ml_b_i, ml_b_f, ml_gn, mla_q_norm, mla_kv_norm,
           mla_w_uq, mla_w_uk, mla_w_uv, w_out, ln_g, ln_b):
    bp, sp, d = x_prompt.shape
    bs, ss, _ = x_sample.shape
    past = page_table.shape[1] * PAGE_SIZE
    cm_p, sm_p = _rope_tables(jnp.arange(sp))
    cm_s, sm_s = _rope_tables(past + jnp.arange(ss))
    reps = FRONT_TM // ss
    cm_s, sm_s = jnp.tile(cm_s, (reps, 1)), jnp.tile(sm_s, (reps, 1))
    pad = (-(bp + bs)) % 8
    c_all = jnp.concatenate([c_prompt, c_sample, jnp.zeros((pad, d), F32)], axis=0)

    xp, xs = x_prompt, x_sample
    outs = [[] for _ in range(10)]
    for l in range(DEPTH):
        wts = _prep_weights(l, w_in, ml_b_i, ml_b_f, ml_gn, mla_q_norm, mla_kv_norm, mla_w_uq, mla_w_uk,
                            mla_w_uv, w_out, ln_g, ln_b)
        mod = _ada(c_all, w_ada[l], b_ada[l])
        mod_p = mod[:bp].reshape(bp, 1, 3 * d)
        mod_s = mod[bp:bp + bs].reshape(bs, 1, 3 * d)

        mq, mk, mv, g, mo, mz, az, ckv, kr, qh, kh, vh = _front(False, xp, mod_p, cm_p, sm_p, wts)
        seq3 = lambda t: t.reshape(bp, sp, -1)
        c0 = jnp.zeros((bp, ML_H, ML_DK, ML_DV), F32)
        n0 = jnp.zeros((bp, ML_H, 1, ML_DK), F32)
        m0 = jnp.full((bp, ML_H, 1, LANES), -jnp.inf, F32)
        hml, c_p, n_p, m_p = _mlstm(seq3(mq), seq3(mk), seq3(mv), seq3(g), c0, n0, m0, ML_CHUNK, 1)
        o_p = _attn_prompt(qh, kh, seq3(vh))
        yp = _back(False, xp, mod_p, hml.reshape(bp * sp, -1), mo, mz, o_p.reshape(bp * sp, -1), az, wts)
        outs[0].append(ckv.reshape(bp, sp, KV_LORA))
        outs[1].append(kr.reshape(bp, sp, MLA_ROPE))
        outs[2].append(c_p)
        outs[3].append(n_p[:, :, 0, :])
        outs[4].append(m_p[:, :, 0, 0])

        mq, mk, mv, g, mo, mz, az, ckv, kr, qlat, qh, krp = _front(True, xs, mod_s, cm_s, sm_s, wts)
        seq3 = lambda t: t.reshape(bs, ss, -1)
        n0 = state_n[l].astype(F32).reshape(bs, ML_H, 1, ML_DK)
        m0 = jnp.broadcast_to(state_m[l].astype(F32)[:, :, None, None], (bs, ML_H, 1, LANES))
        hml, c_s, n_s, m_s = _mlstm(seq3(mq), seq3(mk), seq3(mv), seq3(g), state_C[l].astype(F32), n0, m0, ss, 8)
        o_s = _attn_sample(qlat.reshape(bs, ss * MLA_H, KV_LORA), qh.reshape(bs, ss * MLA_H, HEAD_PAD),
                           seq3(ckv), seq3(krp), cache_ckv[l], cache_krope[l], page_table)
        ys = _back(True, xs, mod_s, hml.reshape(bs * ss, -1), mo, mz,
                   o_s.reshape(bs * ss, MLA_H * KV_LORA), az, wts)
        outs[5].append(ckv.reshape(bs, ss, KV_LORA).astype(cache_ckv.dtype))
        outs[6].append(kr.reshape(bs, ss, MLA_ROPE).astype(cache_krope.dtype))
        outs[7].append(c_s.astype(state_C.dtype))
        outs[8].append(n_s[:, :, 0, :].astype(state_n.dtype))
        outs[9].append(m_s[:, :, 0, 0].astype(state_m.dtype))
        xp, xs = yp, ys
    return (xp, xs) + tuple(jnp.stack(o) for o in outs)
```

```python
import functools

import numpy as np
import jax
import jax.numpy as jnp
from jax import lax
from jax.experimental import pallas as pl
from jax.experimental.pallas import tpu as pltpu

F32 = jnp.float32
BF16 = jnp.bfloat16

D_MODEL = 1024
DEPTH = 1
PAGE_SIZE = 128
ML_W = 512
MLA_W = 512
ML_DK = 128
ML_DV = 128
ML_H = 4
MLA_VD = 64
MLA_H = 8
MLA_NOPE = 64
MLA_ROPE = 32
ROPE_HALF = MLA_ROPE // 2
MLA_SCALE = (MLA_NOPE + MLA_ROPE) ** -0.5
Q_SCALE = MLA_SCALE * float(np.log2(np.e))
Q_LORA = 384
KV_LORA = 256
ROPE_THETA = 10000.0
ML_CHUNK = 128
EPS = 1e-6
ALPHA = (2.0 * DEPTH) ** 0.25

LANES = 128
HEAD_PAD = 128
VMEM_LIMIT = 48 * 1024 * 1024

_SEG = {}
_off = 0
for _name, _w in (("mq", 512), ("mk", 512), ("mv", 512), ("gate", 128), ("mo", 512), ("mz", 512),
                  ("cq", Q_LORA), ("ckv", KV_LORA), ("kra", 128), ("krr", 128), ("az", 512)):
    _SEG[_name] = (_off, _off + _w)
    _off += _w
N_IN_PAD = _off

FRONT_TM = 256
ATT_TQ = 512
ATT_TK = 512
ATT_QC = 512
DEC_PG = 16


def _nt(a, b):
    return lax.dot_general(a, b, (((1,), (1,)), ((), ())), preferred_element_type=F32)


def _tn(a, b):
    return lax.dot_general(a, b, (((0,), (0,)), ((), ())), preferred_element_type=F32)


def _mm(a, b):
    return jnp.dot(a, b, preferred_element_type=F32)


def _mm_exact(a, b):
    return jnp.dot(a, b, preferred_element_type=F32, precision=lax.Precision.HIGHEST)


def _nt_exact(a, b):
    return lax.dot_general(a, b, (((1,), (1,)), ((), ())), preferred_element_type=F32,
                           precision=lax.Precision.HIGHEST)


def _ada_kernel(c_ref, w_ref, b_ref, o_ref):
    o_ref[...] = _mm(c_ref[...].astype(BF16), w_ref[...].astype(BF16)) + b_ref[...]


def _ada(c_all, w_ada, b_ada):
    m = c_all.shape[0]
    tn = 512
    return pl.pallas_call(
        _ada_kernel,
        grid=(3 * D_MODEL // tn,),
        in_specs=[pl.BlockSpec((m, D_MODEL), lambda j: (0, 0)),
                  pl.BlockSpec((D_MODEL, tn), lambda j: (0, j)),
                  pl.BlockSpec((1, tn), lambda j: (0, j))],
        out_specs=pl.BlockSpec((m, tn), lambda j: (0, j)),
        out_shape=jax.ShapeDtypeStruct((m, 3 * D_MODEL), F32),
        name="ada",
    )(c_all, w_ada, b_ada.reshape(1, -1))


def _rms(x, g):
    return x * lax.rsqrt(jnp.mean(x * x, axis=-1, keepdims=True) + EPS) * g


def _log_sigmoid(x):
    return jnp.minimum(x, 0.0) - jnp.log1p(jnp.exp(-jnp.abs(x)))


def _front_kernel(absorb, x_ref, sh_ref, sc_ref, cm_ref, sm_ref, win_ref, bg_ref, qn_ref, kvn_ref,
                  wqa_ref, wqr_ref, wk_ref, wv_ref,
                  mq_ref, mk_ref, mv_ref, g_ref, mo_ref, mz_ref, az_ref, ckv_ref, kr_ref,
                  o1_ref, o2_ref, o3_ref):
    tb, ts, d = x_ref.shape
    tm = tb * ts
    h = x_ref[...] * (1.0 + sc_ref[...]) + sh_ref[...]
    h = h.reshape(tm, d).astype(BF16)

    def seg(name):
        lo, hi = _SEG[name]
        return _mm(h, win_ref[:, lo:hi])

    mq_ref[...] = seg("mq")
    mk_ref[...] = seg("mk") * (ML_DK ** -0.5)
    mv_ref[...] = seg("mv")
    gz = seg("gate") + bg_ref[...]
    lane = lax.broadcasted_iota(jnp.int32, gz.shape, 1)
    g_ref[...] = jnp.where(lane < ML_H, gz, jnp.where(lane < 2 * ML_H, _log_sigmoid(gz), 0.0))
    mo_ref[...] = seg("mo")
    mz_ref[...] = seg("mz")
    az_ref[...] = seg("az")

    cm = cm_ref[...]
    sm = sm_ref[...]
    ckvn = _rms(seg("ckv"), kvn_ref[...])
    ckv_ref[...] = ckvn
    ckvn_b = ckvn.astype(BF16)
    krp = seg("kra") * cm + seg("krr") * sm
    kr_ref[...] = krp[:, MLA_NOPE:MLA_NOPE + MLA_ROPE]

    cqn = _rms(seg("cq"), qn_ref[...]).astype(BF16)
    qa = _mm(cqn, wqa_ref[...])
    qr = _mm(cqn, wqr_ref[...])
    for hh in range(MLA_H):
        sl = slice(hh * HEAD_PAD, (hh + 1) * HEAD_PAD)
        qh = ((qa[:, sl] * cm + qr[:, sl] * sm) * Q_SCALE).astype(BF16)
        if absorb:
            o1_ref[:, hh * KV_LORA:(hh + 1) * KV_LORA] = _mm(qh, wk_ref[hh]).astype(BF16)
            o2_ref[:, sl] = qh
        else:
            o1_ref[0, hh] = qh
            o2_ref[0, hh] = (_mm(ckvn_b, wk_ref[:, sl]) + krp).astype(BF16)
    if absorb:
        o3_ref[...] = krp.astype(BF16)
    else:
        o3_ref[0] = _nt(wv_ref[...], ckvn_b).astype(BF16)


def _front(absorb, x, mod, cm, sm, wts):
    b, s, d = x.shape
    n = b * s
    tm = FRONT_TM
    if absorb:
        tb, ts = tm // s, s
        x_map = lambda i: (i, 0, 0)
        mod_map = lambda k: (lambda i: (i, 0, k))
        tab_map = lambda i: (0, 0)
    else:
        tb, ts = 1, tm
        spb = s // tm
        x_map = lambda i: (i // spb, i % spb, 0)
        mod_map = lambda k: (lambda i: (i // spb, 0, k))
        tab_map = lambda i: (i % spb, 0)
    const2 = lambda i: (0, 0)
    tok = lambda w: pl.BlockSpec((tm, w), lambda i: (i, 0))
    in_specs = [
        pl.BlockSpec((tb, ts, d), x_map),
        pl.BlockSpec((tb, 1, d), mod_map(0)),
        pl.BlockSpec((tb, 1, d), mod_map(1)),
        pl.BlockSpec((tm, LANES), tab_map),
        pl.BlockSpec((tm, LANES), tab_map),
        pl.BlockSpec((d, N_IN_PAD), const2),
        pl.BlockSpec((1, LANES), const2),
        pl.BlockSpec((1, Q_LORA), const2),
        pl.BlockSpec((1, KV_LORA), const2),
        pl.BlockSpec((Q_LORA, MLA_H * HEAD_PAD), const2),
        pl.BlockSpec((Q_LORA, MLA_H * HEAD_PAD), const2),
    ]
    out_specs = [tok(512), tok(512), tok(512), tok(LANES), tok(512), tok(512), tok(512),
                 tok(KV_LORA), tok(MLA_ROPE)]
    out_shape = [jax.ShapeDtypeStruct((n, w), F32)
                 for w in (512, 512, 512, LANES, 512, 512, 512, KV_LORA, MLA_ROPE)]
    if absorb:
        in_specs += [pl.BlockSpec((MLA_H, HEAD_PAD, KV_LORA), lambda i: (0, 0, 0)),
                     pl.BlockSpec((MLA_W, KV_LORA), const2)]
        out_specs += [tok(MLA_H * KV_LORA), tok(MLA_H * HEAD_PAD), tok(HEAD_PAD)]
        out_shape += [jax.ShapeDtypeStruct((n, MLA_H * KV_LORA), BF16),
                      jax.ShapeDtypeStruct((n, MLA_H * HEAD_PAD), BF16),
                      jax.ShapeDtypeStruct((n, HEAD_PAD), BF16)]
        wk = wts["wukt"]
    else:
        head_map = lambda i: (i // spb, 0, i % spb, 0)
        in_specs += [pl.BlockSpec((KV_LORA, MLA_H * HEAD_PAD), const2),
                     pl.BlockSpec((MLA_W, KV_LORA), const2)]
        out_specs += [pl.BlockSpec((1, MLA_H, tm, HEAD_PAD), head_map),
                      pl.BlockSpec((1, MLA_H, tm, HEAD_PAD), head_map),
                      pl.BlockSpec((1, MLA_W, tm), lambda i: (i // spb, 0, i % spb))]
        out_shape += [jax.ShapeDtypeStruct((b, MLA_H, s, HEAD_PAD), BF16),
                      jax.ShapeDtypeStruct((b, MLA_H, s, HEAD_PAD), BF16),
                      jax.ShapeDtypeStruct((b, MLA_W, s), BF16)]
        wk = wts["wk_pad"]
    return pl.pallas_call(
        functools.partial(_front_kernel, absorb),
        grid=(n // tm,),
        in_specs=in_specs,
        out_specs=out_specs,
        out_shape=out_shape,
        compiler_params=pltpu.CompilerParams(dimension_semantics=("arbitrary",),
                                             vmem_limit_bytes=VMEM_LIMIT),
        name="front_sample" if absorb else "front_prompt",
    )(x, mod, mod, cm, sm, wts["w_in"], wts["b_gate"], wts["q_norm"], wts["kv_norm"],
      wts["wqa"], wts["wqr"], wk, wts["w_uv_t"])


def _mlstm_kernel(q_ref, k_ref, v_ref, g_ref, c0_ref, n0_ref, m0_ref,
                  h_ref, c_ref, n_ref, m_ref):
    tb, L, _ = q_ref.shape
    ci = pl.program_id(1)

    @pl.when(ci == 0)
    def _():
        c_ref[...] = c0_ref[...]
        n_ref[...] = n0_ref[...]
        m_ref[...] = m0_ref[...]

    row = lax.broadcasted_iota(jnp.int32, (L, L), 0)
    col = lax.broadcasted_iota(jnp.int32, (L, L), 1)
    causal = col <= row
    tril = causal.astype(F32)
    sel = (lax.broadcasted_iota(jnp.int32, (8, LANES), 0)
           == lax.broadcasted_iota(jnp.int32, (8, LANES), 1)).astype(F32)

    for t in range(tb):
        g = g_ref[t]
        fcum = _mm_exact(tril, g)
        g_rows = _nt_exact(sel, g)
        f_rows = _nt_exact(sel, fcum)
        for hh in range(ML_H):
            sl = slice(hh * ML_DK, (hh + 1) * ML_DK)
            q = q_ref[t, :, sl]
            k = k_ref[t, :, sl]
            v = v_ref[t, :, sl]
            qb = q.astype(BF16)
            c_prev = c_ref[t, hh]
            n_prev = n_ref[t, hh]
            m_prev = m_ref[t, hh][:, :1]
            f_col = fcum[:, ML_H + hh:ML_H + hh + 1]
            ig_col = g[:, hh:hh + 1]
            f_row = f_rows[ML_H + hh:ML_H + hh + 1, :]
            ig_row = g_rows[hh:hh + 1, :]
            dmat = jnp.where(causal, f_col - f_row + ig_row, -jnp.inf)
            m_inter = f_col + m_prev
            m_t = jnp.maximum(m_inter, jnp.max(dmat, axis=-1, keepdims=True))
            w = jnp.exp(dmat - m_t)
            a = jnp.exp(m_inter - m_t)
            sq = _nt(qb, k.astype(BF16)) * w
            num = _mm(sq.astype(BF16), v.astype(BF16)) + a * _mm(qb, c_prev.astype(BF16))
            den = jnp.sum(sq, axis=-1, keepdims=True) + a * jnp.sum(q * n_prev, axis=-1, keepdims=True)
            h_ref[t, :, sl] = num / jnp.maximum(jnp.abs(den), jnp.exp(-m_t))
            m_last = m_t[L - 1:L, :]
            wl_col = jnp.exp(f_col[L - 1:L, :] - f_col + ig_col - m_last)
            al = a[L - 1:L, :]
            kw = k * wl_col
            c_ref[t, hh] = al * c_prev + _tn(kw.astype(BF16), v.astype(BF16))
            n_ref[t, hh] = al * n_prev + jnp.sum(kw, axis=0, keepdims=True)
            m_ref[t, hh] = jnp.broadcast_to(m_last, (1, LANES))


def _mlstm(q, k, v, g, c0, n0, m0, chunk, tb):
    b, s, w = q.shape
    nc = s // chunk
    seq = lambda ww: pl.BlockSpec((tb, chunk, ww), lambda i, c: (i, c, 0))
    st_c = pl.BlockSpec((tb, ML_H, ML_DK, ML_DV), lambda i, c: (i, 0, 0, 0))
    st_v = pl.BlockSpec((tb, ML_H, 1, LANES), lambda i, c: (i, 0, 0, 0))
    return pl.pallas_call(
        _mlstm_kernel,
        grid=(b // tb, nc),
        in_specs=[seq(w), seq(w), seq(w), seq(LANES), st_c, st_v, st_v],
        out_specs=[seq(w), st_c, st_v, st_v],
        out_shape=[jax.ShapeDtypeStruct((b, s, w), F32),
                   jax.ShapeDtypeStruct((b, ML_H, ML_DK, ML_DV), F32),
                   jax.ShapeDtypeStruct((b, ML_H, 1, LANES), F32),
                   jax.ShapeDtypeStruct((b, ML_H, 1, LANES), F32)],
        compiler_params=pltpu.CompilerParams(dimension_semantics=("arbitrary", "arbitrary"),
                                             vmem_limit_bytes=VMEM_LIMIT),
        name="mlstm",
    )(q, k, v, g, c0, n0, m0)


def _attn_kernel(qi_ref, ki_ref, last_ref, q_ref, k_ref, vt_ref, o_ref, m_sc, l_sc, acc_sc):
    step = pl.program_id(2)
    qi = qi_ref[step]
    ki = ki_ref[step]
    tq = q_ref.shape[2]
    tk = k_ref.shape[2]

    @pl.when(ki == 0)
    def _():
        m_sc[...] = jnp.full(m_sc.shape, -jnp.inf, F32)
        l_sc[...] = jnp.zeros(l_sc.shape, F32)
        acc_sc[...] = jnp.zeros(acc_sc.shape, F32)

    def tile(masked):
        m_all, l_all, acc_all = m_sc[...], l_sc[...], acc_sc[...]
        m_out, l_out, acc_out = [], [], []
        for hh in range(2):
            rows = slice(hh * MLA_VD, (hh + 1) * MLA_VD)
            k = k_ref[0, hh]
            vt = vt_ref[0, rows, :]
            m_h, l_h, acc_h = [], [], []
            for c0 in range(0, tq, ATT_QC):
                qs = slice(c0, c0 + ATT_QC)
                st = _nt(k, q_ref[0, hh, qs, :])
                if masked:
                    keys = ki * tk + lax.broadcasted_iota(jnp.int32, (tk, ATT_QC), 0)
                    qrys = qi * tq + c0 + lax.broadcasted_iota(jnp.int32, (tk, ATT_QC), 1)
                    st = jnp.where(keys <= qrys, st, -jnp.inf)
                m_prev = m_all[hh, :, qs]
                m_new = jnp.maximum(m_prev, jnp.max(st, axis=0, keepdims=True))
                alpha = jnp.exp2(m_prev - m_new)
                pt = jnp.exp2(st - m_new)
                l_h.append(alpha * l_all[hh, :, qs] + jnp.sum(pt, axis=0, keepdims=True))
                m_h.append(m_new)
                acc_h.append(acc_all[rows, qs] * alpha + _mm(vt, pt.astype(BF16)))
            m_out.append(jnp.concatenate(m_h, axis=1))
            l_out.append(jnp.concatenate(l_h, axis=1))
            acc_out.append(jnp.concatenate(acc_h, axis=1))
        m_sc[...] = jnp.stack(m_out)
        l_sc[...] = jnp.stack(l_out)
        acc_sc[...] = jnp.concatenate(acc_out, axis=0)

    crosses_diagonal = (ki + 1) * tk - 1 > qi * tq
    pl.when(crosses_diagonal)(functools.partial(tile, True))
    pl.when(jnp.logical_not(crosses_diagonal))(functools.partial(tile, False))

    @pl.when(last_ref[step] == 1)
    def _():
        out_t = jnp.concatenate([acc_sc[0:MLA_VD, :] / l_sc[0], acc_sc[MLA_VD:, :] / l_sc[1]], axis=0)
        o_ref[0] = out_t.T


def _attn_prompt(qh, kh, vt):
    b, nh, s, _ = qh.shape
    tq, tk = ATT_TQ, ATT_TK
    qi_l, ki_l, last_l = [], [], []
    for qi in range(s // tq):
        nk = ((qi + 1) * tq + tk - 1) // tk
        for ki in range(nk):
            qi_l.append(qi)
            ki_l.append(ki)
            last_l.append(int(ki == nk - 1))
    nsteps = len(qi_l)
    sched = [jnp.asarray(np.asarray(a, np.int32)) for a in (qi_l, ki_l, last_l)]
    grid_spec = pltpu.PrefetchScalarGridSpec(
        num_scalar_prefetch=3,
        grid=(b, nh // 2, nsteps),
        in_specs=[pl.BlockSpec((1, 2, tq, HEAD_PAD), lambda bi, hp, st, qi, ki, la: (bi, hp, qi[st], 0)),
                  pl.BlockSpec((1, 2, tk, HEAD_PAD), lambda bi, hp, st, qi, ki, la: (bi, hp, ki[st], 0)),
                  pl.BlockSpec((1, 2 * MLA_VD, tk), lambda bi, hp, st, qi, ki, la: (bi, hp, ki[st]))],
        out_specs=pl.BlockSpec((1, tq, LANES), lambda bi, hp, st, qi, ki, la: (bi, qi[st], hp)),
        scratch_shapes=[pltpu.VMEM((2, 1, tq), F32), pltpu.VMEM((2, 1, tq), F32),
                        pltpu.VMEM((2 * MLA_VD, tq), F32)],
    )
    return pl.pallas_call(
        _attn_kernel,
        grid_spec=grid_spec,
        out_shape=jax.ShapeDtypeStruct((b, s, nh * MLA_VD), F32),
        compiler_params=pltpu.CompilerParams(
            dimension_semantics=("arbitrary", "arbitrary", "arbitrary"),
            vmem_limit_bytes=VMEM_LIMIT),
        name="attn_prompt",
    )(*sched, qh, kh, vt)


def _dec_kernel(pt_ref, qlat_ref, qh_ref, ckvn_ref, krn_ref, *rest):
    pg = DEC_PG
    ckv_refs = rest[:pg]
    kr_refs = rest[pg:2 * pg]
    o_ref, m_sc, l_sc, acc_sc, padc_sc, padk_sc = rest[2 * pg:]
    j = pl.program_id(1)
    qlat = qlat_ref[0]
    qh = qh_ref[0]
    nq = qlat.shape[0]
    ds = ckvn_ref.shape[1]

    @pl.when(j == 0)
    def _():
        padc_sc[...] = jnp.zeros(padc_sc.shape, BF16)
        padk_sc[...] = jnp.zeros(padk_sc.shape, BF16)
        padc_sc[0:ds, :] = ckvn_ref[0].astype(BF16)
        padk_sc[0:ds, :] = krn_ref[0]
        cn = padc_sc[...]
        s = _nt(qlat, cn) + _nt(qh, padk_sc[...])
        tok = lax.broadcasted_iota(jnp.int32, s.shape, 0) // MLA_H
        key = lax.broadcasted_iota(jnp.int32, s.shape, 1)
        s = jnp.where(key <= tok, s, -jnp.inf)
        m = jnp.max(s, axis=-1, keepdims=True)
        p = jnp.exp2(s - m)
        m_sc[...] = m
        l_sc[...] = jnp.sum(p, axis=-1, keepdims=True)
        acc_sc[...] = _mm(p.astype(BF16), cn)

    qr = qh[:, MLA_NOPE:MLA_NOPE + MLA_ROPE]
    pages = [r[0].astype(BF16) for r in ckv_refs]
    s = jnp.concatenate(
        [_nt(qlat, pages[i]) + _mm(qr, kr_refs[i][0].astype(BF16)) for i in range(pg)], axis=1)
    m_prev = m_sc[...]
    m_new = jnp.maximum(m_prev, jnp.max(s, axis=-1, keepdims=True))
    alpha = jnp.exp2(m_prev - m_new)
    p = jnp.exp2(s - m_new)
    l_sc[...] = alpha * l_sc[...] + jnp.sum(p, axis=-1, keepdims=True)
    m_sc[...] = m_new
    pb = p.astype(BF16)
    acc = acc_sc[...] * alpha
    for i in range(pg):
        acc = acc + _mm(pb[:, i * PAGE_SIZE:(i + 1) * PAGE_SIZE], pages[i])
    acc_sc[...] = acc

    @pl.when(j == pl.num_programs(1) - 1)
    def _():
        o_ref[0] = acc_sc[...] / l_sc[...]


def _attn_sample(qlat, qh, ckvn, krn, pool_ckv, pool_kr, page_table):
    b, nq, _ = qlat.shape
    ds = ckvn.shape[1]
    n_pages = page_table.shape[1]
    pg = DEC_PG
    pt = page_table.reshape(-1)

    def page_map(i):
        return lambda bi, j, pt_ref: (pt_ref[bi * n_pages + j * pg + i], 0, 0)

    fixed = lambda w, r: pl.BlockSpec((1, r, w), lambda bi, j, pt_ref: (bi, 0, 0))
    grid_spec = pltpu.PrefetchScalarGridSpec(
        num_scalar_prefetch=1,
        grid=(b, n_pages // pg),
        in_specs=([fixed(KV_LORA, nq), fixed(HEAD_PAD, nq), fixed(KV_LORA, ds), fixed(HEAD_PAD, ds)]
                  + [pl.BlockSpec((1, PAGE_SIZE, KV_LORA), page_map(i)) for i in range(pg)]
                  + [pl.BlockSpec((1, MLA_ROPE, PAGE_SIZE), page_map(i)) for i in range(pg)]),
        out_specs=fixed(KV_LORA, nq),
        scratch_shapes=[pltpu.VMEM((nq, 1), F32), pltpu.VMEM((nq, 1), F32),
                        pltpu.VMEM((nq, KV_LORA), F32),
                        pltpu.VMEM((PAGE_SIZE, KV_LORA), BF16), pltpu.VMEM((PAGE_SIZE, HEAD_PAD), BF16)],
    )
    return pl.pallas_call(
        _dec_kernel,
        grid_spec=grid_spec,
        out_shape=jax.ShapeDtypeStruct((b, nq, KV_LORA), F32),
        compiler_params=pltpu.CompilerParams(dimension_semantics=("arbitrary", "arbitrary"),
                                             vmem_limit_bytes=VMEM_LIMIT),
        name="attn_sample",
    )(pt, qlat, qh, ckvn, krn, *([pool_ckv] * pg), *([pool_kr] * pg))


def _back_kernel(absorb, x_ref, gate_ref, hml_ref, mo_ref, mz_ref, o_ref, az_ref, gn_ref, wuv_ref,
                 wout_ref, lng_ref, lnb_ref, y_ref):
    tb, ts, d = x_ref.shape
    tm = tb * ts
    hm = hml_ref[...] * jax.nn.sigmoid(mo_ref[...])
    parts = []
    for hh in range(ML_H):
        t = hm[:, hh * ML_DV:(hh + 1) * ML_DV]
        mu = jnp.mean(t, axis=-1, keepdims=True)
        tc = t - mu
        var = jnp.mean(tc * tc, axis=-1, keepdims=True)
        parts.append(tc * lax.rsqrt(var + EPS))
    mz = mz_ref[...]
    y_ml = jnp.concatenate(parts, axis=1) * gn_ref[...] * (mz * jax.nn.sigmoid(mz))
    if absorb:
        olat = o_ref[...].astype(BF16)
        o_mla = jnp.concatenate(
            [_mm(olat[:, p * 2 * KV_LORA:(p + 1) * 2 * KV_LORA], wuv_ref[p]) for p in range(MLA_H // 2)],
            axis=1)
    else:
        o_mla = o_ref[...]
    az = az_ref[...]
    y_mla = o_mla * (az * jax.nn.sigmoid(az))
    out = _mm(y_ml.astype(BF16), wout_ref[0:ML_W, :]) + _mm(y_mla.astype(BF16), wout_ref[ML_W:, :])
    z = ALPHA * x_ref[...] + gate_ref[...] * out.reshape(tb, ts, d)
    mu = jnp.mean(z, axis=-1, keepdims=True)
    zc = z - mu
    var = jnp.mean(zc * zc, axis=-1, keepdims=True)
    y_ref[...] = zc * lax.rsqrt(var + EPS) * lng_ref[...] + lnb_ref[...]


def _back(absorb, x, mod, hml, mo, mz, o, az, wts):
    b, s, d = x.shape
    n = b * s
    tm = FRONT_TM
    if absorb:
        tb, ts = tm // s, s
        x_map = lambda i: (i, 0, 0)
        gate_map = lambda i: (i, 0, 2)
    else:
        tb, ts = 1, tm
        spb = s // tm
        x_map = lambda i: (i // spb, i % spb, 0)
        gate_map = lambda i: (i // spb, 0, 2)
    const2 = lambda i: (0, 0)
    tok = lambda w: pl.BlockSpec((tm, w), lambda i: (i, 0))
    ow = o.shape[1]
    return pl.pallas_call(
        functools.partial(_back_kernel, absorb),
        grid=(n // tm,),
        in_specs=[pl.BlockSpec((tb, ts, d), x_map),
                  pl.BlockSpec((tb, 1, d), gate_map),
                  tok(ML_W), tok(ML_W), tok(ML_W), tok(ow), tok(MLA_W),
                  pl.BlockSpec((1, ML_W), const2),
                  pl.BlockSpec((MLA_H // 2, 2 * KV_LORA, LANES), lambda i: (0, 0, 0)),
                  pl.BlockSpec((d, d), const2),
                  pl.BlockSpec((1, d), const2),
                  pl.BlockSpec((1, d), const2)],
        out_specs=pl.BlockSpec((tb, ts, d), x_map),
        out_shape=jax.ShapeDtypeStruct((b, s, d), F32),
        compiler_params=pltpu.CompilerParams(dimension_semantics=("arbitrary",),
                                             vmem_limit_bytes=VMEM_LIMIT),
        name="back_sample" if absorb else "back_prompt",
    )(x, mod, hml, mo, mz, o, az, wts["gn"], wts["wuv_pair"], wts["w_out"], wts["ln_g"], wts["ln_b"])


def _prep_weights(l, w_in, ml_b_i, ml_b_f, ml_gn, mla_q_norm, mla_kv_norm, mla_w_uq, mla_w_uk,
                  mla_w_uv, w_out, ln_g, ln_b):
    offs = np.cumsum((0, 512, 512, 512, ML_H, ML_H, 512, 512, Q_LORA, KV_LORA, MLA_ROPE, 512))
    names = ("mq", "mk", "mv", "mi", "mf", "mo", "mz", "cq", "ckv", "kr", "az")
    w = {nm: w_in[l][:, offs[i]:offs[i + 1]] for i, nm in enumerate(names)}
    d = w_in.shape[1]
    z = lambda c: jnp.zeros((d, c), F32)
    kr1, kr2 = w["kr"][:, :ROPE_HALF], w["kr"][:, ROPE_HALF:]
    tail = HEAD_PAD - MLA_NOPE - MLA_ROPE
    w_in_p = jnp.concatenate(
        [w["mq"], w["mk"], w["mv"], w["mi"], w["mf"], z(LANES - 2 * ML_H), w["mo"], w["mz"], w["cq"], w["ckv"],
         z(MLA_NOPE), kr1, kr2, z(tail), z(MLA_NOPE), -kr2, kr1, z(tail), w["az"]], axis=1).astype(BF16)
    b_gate = jnp.concatenate([ml_b_i[l], ml_b_f[l], jnp.zeros((LANES - 2 * ML_H,), F32)]).reshape(1, LANES)
    uq = mla_w_uq[l].reshape(Q_LORA, MLA_H, MLA_NOPE + MLA_ROPE)
    nope, r1, r2 = uq[..., :MLA_NOPE], uq[..., MLA_NOPE:MLA_NOPE + ROPE_HALF], uq[..., MLA_NOPE + ROPE_HALF:]
    zq = lambda c: jnp.zeros((Q_LORA, MLA_H, c), F32)
    wqa = jnp.concatenate([nope, r1, r2, zq(tail)], axis=-1).reshape(Q_LORA, MLA_H * HEAD_PAD).astype(BF16)
    wqr = jnp.concatenate([zq(MLA_NOPE), -r2, r1, zq(tail)], axis=-1).reshape(Q_LORA, MLA_H * HEAD_PAD).astype(BF16)
    uk = mla_w_uk[l].reshape(KV_LORA, MLA_H, MLA_NOPE)
    wk_pad = jnp.concatenate([uk, jnp.zeros((KV_LORA, MLA_H, HEAD_PAD - MLA_NOPE), F32)],
                             axis=-1).reshape(KV_LORA, MLA_H * HEAD_PAD).astype(BF16)
    wukt = jnp.concatenate([jnp.transpose(uk, (1, 2, 0)),
                            jnp.zeros((MLA_H, HEAD_PAD - MLA_NOPE, KV_LORA), F32)], axis=1).astype(BF16)
    uv = mla_w_uv[l].reshape(KV_LORA, MLA_H // 2, 2, MLA_VD)
    zv = jnp.zeros((KV_LORA, MLA_H // 2, MLA_VD), F32)
    wuv_pair = jnp.concatenate(
        [jnp.concatenate([uv[:, :, 0], zv], axis=-1), jnp.concatenate([zv, uv[:, :, 1]], axis=-1)],
        axis=0)
    wuv_pair = jnp.transpose(wuv_pair, (1, 0, 2)).astype(BF16)
    return dict(w_in=w_in_p, b_gate=b_gate, q_norm=mla_q_norm[l].reshape(1, -1),
                kv_norm=mla_kv_norm[l].reshape(1, -1), wqa=wqa, wqr=wqr, wk_pad=wk_pad, wukt=wukt,
                w_uv_t=mla_w_uv[l].T.astype(BF16), wuv_pair=wuv_pair, gn=ml_gn[l].reshape(1, -1),
                w_out=w_out[l].astype(BF16), ln_g=ln_g[l].reshape(1, -1), ln_b=ln_b[l].reshape(1, -1))


def _rope_tables(pos):
    inv = ROPE_THETA ** (-jnp.arange(ROPE_HALF, dtype=F32) / ROPE_HALF)
    ang = pos.astype(F32)[:, None] * inv[None, :]
    n = pos.shape[0]
    tail = HEAD_PAD - MLA_NOPE - MLA_ROPE
    cm = jnp.concatenate([jnp.ones((n, MLA_NOPE), F32), jnp.cos(ang), jnp.cos(ang), jnp.zeros((n, tail), F32)], 1)
    sm = jnp.concatenate([jnp.zeros((n, MLA_NOPE), F32), jnp.sin(ang), jnp.sin(ang), jnp.zeros((n, tail), F32)], 1)
    return cm, sm


def kernel(x_prompt, x_sample, c_prompt, c_sample, cache_ckv, cache_krope, state_C, state_n, state_m,
           page_table, w_ada, b_ada, w_in, ml_b_i, ml_b_f, ml_gn, mla_q_norm, mla_kv_norm,
           mla_w_uq, mla_w_uk, mla_w_uv, w_out, ln_g, ln_b):
    bp, sp, d = x_prompt.shape
    bs, ss, _ = x_sample.shape
    past = page_table.shape[1] * PAGE_SIZE
    cm_p, sm_p = _rope_tables(jnp.arange(sp))
    cm_s, sm_s = _rope_tables(past + jnp.arange(ss))
    reps = FRONT_TM // ss
    cm_s, sm_s = jnp.tile(cm_s, (reps, 1)), jnp.tile(sm_s, (reps, 1))
    pad = (-(bp + bs)) % 8
    c_all = jnp.concatenate([c_prompt, c_sample, jnp.zeros((pad, d), F32)], axis=0)

    xp, xs = x_prompt, x_sample
    outs = [[] for _ in range(10)]
    for l in range(DEPTH):
        wts = _prep_weights(l, w_in, ml_b_i, ml_b_f, ml_gn, mla_q_norm, mla_kv_norm, mla_w_uq, mla_w_uk,
                            mla_w_uv, w_out, ln_g, ln_b)
        mod = _ada(c_all, w_ada[l], b_ada[l])
        mod_p = mod[:bp].reshape(bp, 1, 3 * d)
        mod_s = mod[bp:bp + bs].reshape(bs, 1, 3 * d)

        mq, mk, mv, g, mo, mz, az, ckv, kr, qh, kh, vt = _front(False, xp, mod_p, cm_p, sm_p, wts)
        seq3 = lambda t: t.reshape(bp, sp, -1)
        c0 = jnp.zeros((bp, ML_H, ML_DK, ML_DV), F32)
        n0 = jnp.zeros((bp, ML_H, 1, ML_DK), F32)
        m0 = jnp.full((bp, ML_H, 1, LANES), -jnp.inf, F32)
        hml, c_p, n_p, m_p = _mlstm(seq3(mq), seq3(mk), seq3(mv), seq3(g), c0, n0, m0, ML_CHUNK, 1)
        o_p = _attn_prompt(qh, kh, vt)
        yp = _back(False, xp, mod_p, hml.reshape(bp * sp, -1), mo, mz, o_p.reshape(bp * sp, -1), az, wts)
        outs[0].append(ckv.reshape(bp, sp, KV_LORA))
        outs[1].append(kr.reshape(bp, sp, MLA_ROPE))
        outs[2].append(c_p)
        outs[3].append(n_p[:, :, 0, :])
        outs[4].append(m_p[:, :, 0, 0])

        mq, mk, mv, g, mo, mz, az, ckv, kr, qlat, qh, krp = _front(True, xs, mod_s, cm_s, sm_s, wts)
        seq3 = lambda t: t.reshape(bs, ss, -1)
        n0 = state_n[l].astype(F32).reshape(bs, ML_H, 1, ML_DK)
        m0 = jnp.broadcast_to(state_m[l].astype(F32)[:, :, None, None], (bs, ML_H, 1, LANES))
        hml, c_s, n_s, m_s = _mlstm(seq3(mq), seq3(mk), seq3(mv), seq3(g), state_C[l].astype(F32), n0, m0, ss, 8)
        o_s = _attn_sample(qlat.reshape(bs, ss * MLA_H, KV_LORA), qh.reshape(bs, ss * MLA_H, HEAD_PAD),
                           seq3(ckv), seq3(krp), cache_ckv[l], jnp.swapaxes(cache_krope[l], 1, 2), page_table)
        ys = _back(True, xs, mod_s, hml.reshape(bs * ss, -1), mo, mz,
                   o_s.reshape(bs * ss, MLA_H * KV_LORA), az, wts)
        outs[5].append(ckv.reshape(bs, ss, KV_LORA).astype(cache_ckv.dtype))
        outs[6].append(kr.reshape(bs, ss, MLA_ROPE).astype(cache_krope.dtype))
        outs[7].append(c_s.astype(state_C.dtype))
        outs[8].append(n_s[:, :, 0, :].astype(state_n.dtype))
        outs[9].append(m_s[:, :, 0, 0].astype(state_m.dtype))
        xp, xs = yp, ys
    return (xp, xs) + tuple(jnp.stack(o) for o in outs)
```

```python
import functools

import numpy as np
import jax
import jax.numpy as jnp
from jax import lax
from jax.experimental import pallas as pl
from jax.experimental.pallas import tpu as pltpu

F32 = jnp.float32
BF16 = jnp.bfloat16

D_MODEL = 1024
DEPTH = 1
PAGE_SIZE = 128
ML_W = 512
MLA_W = 512
ML_DK = 128
ML_DV = 128
ML_H = 4
MLA_VD = 64
MLA_H = 8
MLA_NOPE = 64
MLA_ROPE = 32
ROPE_HALF = MLA_ROPE // 2
MLA_SCALE = (MLA_NOPE + MLA_ROPE) ** -0.5
Q_SCALE = MLA_SCALE * float(np.log2(np.e))
Q_LORA = 384
KV_LORA = 256
ROPE_THETA = 10000.0
ML_CHUNK = 128
EPS = 1e-6
ALPHA = (2.0 * DEPTH) ** 0.25

LANES = 128
HEAD_PAD = 128
VMEM_LIMIT = 48 * 1024 * 1024

_SEG = {}
_off = 0
for _name, _w in (("mq", 512), ("mk", 512), ("mv", 512), ("gate", 128), ("mo", 512), ("mz", 512),
                  ("cq", Q_LORA), ("ckv", KV_LORA), ("kra", 128), ("krr", 128), ("az", 512)):
    _SEG[_name] = (_off, _off + _w)
    _off += _w
N_IN_PAD = _off

FRONT_TM = 256
ATT_TQ = 512
ATT_TK = 1024
ATT_QC = 512
DEC_PG = 16


def _nt(a, b):
    return lax.dot_general(a, b, (((1,), (1,)), ((), ())), preferred_element_type=F32)


def _tn(a, b):
    return lax.dot_general(a, b, (((0,), (0,)), ((), ())), preferred_element_type=F32)


def _mm(a, b):
    return jnp.dot(a, b, preferred_element_type=F32)


def _mm_exact(a, b):
    return jnp.dot(a, b, preferred_element_type=F32, precision=lax.Precision.HIGHEST)


def _nt_exact(a, b):
    return lax.dot_general(a, b, (((1,), (1,)), ((), ())), preferred_element_type=F32,
                           precision=lax.Precision.HIGHEST)


def _ada_kernel(c_ref, w_ref, b_ref, o_ref):
    o_ref[...] = _mm(c_ref[...].astype(BF16), w_ref[...].astype(BF16)) + b_ref[...]


def _ada(c_all, w_ada, b_ada):
    m = c_all.shape[0]
    tn = 512
    return pl.pallas_call(
        _ada_kernel,
        grid=(3 * D_MODEL // tn,),
        in_specs=[pl.BlockSpec((m, D_MODEL), lambda j: (0, 0)),
                  pl.BlockSpec((D_MODEL, tn), lambda j: (0, j)),
                  pl.BlockSpec((1, tn), lambda j: (0, j))],
        out_specs=pl.BlockSpec((m, tn), lambda j: (0, j)),
        out_shape=jax.ShapeDtypeStruct((m, 3 * D_MODEL), F32),
        name="ada",
    )(c_all, w_ada, b_ada.reshape(1, -1))


def _rms(x, g):
    return x * lax.rsqrt(jnp.mean(x * x, axis=-1, keepdims=True) + EPS) * g


def _log_sigmoid(x):
    return jnp.minimum(x, 0.0) - jnp.log1p(jnp.exp(-jnp.abs(x)))


def _front_kernel(absorb, x_ref, sh_ref, sc_ref, cm_ref, sm_ref, win_ref, bg_ref, qn_ref, kvn_ref,
                  wqa_ref, wqr_ref, wk_ref, wv_ref,
                  mq_ref, mk_ref, mv_ref, g_ref, mo_ref, mz_ref, az_ref, ckv_ref, kr_ref,
                  o1_ref, o2_ref, o3_ref):
    tb, ts, d = x_ref.shape
    tm = tb * ts
    h = x_ref[...] * (1.0 + sc_ref[...]) + sh_ref[...]
    h = h.reshape(tm, d).astype(BF16)

    def seg(name):
        lo, hi = _SEG[name]
        return _mm(h, win_ref[:, lo:hi])

    mq_ref[...] = seg("mq").astype(BF16)
    mk_ref[...] = (seg("mk") * (ML_DK ** -0.5)).astype(BF16)
    mv_ref[...] = seg("mv").astype(BF16)
    gz = seg("gate") + bg_ref[...]
    lane = lax.broadcasted_iota(jnp.int32, gz.shape, 1)
    g_ref[...] = jnp.where(lane < ML_H, gz, jnp.where(lane < 2 * ML_H, _log_sigmoid(gz), 0.0))
    mo_ref[...] = seg("mo").astype(BF16)
    mz_ref[...] = seg("mz").astype(BF16)
    az_ref[...] = seg("az").astype(BF16)

    cm = cm_ref[...]
    sm = sm_ref[...]
    ckvn = _rms(seg("ckv"), kvn_ref[...])
    ckv_ref[...] = ckvn
    ckvn_b = ckvn.astype(BF16)
    krp = seg("kra") * cm + seg("krr") * sm
    kr_ref[...] = krp[:, MLA_NOPE:MLA_NOPE + MLA_ROPE]

    cqn = _rms(seg("cq"), qn_ref[...]).astype(BF16)
    qa = _mm(cqn, wqa_ref[...])
    qr = _mm(cqn, wqr_ref[...])
    for hh in range(MLA_H):
        sl = slice(hh * HEAD_PAD, (hh + 1) * HEAD_PAD)
        qh = ((qa[:, sl] * cm + qr[:, sl] * sm) * Q_SCALE).astype(BF16)
        if absorb:
            o1_ref[:, hh * KV_LORA:(hh + 1) * KV_LORA] = _mm(qh, wk_ref[hh]).astype(BF16)
            o2_ref[:, sl] = qh
        else:
            o1_ref[0, hh] = qh
            o2_ref[0, hh] = (_mm(ckvn_b, wk_ref[:, sl]) + krp).astype(BF16)
    if absorb:
        o3_ref[...] = krp.astype(BF16)
    else:
        o3_ref[0] = _nt(wv_ref[...], ckvn_b).astype(BF16)


def _front(absorb, x, mod, cm, sm, wts):
    b, s, d = x.shape
    n = b * s
    tm = FRONT_TM
    if absorb:
        tb, ts = tm // s, s
        x_map = lambda i: (i, 0, 0)
        mod_map = lambda k: (lambda i: (i, 0, k))
        tab_map = lambda i: (0, 0)
    else:
        tb, ts = 1, tm
        spb = s // tm
        x_map = lambda i: (i // spb, i % spb, 0)
        mod_map = lambda k: (lambda i: (i // spb, 0, k))
        tab_map = lambda i: (i % spb, 0)
    const2 = lambda i: (0, 0)
    tok = lambda w: pl.BlockSpec((tm, w), lambda i: (i, 0))
    in_specs = [
        pl.BlockSpec((tb, ts, d), x_map),
        pl.BlockSpec((tb, 1, d), mod_map(0)),
        pl.BlockSpec((tb, 1, d), mod_map(1)),
        pl.BlockSpec((tm, LANES), tab_map),
        pl.BlockSpec((tm, LANES), tab_map),
        pl.BlockSpec((d, N_IN_PAD), const2),
        pl.BlockSpec((1, LANES), const2),
        pl.BlockSpec((1, Q_LORA), const2),
        pl.BlockSpec((1, KV_LORA), const2),
        pl.BlockSpec((Q_LORA, MLA_H * HEAD_PAD), const2),
        pl.BlockSpec((Q_LORA, MLA_H * HEAD_PAD), const2),
    ]
    out_specs = [tok(512), tok(512), tok(512), tok(LANES), tok(512), tok(512), tok(512),
                 tok(KV_LORA), tok(MLA_ROPE)]
    out_shape = [jax.ShapeDtypeStruct((n, w), dt)
                 for w, dt in ((512, BF16), (512, BF16), (512, BF16), (LANES, F32), (512, BF16), (512, BF16),
                               (512, BF16), (KV_LORA, F32), (MLA_ROPE, F32))]
    if absorb:
        in_specs += [pl.BlockSpec((MLA_H, HEAD_PAD, KV_LORA), lambda i: (0, 0, 0)),
                     pl.BlockSpec((MLA_W, KV_LORA), const2)]
        out_specs += [tok(MLA_H * KV_LORA), tok(MLA_H * HEAD_PAD), tok(HEAD_PAD)]
        out_shape += [jax.ShapeDtypeStruct((n, MLA_H * KV_LORA), BF16),
                      jax.ShapeDtypeStruct((n, MLA_H * HEAD_PAD), BF16),
                      jax.ShapeDtypeStruct((n, HEAD_PAD), BF16)]
        wk = wts["wukt"]
    else:
        head_map = lambda i: (i // spb, 0, i % spb, 0)
        in_specs += [pl.BlockSpec((KV_LORA, MLA_H * HEAD_PAD), const2),
                     pl.BlockSpec((MLA_W, KV_LORA), const2)]
        out_specs += [pl.BlockSpec((1, MLA_H, tm, HEAD_PAD), head_map),
                      pl.BlockSpec((1, MLA_H, tm, HEAD_PAD), head_map),
                      pl.BlockSpec((1, MLA_W, tm), lambda i: (i // spb, 0, i % spb))]
        out_shape += [jax.ShapeDtypeStruct((b, MLA_H, s, HEAD_PAD), BF16),
                      jax.ShapeDtypeStruct((b, MLA_H, s, HEAD_PAD), BF16),
                      jax.ShapeDtypeStruct((b, MLA_W, s), BF16)]
        wk = wts["wk_pad"]
    return pl.pallas_call(
        functools.partial(_front_kernel, absorb),
        grid=(n // tm,),
        in_specs=in_specs,
        out_specs=out_specs,
        out_shape=out_shape,
        compiler_params=pltpu.CompilerParams(dimension_semantics=("arbitrary",),
                                             vmem_limit_bytes=VMEM_LIMIT),
        name="front_sample" if absorb else "front_prompt",
    )(x, mod, mod, cm, sm, wts["w_in"], wts["b_gate"], wts["q_norm"], wts["kv_norm"],
      wts["wqa"], wts["wqr"], wk, wts["w_uv_t"])


def _mlstm_kernel(q_ref, k_ref, v_ref, g_ref, c0_ref, n0_ref, m0_ref,
                  h_ref, c_ref, n_ref, m_ref):
    tb, L, _ = q_ref.shape
    ci = pl.program_id(1)

    @pl.when(ci == 0)
    def _():
        c_ref[...] = c0_ref[...]
        n_ref[...] = n0_ref[...]
        m_ref[...] = m0_ref[...]

    row = lax.broadcasted_iota(jnp.int32, (L, L), 0)
    col = lax.broadcasted_iota(jnp.int32, (L, L), 1)
    causal = col <= row
    tril = causal.astype(F32)
    sel = (lax.broadcasted_iota(jnp.int32, (8, LANES), 0)
           == lax.broadcasted_iota(jnp.int32, (8, LANES), 1)).astype(F32)

    for t in range(tb):
        g = g_ref[t]
        fcum = _mm_exact(tril, g)
        g_rows = _nt_exact(sel, g)
        f_rows = _nt_exact(sel, fcum)
        for hh in range(ML_H):
            sl = slice(hh * ML_DK, (hh + 1) * ML_DK)
            qb = q_ref[t, :, sl]
            kb = k_ref[t, :, sl]
            vb = v_ref[t, :, sl]
            q = qb.astype(F32)
            k = kb.astype(F32)
            c_prev = c_ref[t, hh]
            n_prev = n_ref[t, hh]
            m_prev = m_ref[t, hh][:, :1]
            f_col = fcum[:, ML_H + hh:ML_H + hh + 1]
            ig_col = g[:, hh:hh + 1]
            f_row = f_rows[ML_H + hh:ML_H + hh + 1, :]
            ig_row = g_rows[hh:hh + 1, :]
            dmat = jnp.where(causal, f_col - f_row + ig_row, -jnp.inf)
            m_inter = f_col + m_prev
            m_t = jnp.maximum(m_inter, jnp.max(dmat, axis=-1, keepdims=True))
            w = jnp.exp(dmat - m_t)
            a = jnp.exp(m_inter - m_t)
            sq = _nt(qb, kb) * w
            num = _mm(sq.astype(BF16), vb) + a * _mm(qb, c_prev.astype(BF16))
            den = jnp.sum(sq, axis=-1, keepdims=True) + a * jnp.sum(q * n_prev, axis=-1, keepdims=True)
            h_ref[t, :, sl] = (num / jnp.maximum(jnp.abs(den), jnp.exp(-m_t))).astype(h_ref.dtype)
            m_last = m_t[L - 1:L, :]
            wl_col = jnp.exp(f_col[L - 1:L, :] - f_col + ig_col - m_last)
            al = a[L - 1:L, :]
            kw = k * wl_col
            c_ref[t, hh] = al * c_prev + _tn(kw.astype(BF16), vb)
            n_ref[t, hh] = al * n_prev + jnp.sum(kw, axis=0, keepdims=True)
            m_ref[t, hh] = jnp.broadcast_to(m_last, (1, LANES))


def _mlstm(q, k, v, g, c0, n0, m0, chunk, tb):
    b, s, w = q.shape
    nc = s // chunk
    seq = lambda ww: pl.BlockSpec((tb, chunk, ww), lambda i, c: (i, c, 0))
    st_c = pl.BlockSpec((tb, ML_H, ML_DK, ML_DV), lambda i, c: (i, 0, 0, 0))
    st_v = pl.BlockSpec((tb, ML_H, 1, LANES), lambda i, c: (i, 0, 0, 0))
    return pl.pallas_call(
        _mlstm_kernel,
        grid=(b // tb, nc),
        in_specs=[seq(w), seq(w), seq(w), seq(LANES), st_c, st_v, st_v],
        out_specs=[seq(w), st_c, st_v, st_v],
        out_shape=[jax.ShapeDtypeStruct((b, s, w), BF16),
                   jax.ShapeDtypeStruct((b, ML_H, ML_DK, ML_DV), F32),
                   jax.ShapeDtypeStruct((b, ML_H, 1, LANES), F32),
                   jax.ShapeDtypeStruct((b, ML_H, 1, LANES), F32)],
        compiler_params=pltpu.CompilerParams(dimension_semantics=("arbitrary", "arbitrary"),
                                             vmem_limit_bytes=VMEM_LIMIT),
        name="mlstm",
    )(q, k, v, g, c0, n0, m0)


def _attn_kernel(qi_ref, ki_ref, last_ref, q_ref, k_ref, vt_ref, o_ref, m_sc, l_sc, acc_sc):
    step = pl.program_id(2)
    qi = qi_ref[step]
    ki = ki_ref[step]
    tq = q_ref.shape[2]
    tk = k_ref.shape[2]

    @pl.when(ki == 0)
    def _():
        m_sc[...] = jnp.full(m_sc.shape, -jnp.inf, F32)
        l_sc[...] = jnp.zeros(l_sc.shape, F32)
        acc_sc[...] = jnp.zeros(acc_sc.shape, F32)

    def tile(masked):
        m_all, l_all, acc_all = m_sc[...], l_sc[...], acc_sc[...]
        m_out, l_out, acc_out = [], [], []
        for hh in range(2):
            rows = slice(hh * MLA_VD, (hh + 1) * MLA_VD)
            k = k_ref[0, hh]
            vt = vt_ref[0, rows, :]
            m_h, l_h, acc_h = [], [], []
            for c0 in range(0, tq, ATT_QC):
                qs = slice(c0, c0 + ATT_QC)
                st = _nt(k, q_ref[0, hh, qs, :])
                if masked:
                    keys = ki * tk + lax.broadcasted_iota(jnp.int32, (tk, ATT_QC), 0)
                    qrys = qi * tq + c0 + lax.broadcasted_iota(jnp.int32, (tk, ATT_QC), 1)
                    st = jnp.where(keys <= qrys, st, -jnp.inf)
                m_prev = m_all[hh, :, qs]
                m_new = jnp.maximum(m_prev, jnp.max(st, axis=0, keepdims=True))
                alpha = jnp.exp2(m_prev - m_new)
                pt = jnp.exp2(st - m_new)
                l_h.append(alpha * l_all[hh, :, qs] + jnp.sum(pt, axis=0, keepdims=True))
                m_h.append(m_new)
                acc_h.append(acc_all[rows, qs] * alpha + _mm(vt, pt.astype(BF16)))
            m_out.append(jnp.concatenate(m_h, axis=1))
            l_out.append(jnp.concatenate(l_h, axis=1))
            acc_out.append(jnp.concatenate(acc_h, axis=1))
        m_sc[...] = jnp.stack(m_out)
        l_sc[...] = jnp.stack(l_out)
        acc_sc[...] = jnp.concatenate(acc_out, axis=0)

    crosses_diagonal = (ki + 1) * tk - 1 > qi * tq
    pl.when(crosses_diagonal)(functools.partial(tile, True))
    pl.when(jnp.logical_not(crosses_diagonal))(functools.partial(tile, False))

    @pl.when(last_ref[step] == 1)
    def _():
        out_t = jnp.concatenate([acc_sc[0:MLA_VD, :] / l_sc[0], acc_sc[MLA_VD:, :] / l_sc[1]], axis=0)
        o_ref[0] = out_t.T.astype(o_ref.dtype)


def _attn_prompt(qh, kh, vt):
    b, nh, s, _ = qh.shape
    tq, tk = ATT_TQ, ATT_TK
    qi_l, ki_l, last_l = [], [], []
    for qi in range(s // tq):
        nk = ((qi + 1) * tq + tk - 1) // tk
        for ki in range(nk):
            qi_l.append(qi)
            ki_l.append(ki)
            last_l.append(int(ki == nk - 1))
    nsteps = len(qi_l)
    sched = [jnp.asarray(np.asarray(a, np.int32)) for a in (qi_l, ki_l, last_l)]
    grid_spec = pltpu.PrefetchScalarGridSpec(
        num_scalar_prefetch=3,
        grid=(b, nh // 2, nsteps),
        in_specs=[pl.BlockSpec((1, 2, tq, HEAD_PAD), lambda bi, hp, st, qi, ki, la: (bi, hp, qi[st], 0)),
                  pl.BlockSpec((1, 2, tk, HEAD_PAD), lambda bi, hp, st, qi, ki, la: (bi, hp, ki[st], 0)),
                  pl.BlockSpec((1, 2 * MLA_VD, tk), lambda bi, hp, st, qi, ki, la: (bi, hp, ki[st]))],
        out_specs=pl.BlockSpec((1, tq, LANES), lambda bi, hp, st, qi, ki, la: (bi, qi[st], hp)),
        scratch_shapes=[pltpu.VMEM((2, 1, tq), F32), pltpu.VMEM((2, 1, tq), F32),
                        pltpu.VMEM((2 * MLA_VD, tq), F32)],
    )
    return pl.pallas_call(
        _attn_kernel,
        grid_spec=grid_spec,
        out_shape=jax.ShapeDtypeStruct((b, s, nh * MLA_VD), BF16),
        compiler_params=pltpu.CompilerParams(
            dimension_semantics=("arbitrary", "arbitrary", "arbitrary"),
            vmem_limit_bytes=VMEM_LIMIT),
        name="attn_prompt",
    )(*sched, qh, kh, vt)


def _dec_kernel(pt_ref, qlat_ref, qh_ref, ckvn_ref, krn_ref, *rest):
    pg = DEC_PG
    ckv_refs = rest[:pg]
    kr_refs = rest[pg:2 * pg]
    o_ref, m_sc, l_sc, acc_sc, padc_sc, padk_sc = rest[2 * pg:]
    j = pl.program_id(1)
    qlat = qlat_ref[0]
    qh = qh_ref[0]
    nq = qlat.shape[0]
    ds = ckvn_ref.shape[1]

    @pl.when(j == 0)
    def _():
        padc_sc[...] = jnp.zeros(padc_sc.shape, BF16)
        padk_sc[...] = jnp.zeros(padk_sc.shape, BF16)
        padc_sc[0:ds, :] = ckvn_ref[0].astype(BF16)
        padk_sc[0:ds, :] = krn_ref[0]
        cn = padc_sc[...]
        s = _nt(qlat, cn) + _nt(qh, padk_sc[...])
        tok = lax.broadcasted_iota(jnp.int32, s.shape, 0) // MLA_H
        key = lax.broadcasted_iota(jnp.int32, s.shape, 1)
        s = jnp.where(key <= tok, s, -jnp.inf)
        m = jnp.max(s, axis=-1, keepdims=True)
        p = jnp.exp2(s - m)
        m_sc[...] = m
        l_sc[...] = jnp.sum(p, axis=-1, keepdims=True)
        acc_sc[...] = _mm(p.astype(BF16), cn)

    qr = qh[:, MLA_NOPE:MLA_NOPE + MLA_ROPE]
    pages = jnp.concatenate([r[0].astype(BF16) for r in ckv_refs], axis=0)
    krt = jnp.concatenate([r[0].astype(BF16) for r in kr_refs], axis=1)
    s = _nt(qlat, pages) + _mm(qr, krt)
    m_prev = m_sc[...]
    m_new = jnp.maximum(m_prev, jnp.max(s, axis=-1, keepdims=True))
    alpha = jnp.exp2(m_prev - m_new)
    p = jnp.exp2(s - m_new)
    l_sc[...] = alpha * l_sc[...] + jnp.sum(p, axis=-1, keepdims=True)
    m_sc[...] = m_new
    acc_sc[...] = acc_sc[...] * alpha + _mm(p.astype(BF16), pages)

    @pl.when(j == pl.num_programs(1) - 1)
    def _():
        o_ref[0] = acc_sc[...] / l_sc[...]


def _attn_sample(qlat, qh, ckvn, krn, pool_ckv, pool_kr, page_table):
    b, nq, _ = qlat.shape
    ds = ckvn.shape[1]
    n_pages = page_table.shape[1]
    pg = DEC_PG
    pt = page_table.reshape(-1)

    def page_map(i):
        return lambda bi, j, pt_ref: (pt_ref[bi * n_pages + j * pg + i], 0, 0)

    fixed = lambda w, r: pl.BlockSpec((1, r, w), lambda bi, j, pt_ref: (bi, 0, 0))
    grid_spec = pltpu.PrefetchScalarGridSpec(
        num_scalar_prefetch=1,
        grid=(b, n_pages // pg),
        in_specs=([fixed(KV_LORA, nq), fixed(HEAD_PAD, nq), fixed(KV_LORA, ds), fixed(HEAD_PAD, ds)]
                  + [pl.BlockSpec((1, PAGE_SIZE, KV_LORA), page_map(i)) for i in range(pg)]
                  + [pl.BlockSpec((1, MLA_ROPE, PAGE_SIZE), page_map(i)) for i in range(pg)]),
        out_specs=fixed(KV_LORA, nq),
        scratch_shapes=[pltpu.VMEM((nq, 1), F32), pltpu.VMEM((nq, 1), F32),
                        pltpu.VMEM((nq, KV_LORA), F32),
                        pltpu.VMEM((PAGE_SIZE, KV_LORA), BF16), pltpu.VMEM((PAGE_SIZE, HEAD_PAD), BF16)],
    )
    return pl.pallas_call(
        _dec_kernel,
        grid_spec=grid_spec,
        out_shape=jax.ShapeDtypeStruct((b, nq, KV_LORA), F32),
        compiler_params=pltpu.CompilerParams(dimension_semantics=("arbitrary", "arbitrary"),
                                             vmem_limit_bytes=VMEM_LIMIT),
        name="attn_sample",
    )(pt, qlat, qh, ckvn, krn, *([pool_ckv] * pg), *([pool_kr] * pg))


def _back_kernel(absorb, x_ref, gate_ref, hml_ref, mo_ref, mz_ref, o_ref, az_ref, gn_ref, wuv_ref,
                 wout_ref, lng_ref, lnb_ref, y_ref):
    tb, ts, d = x_ref.shape
    tm = tb * ts
    hm = hml_ref[...].astype(F32) * jax.nn.sigmoid(mo_ref[...].astype(F32))
    parts = []
    for hh in range(ML_H):
        t = hm[:, hh * ML_DV:(hh + 1) * ML_DV]
        mu = jnp.mean(t, axis=-1, keepdims=True)
        tc = t - mu
        var = jnp.mean(tc * tc, axis=-1, keepdims=True)
        parts.append(tc * lax.rsqrt(var + EPS))
    mz = mz_ref[...].astype(F32)
    y_ml = jnp.concatenate(parts, axis=1) * gn_ref[...] * (mz * jax.nn.sigmoid(mz))
    if absorb:
        olat = o_ref[...].astype(BF16)
        o_mla = jnp.concatenate(
            [_mm(olat[:, p * 2 * KV_LORA:(p + 1) * 2 * KV_LORA], wuv_ref[p]) for p in range(MLA_H // 2)],
            axis=1)
    else:
        o_mla = o_ref[...].astype(F32)
    az = az_ref[...].astype(F32)
    y_mla = o_mla * (az * jax.nn.sigmoid(az))
    out = _mm(y_ml.astype(BF16), wout_ref[0:ML_W, :]) + _mm(y_mla.astype(BF16), wout_ref[ML_W:, :])
    z = ALPHA * x_ref[...] + gate_ref[...] * out.reshape(tb, ts, d)
    mu = jnp.mean(z, axis=-1, keepdims=True)
    zc = z - mu
    var = jnp.mean(zc * zc, axis=-1, keepdims=True)
    y_ref[...] = zc * lax.rsqrt(var + EPS) * lng_ref[...] + lnb_ref[...]


def _back(absorb, x, mod, hml, mo, mz, o, az, wts):
    b, s, d = x.shape
    n = b * s
    tm = FRONT_TM
    if absorb:
        tb, ts = tm // s, s
        x_map = lambda i: (i, 0, 0)
        gate_map = lambda i: (i, 0, 2)
    else:
        tb, ts = 1, tm
        spb = s // tm
        x_map = lambda i: (i // spb, i % spb, 0)
        gate_map = lambda i: (i // spb, 0, 2)
    const2 = lambda i: (0, 0)
    tok = lambda w: pl.BlockSpec((tm, w), lambda i: (i, 0))
    ow = o.shape[1]
    return pl.pallas_call(
        functools.partial(_back_kernel, absorb),
        grid=(n // tm,),
        in_specs=[pl.BlockSpec((tb, ts, d), x_map),
                  pl.BlockSpec((tb, 1, d), gate_map),
                  tok(ML_W), tok(ML_W), tok(ML_W), tok(ow), tok(MLA_W),
                  pl.BlockSpec((1, ML_W), const2),
                  pl.BlockSpec((MLA_H // 2, 2 * KV_LORA, LANES), lambda i: (0, 0, 0)),
                  pl.BlockSpec((d, d), const2),
                  pl.BlockSpec((1, d), const2),
                  pl.BlockSpec((1, d), const2)],
        out_specs=pl.BlockSpec((tb, ts, d), x_map),
        out_shape=jax.ShapeDtypeStruct((b, s, d), F32),
        compiler_params=pltpu.CompilerParams(dimension_semantics=("arbitrary",),
                                             vmem_limit_bytes=VMEM_LIMIT),
        name="back_sample" if absorb else "back_prompt",
    )(x, mod, hml, mo, mz, o, az, wts["gn"], wts["wuv_pair"], wts["w_out"], wts["ln_g"], wts["ln_b"])


def _prep_weights(l, w_in, ml_b_i, ml_b_f, ml_gn, mla_q_norm, mla_kv_norm, mla_w_uq, mla_w_uk,
                  mla_w_uv, w_out, ln_g, ln_b):
    offs = np.cumsum((0, 512, 512, 512, ML_H, ML_H, 512, 512, Q_LORA, KV_LORA, MLA_ROPE, 512))
    names = ("mq", "mk", "mv", "mi", "mf", "mo", "mz", "cq", "ckv", "kr", "az")
    w = {nm: w_in[l][:, offs[i]:offs[i + 1]] for i, nm in enumerate(names)}
    d = w_in.shape[1]
    z = lambda c: jnp.zeros((d, c), F32)
    kr1, kr2 = w["kr"][:, :ROPE_HALF], w["kr"][:, ROPE_HALF:]
    tail = HEAD_PAD - MLA_NOPE - MLA_ROPE
    w_in_p = jnp.concatenate(
        [w["mq"], w["mk"], w["mv"], w["mi"], w["mf"], z(LANES - 2 * ML_H), w["mo"], w["mz"], w["cq"], w["ckv"],
         z(MLA_NOPE), kr1, kr2, z(tail), z(MLA_NOPE), -kr2, kr1, z(tail), w["az"]], axis=1).astype(BF16)
    b_gate = jnp.concatenate([ml_b_i[l], ml_b_f[l], jnp.zeros((LANES - 2 * ML_H,), F32)]).reshape(1, LANES)
    uq = mla_w_uq[l].reshape(Q_LORA, MLA_H, MLA_NOPE + MLA_ROPE)
    nope, r1, r2 = uq[..., :MLA_NOPE], uq[..., MLA_NOPE:MLA_NOPE + ROPE_HALF], uq[..., MLA_NOPE + ROPE_HALF:]
    zq = lambda c: jnp.zeros((Q_LORA, MLA_H, c), F32)
    wqa = jnp.concatenate([nope, r1, r2, zq(tail)], axis=-1).reshape(Q_LORA, MLA_H * HEAD_PAD).astype(BF16)
    wqr = jnp.concatenate([zq(MLA_NOPE), -r2, r1, zq(tail)], axis=-1).reshape(Q_LORA, MLA_H * HEAD_PAD).astype(BF16)
    uk = mla_w_uk[l].reshape(KV_LORA, MLA_H, MLA_NOPE)
    wk_pad = jnp.concatenate([uk, jnp.zeros((KV_LORA, MLA_H, HEAD_PAD - MLA_NOPE), F32)],
                             axis=-1).reshape(KV_LORA, MLA_H * HEAD_PAD).astype(BF16)
    wukt = jnp.concatenate([jnp.transpose(uk, (1, 2, 0)),
                            jnp.zeros((MLA_H, HEAD_PAD - MLA_NOPE, KV_LORA), F32)], axis=1).astype(BF16)
    uv = mla_w_uv[l].reshape(KV_LORA, MLA_H // 2, 2, MLA_VD)
    zv = jnp.zeros((KV_LORA, MLA_H // 2, MLA_VD), F32)
    wuv_pair = jnp.concatenate(
        [jnp.concatenate([uv[:, :, 0], zv], axis=-1), jnp.concatenate([zv, uv[:, :, 1]], axis=-1)],
        axis=0)
    wuv_pair = jnp.transpose(wuv_pair, (1, 0, 2)).astype(BF16)
    return dict(w_in=w_in_p, b_gate=b_gate, q_norm=mla_q_norm[l].reshape(1, -1),
                kv_norm=mla_kv_norm[l].reshape(1, -1), wqa=wqa, wqr=wqr, wk_pad=wk_pad, wukt=wukt,
                w_uv_t=mla_w_uv[l].T.astype(BF16), wuv_pair=wuv_pair, gn=ml_gn[l].reshape(1, -1),
                w_out=w_out[l].astype(BF16), ln_g=ln_g[l].reshape(1, -1), ln_b=ln_b[l].reshape(1, -1))


def _rope_tables(pos):
    f32 = np.float32
    inv = f32(ROPE_THETA) ** (-np.arange(ROPE_HALF, dtype=f32) / f32(ROPE_HALF))
    ang = pos.astype(f32)[:, None] * inv[None, :]
    n = pos.shape[0]
    tail = HEAD_PAD - MLA_NOPE - MLA_ROPE
    cos, sin = np.cos(ang).astype(f32), np.sin(ang).astype(f32)
    cm = np.concatenate([np.ones((n, MLA_NOPE), f32), cos, cos, np.zeros((n, tail), f32)], 1)
    sm = np.concatenate([np.zeros((n, MLA_NOPE), f32), sin, sin, np.zeros((n, tail), f32)], 1)
    return cm, sm


def kernel(x_prompt, x_sample, c_prompt, c_sample, cache_ckv, cache_krope, state_C, state_n, state_m,
           page_table, w_ada, b_ada, w_in, ml_b_i, ml_b_f, ml_gn, mla_q_norm, mla_kv_norm,
           mla_w_uq, mla_w_uk, mla_w_uv, w_out, ln_g, ln_b):
    bp, sp, d = x_prompt.shape
    bs, ss, _ = x_sample.shape
    past = page_table.shape[1] * PAGE_SIZE
    cm_p, sm_p = (jnp.asarray(t) for t in _rope_tables(np.arange(sp)))
    reps = FRONT_TM // ss
    cm_s, sm_s = (jnp.asarray(np.tile(t, (reps, 1))) for t in _rope_tables(past + np.arange(ss)))
    pad = (-(bp + bs)) % 8
    c_all = jnp.concatenate([c_prompt, c_sample, jnp.zeros((pad, d), F32)], axis=0)

    xp, xs = x_prompt, x_sample
    outs = [[] for _ in range(10)]
    for l in range(DEPTH):
        wts = _prep_weights(l, w_in, ml_b_i, ml_b_f, ml_gn, mla_q_norm, mla_kv_norm, mla_w_uq, mla_w_uk,
                            mla_w_uv, w_out, ln_g, ln_b)
        mod = _ada(c_all, w_ada[l], b_ada[l])
        mod_p = mod[:bp].reshape(bp, 1, 3 * d)
        mod_s = mod[bp:bp + bs].reshape(bs, 1, 3 * d)

        mq, mk, mv, g, mo, mz, az, ckv, kr, qh, kh, vt = _front(False, xp, mod_p, cm_p, sm_p, wts)
        seq3 = lambda t: t.reshape(bp, sp, -1)
        c0 = jnp.zeros((bp, ML_H, ML_DK, ML_DV), F32)
        n0 = jnp.zeros((bp, ML_H, 1, ML_DK), F32)
        m0 = jnp.full((bp, ML_H, 1, LANES), -jnp.inf, F32)
        hml, c_p, n_p, m_p = _mlstm(seq3(mq), seq3(mk), seq3(mv), seq3(g), c0, n0, m0, ML_CHUNK, bp)
        o_p = _attn_prompt(qh, kh, vt)
        yp = _back(False, xp, mod_p, hml.reshape(bp * sp, -1), mo, mz, o_p.reshape(bp * sp, -1), az, wts)
        outs[0].append(ckv.reshape(bp, sp, KV_LORA))
        outs[1].append(kr.reshape(bp, sp, MLA_ROPE))
        outs[2].append(c_p)
        outs[3].append(n_p[:, :, 0, :])
        outs[4].append(m_p[:, :, 0, 0])

        mq, mk, mv, g, mo, mz, az, ckv, kr, qlat, qh, krp = _front(True, xs, mod_s, cm_s, sm_s, wts)
        seq3 = lambda t: t.reshape(bs, ss, -1)
        n0 = state_n[l].astype(F32).reshape(bs, ML_H, 1, ML_DK)
        m0 = jnp.broadcast_to(state_m[l].astype(F32)[:, :, None, None], (bs, ML_H, 1, LANES))
        hml, c_s, n_s, m_s = _mlstm(seq3(mq), seq3(mk), seq3(mv), seq3(g), state_C[l].astype(F32), n0, m0, ss, 8)
        o_s = _attn_sample(qlat.reshape(bs, ss * MLA_H, KV_LORA), qh.reshape(bs, ss * MLA_H, HEAD_PAD),
                           seq3(ckv), seq3(krp), cache_ckv[l], jnp.swapaxes(cache_krope[l], 1, 2), page_table)
        ys = _back(True, xs, mod_s, hml.reshape(bs * ss, -1), mo, mz,
                   o_s.reshape(bs * ss, MLA_H * KV_LORA), az, wts)
        outs[5].append(ckv.reshape(bs, ss, KV_LORA).astype(cache_ckv.dtype))
        outs[6].append(kr.reshape(bs, ss, MLA_ROPE).astype(cache_krope.dtype))
        outs[7].append(c_s.astype(state_C.dtype))
        outs[8].append(n_s[:, :, 0, :].astype(state_n.dtype))
        outs[9].append(m_s[:, :, 0, 0].astype(state_m.dtype))
        xp, xs = yp, ys
    return (xp, xs) + tuple(jnp.stack(o) for o in outs)
```

```python
import functools

import numpy as np
import jax
import jax.numpy as jnp
from jax import lax
from jax.experimental import pallas as pl
from jax.experimental.pallas import tpu as pltpu

F32 = jnp.float32
BF16 = jnp.bfloat16

D_MODEL = 1024
DEPTH = 1
PAGE_SIZE = 128
ML_W = 512
MLA_W = 512
ML_DK = 128
ML_DV = 128
ML_H = 4
MLA_VD = 64
MLA_H = 8
MLA_NOPE = 64
MLA_ROPE = 32
ROPE_HALF = MLA_ROPE // 2
MLA_SCALE = (MLA_NOPE + MLA_ROPE) ** -0.5
Q_SCALE = MLA_SCALE * float(np.log2(np.e))
Q_LORA = 384
KV_LORA = 256
ROPE_THETA = 10000.0
ML_CHUNK = 128
EPS = 1e-6
ALPHA = (2.0 * DEPTH) ** 0.25

LANES = 128
HEAD_PAD = 128
VMEM_LIMIT = 48 * 1024 * 1024

_SEG = {}
_off = 0
for _name, _w in (("mq", 512), ("mk", 512), ("mv", 512), ("gate", 128), ("mo", 512), ("mz", 512),
                  ("cq", Q_LORA), ("ckv", KV_LORA), ("kra", 128), ("krr", 128), ("az", 512)):
    _SEG[_name] = (_off, _off + _w)
    _off += _w
N_IN_PAD = _off

FRONT_TM = 256
ATT_TQ = 512
ATT_TK = 1024
ATT_QC = 512
DEC_PG = 16
DEC_GROUPS = 2


def _nt(a, b):
    return lax.dot_general(a, b, (((1,), (1,)), ((), ())), preferred_element_type=F32)


def _tn(a, b):
    return lax.dot_general(a, b, (((0,), (0,)), ((), ())), preferred_element_type=F32)


def _mm(a, b):
    return jnp.dot(a, b, preferred_element_type=F32)


def _mm_exact(a, b):
    return jnp.dot(a, b, preferred_element_type=F32, precision=lax.Precision.HIGHEST)


def _nt_exact(a, b):
    return lax.dot_general(a, b, (((1,), (1,)), ((), ())), preferred_element_type=F32,
                           precision=lax.Precision.HIGHEST)


def _ada_kernel(c_ref, w_ref, b_ref, o_ref):
    o_ref[...] = _mm(c_ref[...].astype(BF16), w_ref[...].astype(BF16)) + b_ref[...]


def _ada(c_all, w_ada, b_ada):
    m = c_all.shape[0]
    tn = 512
    return pl.pallas_call(
        _ada_kernel,
        grid=(3 * D_MODEL // tn,),
        in_specs=[pl.BlockSpec((m, D_MODEL), lambda j: (0, 0)),
                  pl.BlockSpec((D_MODEL, tn), lambda j: (0, j)),
                  pl.BlockSpec((1, tn), lambda j: (0, j))],
        out_specs=pl.BlockSpec((m, tn), lambda j: (0, j)),
        out_shape=jax.ShapeDtypeStruct((m, 3 * D_MODEL), F32),
        name="ada",
    )(c_all, w_ada, b_ada.reshape(1, -1))


def _rms(x, g):
    return x * lax.rsqrt(jnp.mean(x * x, axis=-1, keepdims=True) + EPS) * g


def _log_sigmoid(x):
    return jnp.minimum(x, 0.0) - jnp.log1p(jnp.exp(-jnp.abs(x)))


def _front_kernel(absorb, x_ref, sh_ref, sc_ref, cm_ref, sm_ref, win_ref, bg_ref, qn_ref, kvn_ref,
                  wqa_ref, wqr_ref, wk_ref, wv_ref,
                  mq_ref, mk_ref, mv_ref, g_ref, mo_ref, mz_ref, az_ref, ckv_ref, kr_ref,
                  o1_ref, o2_ref, o3_ref):
    tb, ts, d = x_ref.shape
    tm = tb * ts
    h = x_ref[...] * (1.0 + sc_ref[...]) + sh_ref[...]
    h = h.reshape(tm, d).astype(BF16)

    def seg(name):
        lo, hi = _SEG[name]
        return _mm(h, win_ref[:, lo:hi])

    mq_ref[...] = seg("mq").astype(BF16)
    mk_ref[...] = (seg("mk") * (ML_DK ** -0.5)).astype(BF16)
    mv_ref[...] = seg("mv").astype(BF16)
    gz = seg("gate") + bg_ref[...]
    lane = lax.broadcasted_iota(jnp.int32, gz.shape, 1)
    g_ref[...] = jnp.where(lane < ML_H, gz, jnp.where(lane < 2 * ML_H, _log_sigmoid(gz), 0.0))
    mo_ref[...] = seg("mo").astype(BF16)
    mz_ref[...] = seg("mz").astype(BF16)
    az_ref[...] = seg("az").astype(BF16)

    cm = cm_ref[...]
    sm = sm_ref[...]
    ckvn = _rms(seg("ckv"), kvn_ref[...])
    ckv_ref[...] = ckvn
    ckvn_b = ckvn.astype(BF16)
    krp = seg("kra") * cm + seg("krr") * sm
    kr_ref[...] = krp[:, MLA_NOPE:MLA_NOPE + MLA_ROPE]

    cqn = _rms(seg("cq"), qn_ref[...]).astype(BF16)
    qa = _mm(cqn, wqa_ref[...])
    qr = _mm(cqn, wqr_ref[...])
    for hh in range(MLA_H):
        sl = slice(hh * HEAD_PAD, (hh + 1) * HEAD_PAD)
        qh = ((qa[:, sl] * cm + qr[:, sl] * sm) * Q_SCALE).astype(BF16)
        if absorb:
            o1_ref[:, hh * KV_LORA:(hh + 1) * KV_LORA] = _mm(qh, wk_ref[hh]).astype(BF16)
            o2_ref[:, sl] = qh
        else:
            o1_ref[0, hh] = qh
            o2_ref[0, hh] = (_mm(ckvn_b, wk_ref[:, sl]) + krp).astype(BF16)
    if absorb:
        o3_ref[...] = krp.astype(BF16)
    else:
        o3_ref[0] = _nt(wv_ref[...], ckvn_b).astype(BF16)


def _front(absorb, x, mod, cm, sm, wts):
    b, s, d = x.shape
    n = b * s
    tm = FRONT_TM
    if absorb:
        tb, ts = tm // s, s
        x_map = lambda i: (i, 0, 0)
        mod_map = lambda k: (lambda i: (i, 0, k))
        tab_map = lambda i: (0, 0)
    else:
        tb, ts = 1, tm
        spb = s // tm
        x_map = lambda i: (i // spb, i % spb, 0)
        mod_map = lambda k: (lambda i: (i // spb, 0, k))
        tab_map = lambda i: (i % spb, 0)
    const2 = lambda i: (0, 0)
    tok = lambda w: pl.BlockSpec((tm, w), lambda i: (i, 0))
    in_specs = [
        pl.BlockSpec((tb, ts, d), x_map),
        pl.BlockSpec((tb, 1, d), mod_map(0)),
        pl.BlockSpec((tb, 1, d), mod_map(1)),
        pl.BlockSpec((tm, LANES), tab_map),
        pl.BlockSpec((tm, LANES), tab_map),
        pl.BlockSpec((d, N_IN_PAD), const2),
        pl.BlockSpec((1, LANES), const2),
        pl.BlockSpec((1, Q_LORA), const2),
        pl.BlockSpec((1, KV_LORA), const2),
        pl.BlockSpec((Q_LORA, MLA_H * HEAD_PAD), const2),
        pl.BlockSpec((Q_LORA, MLA_H * HEAD_PAD), const2),
    ]
    out_specs = [tok(512), tok(512), tok(512), tok(LANES), tok(512), tok(512), tok(512),
                 tok(KV_LORA), tok(MLA_ROPE)]
    out_shape = [jax.ShapeDtypeStruct((n, w), dt)
                 for w, dt in ((512, BF16), (512, BF16), (512, BF16), (LANES, F32), (512, BF16), (512, BF16),
                               (512, BF16), (KV_LORA, F32), (MLA_ROPE, F32))]
    if absorb:
        in_specs += [pl.BlockSpec((MLA_H, HEAD_PAD, KV_LORA), lambda i: (0, 0, 0)),
                     pl.BlockSpec((MLA_W, KV_LORA), const2)]
        out_specs += [tok(MLA_H * KV_LORA), tok(MLA_H * HEAD_PAD), tok(HEAD_PAD)]
        out_shape += [jax.ShapeDtypeStruct((n, MLA_H * KV_LORA), BF16),
                      jax.ShapeDtypeStruct((n, MLA_H * HEAD_PAD), BF16),
                      jax.ShapeDtypeStruct((n, HEAD_PAD), BF16)]
        wk = wts["wukt"]
    else:
        head_map = lambda i: (i // spb, 0, i % spb, 0)
        in_specs += [pl.BlockSpec((KV_LORA, MLA_H * HEAD_PAD), const2),
                     pl.BlockSpec((MLA_W, KV_LORA), const2)]
        out_specs += [pl.BlockSpec((1, MLA_H, tm, HEAD_PAD), head_map),
                      pl.BlockSpec((1, MLA_H, tm, HEAD_PAD), head_map),
                      pl.BlockSpec((1, MLA_W, tm), lambda i: (i // spb, 0, i % spb))]
        out_shape += [jax.ShapeDtypeStruct((b, MLA_H, s, HEAD_PAD), BF16),
                      jax.ShapeDtypeStruct((b, MLA_H, s, HEAD_PAD), BF16),
                      jax.ShapeDtypeStruct((b, MLA_W, s), BF16)]
        wk = wts["wk_pad"]
    return pl.pallas_call(
        functools.partial(_front_kernel, absorb),
        grid=(n // tm,),
        in_specs=in_specs,
        out_specs=out_specs,
        out_shape=out_shape,
        compiler_params=pltpu.CompilerParams(dimension_semantics=("arbitrary",),
                                             vmem_limit_bytes=VMEM_LIMIT),
        name="front_sample" if absorb else "front_prompt",
    )(x, mod, mod, cm, sm, wts["w_in"], wts["b_gate"], wts["q_norm"], wts["kv_norm"],
      wts["wqa"], wts["wqr"], wk, wts["w_uv_t"])


def _mlstm_kernel(q_ref, k_ref, v_ref, g_ref, c0_ref, n0_ref, m0_ref,
                  h_ref, c_ref, n_ref, m_ref):
    tb, L, _ = q_ref.shape
    ci = pl.program_id(1)

    @pl.when(ci == 0)
    def _():
        c_ref[...] = c0_ref[...]
        n_ref[...] = n0_ref[...]
        m_ref[...] = m0_ref[...]

    row = lax.broadcasted_iota(jnp.int32, (L, L), 0)
    col = lax.broadcasted_iota(jnp.int32, (L, L), 1)
    causal = col <= row
    tril = causal.astype(F32)
    sel = (lax.broadcasted_iota(jnp.int32, (8, LANES), 0)
           == lax.broadcasted_iota(jnp.int32, (8, LANES), 1)).astype(F32)

    for t in range(tb):
        g = g_ref[t]
        fcum = _mm_exact(tril, g)
        g_rows = _nt_exact(sel, g)
        f_rows = _nt_exact(sel, fcum)
        for hh in range(ML_H):
            sl = slice(hh * ML_DK, (hh + 1) * ML_DK)
            qb = q_ref[t, :, sl]
            kb = k_ref[t, :, sl]
            vb = v_ref[t, :, sl]
            q = qb.astype(F32)
            k = kb.astype(F32)
            c_prev = c_ref[t, hh]
            n_prev = n_ref[t, hh]
            m_prev = m_ref[t, hh][:, :1]
            f_col = fcum[:, ML_H + hh:ML_H + hh + 1]
            ig_col = g[:, hh:hh + 1]
            f_row = f_rows[ML_H + hh:ML_H + hh + 1, :]
            ig_row = g_rows[hh:hh + 1, :]
            dmat = jnp.where(causal, f_col - f_row + ig_row, -jnp.inf)
            m_inter = f_col + m_prev
            m_t = jnp.maximum(m_inter, jnp.max(dmat, axis=-1, keepdims=True))
            w = jnp.exp(dmat - m_t)
            a = jnp.exp(m_inter - m_t)
            sq = _nt(qb, kb) * w
            num = _mm(sq.astype(BF16), vb) + a * _mm(qb, c_prev.astype(BF16))
            den = jnp.sum(sq, axis=-1, keepdims=True) + a * jnp.sum(q * n_prev, axis=-1, keepdims=True)
            h_ref[t, :, sl] = (num / jnp.maximum(jnp.abs(den), jnp.exp(-m_t))).astype(h_ref.dtype)
            m_last = m_t[L - 1:L, :]
            wl_col = jnp.exp(f_col[L - 1:L, :] - f_col + ig_col - m_last)
            al = a[L - 1:L, :]
            kw = k * wl_col
            c_ref[t, hh] = al * c_prev + _tn(kw.astype(BF16), vb)
            n_ref[t, hh] = al * n_prev + jnp.sum(kw, axis=0, keepdims=True)
            m_ref[t, hh] = jnp.broadcast_to(m_last, (1, LANES))


def _mlstm(q, k, v, g, c0, n0, m0, chunk, tb):
    b, s, w = q.shape
    nc = s // chunk
    seq = lambda ww: pl.BlockSpec((tb, chunk, ww), lambda i, c: (i, c, 0))
    st_c = pl.BlockSpec((tb, ML_H, ML_DK, ML_DV), lambda i, c: (i, 0, 0, 0))
    st_v = pl.BlockSpec((tb, ML_H, 1, LANES), lambda i, c: (i, 0, 0, 0))
    return pl.pallas_call(
        _mlstm_kernel,
        grid=(b // tb, nc),
        in_specs=[seq(w), seq(w), seq(w), seq(LANES), st_c, st_v, st_v],
        out_specs=[seq(w), st_c, st_v, st_v],
        out_shape=[jax.ShapeDtypeStruct((b, s, w), BF16),
                   jax.ShapeDtypeStruct((b, ML_H, ML_DK, ML_DV), F32),
                   jax.ShapeDtypeStruct((b, ML_H, 1, LANES), F32),
                   jax.ShapeDtypeStruct((b, ML_H, 1, LANES), F32)],
        compiler_params=pltpu.CompilerParams(dimension_semantics=("arbitrary", "arbitrary"),
                                             vmem_limit_bytes=VMEM_LIMIT),
        name="mlstm",
    )(q, k, v, g, c0, n0, m0)


def _attn_kernel(qi_ref, ki_ref, last_ref, q_ref, k_ref, vt_ref, o_ref, m_sc, l_sc, acc_sc):
    step = pl.program_id(2)
    qi = qi_ref[step]
    ki = ki_ref[step]
    tq = q_ref.shape[2]
    tk = k_ref.shape[2]

    @pl.when(ki == 0)
    def _():
        m_sc[...] = jnp.full(m_sc.shape, -jnp.inf, F32)
        l_sc[...] = jnp.zeros(l_sc.shape, F32)
        acc_sc[...] = jnp.zeros(acc_sc.shape, F32)

    def tile(masked):
        m_all, l_all, acc_all = m_sc[...], l_sc[...], acc_sc[...]
        m_out, l_out, acc_out = [], [], []
        for hh in range(2):
            rows = slice(hh * MLA_VD, (hh + 1) * MLA_VD)
            k = k_ref[0, hh]
            vt = vt_ref[0, rows, :]
            m_h, l_h, acc_h = [], [], []
            for c0 in range(0, tq, ATT_QC):
                qs = slice(c0, c0 + ATT_QC)
                st = _nt(k, q_ref[0, hh, qs, :])
                if masked:
                    keys = ki * tk + lax.broadcasted_iota(jnp.int32, (tk, ATT_QC), 0)
                    qrys = qi * tq + c0 + lax.broadcasted_iota(jnp.int32, (tk, ATT_QC), 1)
                    st = jnp.where(keys <= qrys, st, -jnp.inf)
                m_prev = m_all[hh, :, qs]
                m_new = jnp.maximum(m_prev, jnp.max(st, axis=0, keepdims=True))
                alpha = jnp.exp2(m_prev - m_new)
                pt = jnp.exp2(st - m_new)
                l_h.append(alpha * l_all[hh, :, qs] + jnp.sum(pt, axis=0, keepdims=True))
                m_h.append(m_new)
                acc_h.append(acc_all[rows, qs] * alpha + _mm(vt, pt.astype(BF16)))
            m_out.append(jnp.concatenate(m_h, axis=1))
            l_out.append(jnp.concatenate(l_h, axis=1))
            acc_out.append(jnp.concatenate(acc_h, axis=1))
        m_sc[...] = jnp.stack(m_out)
        l_sc[...] = jnp.stack(l_out)
        acc_sc[...] = jnp.concatenate(acc_out, axis=0)

    crosses_diagonal = (ki + 1) * tk - 1 > qi * tq
    pl.when(crosses_diagonal)(functools.partial(tile, True))
    pl.when(jnp.logical_not(crosses_diagonal))(functools.partial(tile, False))

    @pl.when(last_ref[step] == 1)
    def _():
        out_t = jnp.concatenate([acc_sc[0:MLA_VD, :] / l_sc[0], acc_sc[MLA_VD:, :] / l_sc[1]], axis=0)
        o_ref[0] = out_t.T.astype(o_ref.dtype)


def _attn_prompt(qh, kh, vt):
    b, nh, s, _ = qh.shape
    tq, tk = ATT_TQ, ATT_TK
    qi_l, ki_l, last_l = [], [], []
    for qi in range(s // tq):
        nk = ((qi + 1) * tq + tk - 1) // tk
        for ki in range(nk):
            qi_l.append(qi)
            ki_l.append(ki)
            last_l.append(int(ki == nk - 1))
    nsteps = len(qi_l)
    sched = [jnp.asarray(np.asarray(a, np.int32)) for a in (qi_l, ki_l, last_l)]
    grid_spec = pltpu.PrefetchScalarGridSpec(
        num_scalar_prefetch=3,
        grid=(b, nh // 2, nsteps),
        in_specs=[pl.BlockSpec((1, 2, tq, HEAD_PAD), lambda bi, hp, st, qi, ki, la: (bi, hp, qi[st], 0)),
                  pl.BlockSpec((1, 2, tk, HEAD_PAD), lambda bi, hp, st, qi, ki, la: (bi, hp, ki[st], 0)),
                  pl.BlockSpec((1, 2 * MLA_VD, tk), lambda bi, hp, st, qi, ki, la: (bi, hp, ki[st]))],
        out_specs=pl.BlockSpec((1, tq, LANES), lambda bi, hp, st, qi, ki, la: (bi, qi[st], hp)),
        scratch_shapes=[pltpu.VMEM((2, 1, tq), F32), pltpu.VMEM((2, 1, tq), F32),
                        pltpu.VMEM((2 * MLA_VD, tq), F32)],
    )
    return pl.pallas_call(
        _attn_kernel,
        grid_spec=grid_spec,
        out_shape=jax.ShapeDtypeStruct((b, s, nh * MLA_VD), BF16),
        compiler_params=pltpu.CompilerParams(
            dimension_semantics=("arbitrary", "arbitrary", "arbitrary"),
            vmem_limit_bytes=VMEM_LIMIT),
        name="attn_prompt",
    )(*sched, qh, kh, vt)


def _dec_page_copies(pt_ref, pool_ckv, pool_kr, cbuf, kbuf, sems, step, slot):
    copies = []
    for i in range(DEC_PG):
        page = pt_ref[step * DEC_PG + i]
        keys = pl.ds(i * PAGE_SIZE, PAGE_SIZE)
        copies.append(pltpu.make_async_copy(pool_ckv.at[page], cbuf.at[slot, keys, :], sems.at[slot, 0]))
        copies.append(pltpu.make_async_copy(pool_kr.at[page], kbuf.at[slot, :, keys], sems.at[slot, 1]))
    return copies


def _dec_kernel(pt_ref, qlat_ref, qh_ref, ckvn_ref, krn_ref, pool_ckv, pool_kr,
                o_ref, m_sc, l_sc, acc_sc, padc_sc, padk_sc, cbuf, kbuf, sems):
    j = pl.program_id(1)
    nj = pl.num_programs(1)
    step = pl.program_id(0) * nj + j
    last_step = pl.num_programs(0) * nj - 1
    slot = step % 2
    qlat = qlat_ref[0]
    qh = qh_ref[0]
    ds = ckvn_ref.shape[1]
    copies = functools.partial(_dec_page_copies, pt_ref, pool_ckv, pool_kr, cbuf, kbuf, sems)

    @pl.when(step == 0)
    def _():
        for c in copies(0, 0):
            c.start()

    for c in copies(step + 1, 1 - slot):
        c.start()
    for c in copies(step, slot):
        c.wait()

    @pl.when(j == 0)
    def _():
        padc_sc[...] = jnp.zeros(padc_sc.shape, BF16)
        padk_sc[...] = jnp.zeros(padk_sc.shape, BF16)
        padc_sc[0:ds, :] = ckvn_ref[0].astype(BF16)
        padk_sc[0:ds, :] = krn_ref[0]
        cn = padc_sc[...]
        s = _nt(qlat, cn) + _nt(qh, padk_sc[...])
        tok = lax.broadcasted_iota(jnp.int32, s.shape, 0) // MLA_H
        key = lax.broadcasted_iota(jnp.int32, s.shape, 1)
        s = jnp.where(key <= tok, s, -jnp.inf)
        m = jnp.max(s, axis=-1, keepdims=True)
        p = jnp.exp2(s - m)
        m_sc[...] = m
        l_sc[...] = jnp.sum(p, axis=-1, keepdims=True)
        acc_sc[...] = _mm(p.astype(BF16), cn)

    qr = qh[:, MLA_NOPE:MLA_NOPE + MLA_ROPE]
    pages = cbuf[slot].astype(BF16)
    krt = kbuf[slot].astype(BF16)
    gk = pages.shape[0] // DEC_GROUPS
    parts = []
    for g in range(DEC_GROUPS):
        pg_g = pages[g * gk:(g + 1) * gk]
        s = _nt(qlat, pg_g) + _mm(qr, krt[:, g * gk:(g + 1) * gk])
        m_g = jnp.max(s, axis=-1, keepdims=True)
        p = jnp.exp2(s - m_g)
        parts.append((m_g, jnp.sum(p, axis=-1, keepdims=True), _mm(p.astype(BF16), pg_g)))
    m_prev = m_sc[...]
    m_new = functools.reduce(jnp.maximum, [m_g for m_g, _, _ in parts], m_prev)
    alpha = jnp.exp2(m_prev - m_new)
    l_new = alpha * l_sc[...]
    acc = alpha * acc_sc[...]
    for m_g, l_g, o_g in parts:
        w_g = jnp.exp2(m_g - m_new)
        l_new = l_new + w_g * l_g
        acc = acc + w_g * o_g
    m_sc[...] = m_new
    l_sc[...] = l_new
    acc_sc[...] = acc

    @pl.when(j == nj - 1)
    def _():
        o_ref[0] = acc_sc[...] / l_sc[...]

    @pl.when(step == last_step)
    def _():
        for c in copies(step + 1, 1 - slot):
            c.wait()


def _attn_sample(qlat, qh, ckvn, krn, pool_ckv, pool_kr, page_table):
    b, nq, _ = qlat.shape
    ds = ckvn.shape[1]
    n_pages = page_table.shape[1]
    pg = DEC_PG
    pt = page_table.reshape(-1)
    pt = jnp.concatenate([pt, pt[:pg]])

    fixed = lambda w, r: pl.BlockSpec((1, r, w), lambda bi, j, pt_ref: (bi, 0, 0))
    grid_spec = pltpu.PrefetchScalarGridSpec(
        num_scalar_prefetch=1,
        grid=(b, n_pages // pg),
        in_specs=[fixed(KV_LORA, nq), fixed(HEAD_PAD, nq), fixed(KV_LORA, ds), fixed(HEAD_PAD, ds),
                  pl.BlockSpec(memory_space=pl.ANY), pl.BlockSpec(memory_space=pl.ANY)],
        out_specs=fixed(KV_LORA, nq),
        scratch_shapes=[pltpu.VMEM((nq, 1), F32), pltpu.VMEM((nq, 1), F32),
                        pltpu.VMEM((nq, KV_LORA), F32),
                        pltpu.VMEM((PAGE_SIZE, KV_LORA), BF16), pltpu.VMEM((PAGE_SIZE, HEAD_PAD), BF16),
                        pltpu.VMEM((2, pg * PAGE_SIZE, KV_LORA), F32),
                        pltpu.VMEM((2, MLA_ROPE, pg * PAGE_SIZE), F32),
                        pltpu.SemaphoreType.DMA((2, 2))],
    )
    return pl.pallas_call(
        _dec_kernel,
        grid_spec=grid_spec,
        out_shape=jax.ShapeDtypeStruct((b, nq, KV_LORA), F32),
        compiler_params=pltpu.CompilerParams(dimension_semantics=("arbitrary", "arbitrary"),
                                             vmem_limit_bytes=VMEM_LIMIT),
        name="attn_sample",
    )(pt, qlat, qh, ckvn, krn, pool_ckv, pool_kr)


def _back_kernel(absorb, x_ref, gate_ref, hml_ref, mo_ref, mz_ref, o_ref, az_ref, gn_ref, wuv_ref,
                 wout_ref, lng_ref, lnb_ref, y_ref):
    tb, ts, d = x_ref.shape
    tm = tb * ts
    hm = hml_ref[...].astype(F32) * jax.nn.sigmoid(mo_ref[...].astype(F32))
    parts = []
    for hh in range(ML_H):
        t = hm[:, hh * ML_DV:(hh + 1) * ML_DV]
        mu = jnp.mean(t, axis=-1, keepdims=True)
        tc = t - mu
        var = jnp.mean(tc * tc, axis=-1, keepdims=True)
        parts.append(tc * lax.rsqrt(var + EPS))
    mz = mz_ref[...].astype(F32)
    y_ml = jnp.concatenate(parts, axis=1) * gn_ref[...] * (mz * jax.nn.sigmoid(mz))
    if absorb:
        olat = o_ref[...].astype(BF16)
        o_mla = jnp.concatenate(
            [_mm(olat[:, p * 2 * KV_LORA:(p + 1) * 2 * KV_LORA], wuv_ref[p]) for p in range(MLA_H // 2)],
            axis=1)
    else:
        o_mla = o_ref[...].astype(F32)
    az = az_ref[...].astype(F32)
    y_mla = o_mla * (az * jax.nn.sigmoid(az))
    out = _mm(y_ml.astype(BF16), wout_ref[0:ML_W, :]) + _mm(y_mla.astype(BF16), wout_ref[ML_W:, :])
    z = ALPHA * x_ref[...] + gate_ref[...] * out.reshape(tb, ts, d)
    mu = jnp.mean(z, axis=-1, keepdims=True)
    zc = z - mu
    var = jnp.mean(zc * zc, axis=-1, keepdims=True)
    y_ref[...] = zc * lax.rsqrt(var + EPS) * lng_ref[...] + lnb_ref[...]


def _back(absorb, x, mod, hml, mo, mz, o, az, wts):
    b, s, d = x.shape
    n = b * s
    tm = FRONT_TM
    if absorb:
        tb, ts = tm // s, s
        x_map = lambda i: (i, 0, 0)
        gate_map = lambda i: (i, 0, 2)
    else:
        tb, ts = 1, tm
        spb = s // tm
        x_map = lambda i: (i // spb, i % spb, 0)
        gate_map = lambda i: (i // spb, 0, 2)
    const2 = lambda i: (0, 0)
    tok = lambda w: pl.BlockSpec((tm, w), lambda i: (i, 0))
    ow = o.shape[1]
    return pl.pallas_call(
        functools.partial(_back_kernel, absorb),
        grid=(n // tm,),
        in_specs=[pl.BlockSpec((tb, ts, d), x_map),
                  pl.BlockSpec((tb, 1, d), gate_map),
                  tok(ML_W), tok(ML_W), tok(ML_W), tok(ow), tok(MLA_W),
                  pl.BlockSpec((1, ML_W), const2),
                  pl.BlockSpec((MLA_H // 2, 2 * KV_LORA, LANES), lambda i: (0, 0, 0)),
                  pl.BlockSpec((d, d), const2),
                  pl.BlockSpec((1, d), const2),
                  pl.BlockSpec((1, d), const2)],
        out_specs=pl.BlockSpec((tb, ts, d), x_map),
        out_shape=jax.ShapeDtypeStruct((b, s, d), F32),
        compiler_params=pltpu.CompilerParams(dimension_semantics=("arbitrary",),
                                             vmem_limit_bytes=VMEM_LIMIT),
        name="back_sample" if absorb else "back_prompt",
    )(x, mod, hml, mo, mz, o, az, wts["gn"], wts["wuv_pair"], wts["w_out"], wts["ln_g"], wts["ln_b"])


def _prep_weights(l, w_in, ml_b_i, ml_b_f, ml_gn, mla_q_norm, mla_kv_norm, mla_w_uq, mla_w_uk,
                  mla_w_uv, w_out, ln_g, ln_b):
    offs = np.cumsum((0, 512, 512, 512, ML_H, ML_H, 512, 512, Q_LORA, KV_LORA, MLA_ROPE, 512))
    names = ("mq", "mk", "mv", "mi", "mf", "mo", "mz", "cq", "ckv", "kr", "az")
    w = {nm: w_in[l][:, offs[i]:offs[i + 1]] for i, nm in enumerate(names)}
    d = w_in.shape[1]
    z = lambda c: jnp.zeros((d, c), F32)
    kr1, kr2 = w["kr"][:, :ROPE_HALF], w["kr"][:, ROPE_HALF:]
    tail = HEAD_PAD - MLA_NOPE - MLA_ROPE
    w_in_p = jnp.concatenate(
        [w["mq"], w["mk"], w["mv"], w["mi"], w["mf"], z(LANES - 2 * ML_H), w["mo"], w["mz"], w["cq"], w["ckv"],
         z(MLA_NOPE), kr1, kr2, z(tail), z(MLA_NOPE), -kr2, kr1, z(tail), w["az"]], axis=1).astype(BF16)
    b_gate = jnp.concatenate([ml_b_i[l], ml_b_f[l], jnp.zeros((LANES - 2 * ML_H,), F32)]).reshape(1, LANES)
    uq = mla_w_uq[l].reshape(Q_LORA, MLA_H, MLA_NOPE + MLA_ROPE)
    nope, r1, r2 = uq[..., :MLA_NOPE], uq[..., MLA_NOPE:MLA_NOPE + ROPE_HALF], uq[..., MLA_NOPE + ROPE_HALF:]
    zq = lambda c: jnp.zeros((Q_LORA, MLA_H, c), F32)
    wqa = jnp.concatenate([nope, r1, r2, zq(tail)], axis=-1).reshape(Q_LORA, MLA_H * HEAD_PAD).astype(BF16)
    wqr = jnp.concatenate([zq(MLA_NOPE), -r2, r1, zq(tail)], axis=-1).reshape(Q_LORA, MLA_H * HEAD_PAD).astype(BF16)
    uk = mla_w_uk[l].reshape(KV_LORA, MLA_H, MLA_NOPE)
    wk_pad = jnp.concatenate([uk, jnp.zeros((KV_LORA, MLA_H, HEAD_PAD - MLA_NOPE), F32)],
                             axis=-1).reshape(KV_LORA, MLA_H * HEAD_PAD).astype(BF16)
    wukt = jnp.concatenate([jnp.transpose(uk, (1, 2, 0)),
                            jnp.zeros((MLA_H, HEAD_PAD - MLA_NOPE, KV_LORA), F32)], axis=1).astype(BF16)
    uv = mla_w_uv[l].reshape(KV_LORA, MLA_H // 2, 2, MLA_VD)
    zv = jnp.zeros((KV_LORA, MLA_H // 2, MLA_VD), F32)
    wuv_pair = jnp.concatenate(
        [jnp.concatenate([uv[:, :, 0], zv], axis=-1), jnp.concatenate([zv, uv[:, :, 1]], axis=-1)],
        axis=0)
    wuv_pair = jnp.transpose(wuv_pair, (1, 0, 2)).astype(BF16)
    return dict(w_in=w_in_p, b_gate=b_gate, q_norm=mla_q_norm[l].reshape(1, -1),
                kv_norm=mla_kv_norm[l].reshape(1, -1), wqa=wqa, wqr=wqr, wk_pad=wk_pad, wukt=wukt,
                w_uv_t=mla_w_uv[l].T.astype(BF16), wuv_pair=wuv_pair, gn=ml_gn[l].reshape(1, -1),
                w_out=w_out[l].astype(BF16), ln_g=ln_g[l].reshape(1, -1), ln_b=ln_b[l].reshape(1, -1))


def _rope_tables(pos):
    f32 = np.float32
    inv = ROPE_THETA ** (-np.arange(ROPE_HALF, dtype=np.float64) / ROPE_HALF)
    ang = pos.astype(np.float64)[:, None] * inv[None, :]
    n = pos.shape[0]
    tail = HEAD_PAD - MLA_NOPE - MLA_ROPE
    cos, sin = np.cos(ang).astype(f32), np.sin(ang).astype(f32)
    cm = np.concatenate([np.ones((n, MLA_NOPE), f32), cos, cos, np.zeros((n, tail), f32)], 1)
    sm = np.concatenate([np.zeros((n, MLA_NOPE), f32), sin, sin, np.zeros((n, tail), f32)], 1)
    return cm, sm


def kernel(x_prompt, x_sample, c_prompt, c_sample, cache_ckv, cache_krope, state_C, state_n, state_m,
           page_table, w_ada, b_ada, w_in, ml_b_i, ml_b_f, ml_gn, mla_q_norm, mla_kv_norm,
           mla_w_uq, mla_w_uk, mla_w_uv, w_out, ln_g, ln_b):
    bp, sp, d = x_prompt.shape
    bs, ss, _ = x_sample.shape
    past = page_table.shape[1] * PAGE_SIZE
    cm_p, sm_p = (jnp.asarray(t) for t in _rope_tables(np.arange(sp)))
    reps = FRONT_TM // ss
    cm_s, sm_s = (jnp.asarray(np.tile(t, (reps, 1))) for t in _rope_tables(past + np.arange(ss)))
    pad = (-(bp + bs)) % 8
    c_all = jnp.concatenate([c_prompt, c_sample, jnp.zeros((pad, d), F32)], axis=0)

    xp, xs = x_prompt, x_sample
    outs = [[] for _ in range(10)]
    for l in range(DEPTH):
        wts = _prep_weights(l, w_in, ml_b_i, ml_b_f, ml_gn, mla_q_norm, mla_kv_norm, mla_w_uq, mla_w_uk,
                            mla_w_uv, w_out, ln_g, ln_b)
        mod = _ada(c_all, w_ada[l], b_ada[l])
        mod_p = mod[:bp].reshape(bp, 1, 3 * d)
        mod_s = mod[bp:bp + bs].reshape(bs, 1, 3 * d)

        mq, mk, mv, g, mo, mz, az, ckv, kr, qh, kh, vt = _front(False, xp, mod_p, cm_p, sm_p, wts)
        seq3 = lambda t: t.reshape(bp, sp, -1)
        c0 = jnp.zeros((bp, ML_H, ML_DK, ML_DV), F32)
        n0 = jnp.zeros((bp, ML_H, 1, ML_DK), F32)
        m0 = jnp.full((bp, ML_H, 1, LANES), -jnp.inf, F32)
        hml, c_p, n_p, m_p = _mlstm(seq3(mq), seq3(mk), seq3(mv), seq3(g), c0, n0, m0, ML_CHUNK, bp)
        o_p = _attn_prompt(qh, kh, vt)
        yp = _back(False, xp, mod_p, hml.reshape(bp * sp, -1), mo, mz, o_p.reshape(bp * sp, -1), az, wts)
        outs[0].append(ckv.reshape(bp, sp, KV_LORA))
        outs[1].append(kr.reshape(bp, sp, MLA_ROPE))
        outs[2].append(c_p)
        outs[3].append(n_p[:, :, 0, :])
        outs[4].append(m_p[:, :, 0, 0])

        mq, mk, mv, g, mo, mz, az, ckv, kr, qlat, qh, krp = _front(True, xs, mod_s, cm_s, sm_s, wts)
        seq3 = lambda t: t.reshape(bs, ss, -1)
        n0 = state_n[l].astype(F32).reshape(bs, ML_H, 1, ML_DK)
        m0 = jnp.broadcast_to(state_m[l].astype(F32)[:, :, None, None], (bs, ML_H, 1, LANES))
        hml, c_s, n_s, m_s = _mlstm(seq3(mq), seq3(mk), seq3(mv), seq3(g), state_C[l].astype(F32), n0, m0, ss, 8)
        o_s = _attn_sample(qlat.reshape(bs, ss * MLA_H, KV_LORA), qh.reshape(bs, ss * MLA_H, HEAD_PAD),
                           seq3(ckv), seq3(krp), cache_ckv[l], jnp.swapaxes(cache_krope[l], 1, 2), page_table)
        ys = _back(True, xs, mod_s, hml.reshape(bs * ss, -1), mo, mz,
                   o_s.reshape(bs * ss, MLA_H * KV_LORA), az, wts)
        outs[5].append(ckv.reshape(bs, ss, KV_LORA).astype(cache_ckv.dtype))
        outs[6].append(kr.reshape(bs, ss, MLA_ROPE).astype(cache_krope.dtype))
        outs[7].append(c_s.astype(state_C.dtype))
        outs[8].append(n_s[:, :, 0, :].astype(state_n.dtype))
        outs[9].append(m_s[:, :, 0, 0].astype(state_m.dtype))
        xp, xs = yp, ys
    return (xp, xs) + tuple(jnp.stack(o) for o in outs)
```

```python
import functools

import numpy as np
import jax
import jax.numpy as jnp
from jax import lax
from jax.experimental import pallas as pl
from jax.experimental.pallas import tpu as pltpu

F32 = jnp.float32
BF16 = jnp.bfloat16

D_MODEL = 1024
DEPTH = 1
PAGE_SIZE = 128
ML_W = 512
MLA_W = 512
ML_DK = 128
ML_DV = 128
ML_H = 4
MLA_VD = 64
MLA_H = 8
MLA_NOPE = 64
MLA_ROPE = 32
ROPE_HALF = MLA_ROPE // 2
MLA_SCALE = (MLA_NOPE + MLA_ROPE) ** -0.5
Q_SCALE = MLA_SCALE * float(np.log2(np.e))
Q_LORA = 384
KV_LORA = 256
ROPE_THETA = 10000.0
ML_CHUNK = 128
EPS = 1e-6
ALPHA = (2.0 * DEPTH) ** 0.25

LANES = 128
HEAD_PAD = 128
VMEM_LIMIT = 48 * 1024 * 1024

_SEG = {}
_off = 0
for _name, _w in (("mq", 512), ("mk", 512), ("mv", 512), ("gate", 128), ("mo", 512), ("mz", 512),
                  ("cq", Q_LORA), ("ckv", KV_LORA), ("kra", 128), ("krr", 128), ("az", 512)):
    _SEG[_name] = (_off, _off + _w)
    _off += _w
N_IN_PAD = _off

FRONT_TM = 256
ATT_TQ = 512
ATT_TK = 1024
ATT_KC = 256
MASKED_MAX_FLOOR = -1e30
DEC_PG = 64
DEC_GROUPS = 8


def _nt(a, b):
    return lax.dot_general(a, b, (((1,), (1,)), ((), ())), preferred_element_type=F32)


def _tn(a, b):
    return lax.dot_general(a, b, (((0,), (0,)), ((), ())), preferred_element_type=F32)


def _mm(a, b):
    return jnp.dot(a, b, preferred_element_type=F32)


def _mm_exact(a, b):
    return jnp.dot(a, b, preferred_element_type=F32, precision=lax.Precision.HIGHEST)


def _nt_exact(a, b):
    return lax.dot_general(a, b, (((1,), (1,)), ((), ())), preferred_element_type=F32,
                           precision=lax.Precision.HIGHEST)


def _ada_kernel(c_ref, w_ref, b_ref, o_ref):
    o_ref[...] = _mm(c_ref[...].astype(BF16), w_ref[...].astype(BF16)) + b_ref[...]


def _ada(c_all, w_ada, b_ada):
    m = c_all.shape[0]
    tn = 512
    return pl.pallas_call(
        _ada_kernel,
        grid=(3 * D_MODEL // tn,),
        in_specs=[pl.BlockSpec((m, D_MODEL), lambda j: (0, 0)),
                  pl.BlockSpec((D_MODEL, tn), lambda j: (0, j)),
                  pl.BlockSpec((1, tn), lambda j: (0, j))],
        out_specs=pl.BlockSpec((m, tn), lambda j: (0, j)),
        out_shape=jax.ShapeDtypeStruct((m, 3 * D_MODEL), F32),
        name="ada",
    )(c_all, w_ada, b_ada.reshape(1, -1))


def _rms(x, g):
    return x * lax.rsqrt(jnp.mean(x * x, axis=-1, keepdims=True) + EPS) * g


def _log_sigmoid(x):
    return jnp.minimum(x, 0.0) - jnp.log1p(jnp.exp(-jnp.abs(x)))


def _front_kernel(absorb, x_ref, sh_ref, sc_ref, cm_ref, sm_ref, win_ref, bg_ref, qn_ref, kvn_ref,
                  wqa_ref, wqr_ref, wk_ref, wv_ref,
                  mq_ref, mk_ref, mv_ref, g_ref, mo_ref, mz_ref, az_ref, ckv_ref, kr_ref,
                  o1_ref, o2_ref, o3_ref):
    tb, ts, d = x_ref.shape
    tm = tb * ts
    h = x_ref[...] * (1.0 + sc_ref[...]) + sh_ref[...]
    h = h.reshape(tm, d).astype(BF16)

    def seg(name):
        lo, hi = _SEG[name]
        return _mm(h, win_ref[:, lo:hi])

    mq_ref[...] = seg("mq").astype(BF16)
    mk_ref[...] = (seg("mk") * (ML_DK ** -0.5)).astype(BF16)
    mv_ref[...] = seg("mv").astype(BF16)
    gz = seg("gate") + bg_ref[...]
    lane = lax.broadcasted_iota(jnp.int32, gz.shape, 1)
    g_ref[...] = jnp.where(lane < ML_H, gz, jnp.where(lane < 2 * ML_H, _log_sigmoid(gz), 0.0))
    mo_ref[...] = seg("mo").astype(BF16)
    mz_ref[...] = seg("mz").astype(BF16)
    az_ref[...] = seg("az").astype(BF16)

    cm = cm_ref[...]
    sm = sm_ref[...]
    ckvn = _rms(seg("ckv"), kvn_ref[...])
    ckv_ref[...] = ckvn
    ckvn_b = ckvn.astype(BF16)
    krp = seg("kra") * cm + seg("krr") * sm
    kr_ref[...] = krp[:, MLA_NOPE:MLA_NOPE + MLA_ROPE]

    cqn = _rms(seg("cq"), qn_ref[...]).astype(BF16)
    qa = _mm(cqn, wqa_ref[...])
    qr = _mm(cqn, wqr_ref[...])
    for hh in range(MLA_H):
        sl = slice(hh * HEAD_PAD, (hh + 1) * HEAD_PAD)
        qh = ((qa[:, sl] * cm + qr[:, sl] * sm) * Q_SCALE).astype(BF16)
        if absorb:
            o1_ref[:, hh * KV_LORA:(hh + 1) * KV_LORA] = _mm(qh, wk_ref[hh]).astype(BF16)
            o2_ref[:, sl] = qh
        else:
            o1_ref[0, hh] = qh
            o2_ref[0, hh] = (_mm(ckvn_b, wk_ref[:, sl]) + krp).astype(BF16)
    if absorb:
        o3_ref[...] = krp.astype(BF16)
    else:
        o3_ref[0] = _nt(wv_ref[...], ckvn_b).astype(BF16)


def _front(absorb, x, mod, cm, sm, wts):
    b, s, d = x.shape
    n = b * s
    tm = FRONT_TM
    if absorb:
        tb, ts = tm // s, s
        x_map = lambda i: (i, 0, 0)
        mod_map = lambda k: (lambda i: (i, 0, k))
        tab_map = lambda i: (0, 0)
    else:
        tb, ts = 1, tm
        spb = s // tm
        x_map = lambda i: (i // spb, i % spb, 0)
        mod_map = lambda k: (lambda i: (i // spb, 0, k))
        tab_map = lambda i: (i % spb, 0)
    const2 = lambda i: (0, 0)
    tok = lambda w: pl.BlockSpec((tm, w), lambda i: (i, 0))
    in_specs = [
        pl.BlockSpec((tb, ts, d), x_map),
        pl.BlockSpec((tb, 1, d), mod_map(0)),
        pl.BlockSpec((tb, 1, d), mod_map(1)),
        pl.BlockSpec((tm, LANES), tab_map),
        pl.BlockSpec((tm, LANES), tab_map),
        pl.BlockSpec((d, N_IN_PAD), const2),
        pl.BlockSpec((1, LANES), const2),
        pl.BlockSpec((1, Q_LORA), const2),
        pl.BlockSpec((1, KV_LORA), const2),
        pl.BlockSpec((Q_LORA, MLA_H * HEAD_PAD), const2),
        pl.BlockSpec((Q_LORA, MLA_H * HEAD_PAD), const2),
    ]
    out_specs = [tok(512), tok(512), tok(512), tok(LANES), tok(512), tok(512), tok(512),
                 tok(KV_LORA), tok(MLA_ROPE)]
    out_shape = [jax.ShapeDtypeStruct((n, w), dt)
                 for w, dt in ((512, BF16), (512, BF16), (512, BF16), (LANES, F32), (512, BF16), (512, BF16),
                               (512, BF16), (KV_LORA, F32), (MLA_ROPE, F32))]
    if absorb:
        in_specs += [pl.BlockSpec((MLA_H, HEAD_PAD, KV_LORA), lambda i: (0, 0, 0)),
                     pl.BlockSpec((MLA_W, KV_LORA), const2)]
        out_specs += [tok(MLA_H * KV_LORA), tok(MLA_H * HEAD_PAD), tok(HEAD_PAD)]
        out_shape += [jax.ShapeDtypeStruct((n, MLA_H * KV_LORA), BF16),
                      jax.ShapeDtypeStruct((n, MLA_H * HEAD_PAD), BF16),
                      jax.ShapeDtypeStruct((n, HEAD_PAD), BF16)]
        wk = wts["wukt"]
    else:
        head_map = lambda i: (i // spb, 0, i % spb, 0)
        in_specs += [pl.BlockSpec((KV_LORA, MLA_H * HEAD_PAD), const2),
                     pl.BlockSpec((MLA_W, KV_LORA), const2)]
        out_specs += [pl.BlockSpec((1, MLA_H, tm, HEAD_PAD), head_map),
                      pl.BlockSpec((1, MLA_H, tm, HEAD_PAD), head_map),
                      pl.BlockSpec((1, MLA_W, tm), lambda i: (i // spb, 0, i % spb))]
        out_shape += [jax.ShapeDtypeStruct((b, MLA_H, s, HEAD_PAD), BF16),
                      jax.ShapeDtypeStruct((b, MLA_H, s, HEAD_PAD), BF16),
                      jax.ShapeDtypeStruct((b, MLA_W, s), BF16)]
        wk = wts["wk_pad"]
    return pl.pallas_call(
        functools.partial(_front_kernel, absorb),
        grid=(n // tm,),
        in_specs=in_specs,
        out_specs=out_specs,
        out_shape=out_shape,
        compiler_params=pltpu.CompilerParams(dimension_semantics=("arbitrary",),
                                             vmem_limit_bytes=VMEM_LIMIT),
        name="front_sample" if absorb else "front_prompt",
    )(x, mod, mod, cm, sm, wts["w_in"], wts["b_gate"], wts["q_norm"], wts["kv_norm"],
      wts["wqa"], wts["wqr"], wk, wts["w_uv_t"])


def _mlstm_kernel(q_ref, k_ref, v_ref, g_ref, c0_ref, n0_ref, m0_ref,
                  h_ref, c_ref, n_ref, m_ref):
    tb, L, _ = q_ref.shape
    ci = pl.program_id(1)

    @pl.when(ci == 0)
    def _():
        c_ref[...] = c0_ref[...]
        n_ref[...] = n0_ref[...]
        m_ref[...] = m0_ref[...]

    row = lax.broadcasted_iota(jnp.int32, (L, L), 0)
    col = lax.broadcasted_iota(jnp.int32, (L, L), 1)
    causal = col <= row
    tril = causal.astype(F32)
    sel = (lax.broadcasted_iota(jnp.int32, (8, LANES), 0)
           == lax.broadcasted_iota(jnp.int32, (8, LANES), 1)).astype(F32)

    for t in range(tb):
        g = g_ref[t]
        fcum = _mm_exact(tril, g)
        g_rows = _nt_exact(sel, g)
        f_rows = _nt_exact(sel, fcum)
        for hh in range(ML_H):
            sl = slice(hh * ML_DK, (hh + 1) * ML_DK)
            qb = q_ref[t, :, sl]
            kb = k_ref[t, :, sl]
            vb = v_ref[t, :, sl]
            q = qb.astype(F32)
            k = kb.astype(F32)
            c_prev = c_ref[t, hh]
            n_prev = n_ref[t, hh]
            m_prev = m_ref[t, hh][:, :1]
            f_col = fcum[:, ML_H + hh:ML_H + hh + 1]
            ig_col = g[:, hh:hh + 1]
            f_row = f_rows[ML_H + hh:ML_H + hh + 1, :]
            ig_row = g_rows[hh:hh + 1, :]
            dmat = jnp.where(causal, f_col - f_row + ig_row, -jnp.inf)
            m_inter = f_col + m_prev
            m_t = jnp.maximum(m_inter, jnp.max(dmat, axis=-1, keepdims=True))
            w = jnp.exp(dmat - m_t)
            a = jnp.exp(m_inter - m_t)
            sq = _nt(qb, kb) * w
            num = _mm(sq.astype(BF16), vb) + a * _mm(qb, c_prev.astype(BF16))
            den = jnp.sum(sq, axis=-1, keepdims=True) + a * jnp.sum(q * n_prev, axis=-1, keepdims=True)
            h_ref[t, :, sl] = (num / jnp.maximum(jnp.abs(den), jnp.exp(-m_t))).astype(h_ref.dtype)
            m_last = m_t[L - 1:L, :]
            wl_col = jnp.exp(f_col[L - 1:L, :] - f_col + ig_col - m_last)
            al = a[L - 1:L, :]
            kw = k * wl_col
            c_ref[t, hh] = al * c_prev + _tn(kw.astype(BF16), vb)
            n_ref[t, hh] = al * n_prev + jnp.sum(kw, axis=0, keepdims=True)
            m_ref[t, hh] = jnp.broadcast_to(m_last, (1, LANES))


def _mlstm(q, k, v, g, c0, n0, m0, chunk, tb):
    b, s, w = q.shape
    nc = s // chunk
    seq = lambda ww: pl.BlockSpec((tb, chunk, ww), lambda i, c: (i, c, 0))
    st_c = pl.BlockSpec((tb, ML_H, ML_DK, ML_DV), lambda i, c: (i, 0, 0, 0))
    st_v = pl.BlockSpec((tb, ML_H, 1, LANES), lambda i, c: (i, 0, 0, 0))
    return pl.pallas_call(
        _mlstm_kernel,
        grid=(b // tb, nc),
        in_specs=[seq(w), seq(w), seq(w), seq(LANES), st_c, st_v, st_v],
        out_specs=[seq(w), st_c, st_v, st_v],
        out_shape=[jax.ShapeDtypeStruct((b, s, w), BF16),
                   jax.ShapeDtypeStruct((b, ML_H, ML_DK, ML_DV), F32),
                   jax.ShapeDtypeStruct((b, ML_H, 1, LANES), F32),
                   jax.ShapeDtypeStruct((b, ML_H, 1, LANES), F32)],
        compiler_params=pltpu.CompilerParams(dimension_semantics=("arbitrary", "arbitrary"),
                                             vmem_limit_bytes=VMEM_LIMIT),
        name="mlstm",
    )(q, k, v, g, c0, n0, m0)


def _attn_kernel(qi_ref, ki_ref, last_ref, q_ref, k_ref, vt_ref, o_ref, m_sc, l_sc, acc_sc):
    step = pl.program_id(2)
    qi = qi_ref[step]
    ki = ki_ref[step]
    tq = q_ref.shape[2]
    tk = k_ref.shape[2]

    @pl.when(ki == 0)
    def _():
        m_sc[...] = jnp.full(m_sc.shape, -jnp.inf, F32)
        l_sc[...] = jnp.zeros(l_sc.shape, F32)
        acc_sc[...] = jnp.zeros(acc_sc.shape, F32)

    def tile(masked):
        chains = [(hh, c0) for hh in range(2) for c0 in range(0, tk, ATT_KC)]
        scores = {}
        for hh, c0 in chains:
            st = _nt(k_ref[0, hh, c0:c0 + ATT_KC, :], q_ref[0, hh])
            if masked:
                keys = ki * tk + c0 + lax.broadcasted_iota(jnp.int32, (ATT_KC, tq), 0)
                qrys = qi * tq + lax.broadcasted_iota(jnp.int32, (ATT_KC, tq), 1)
                st = jnp.where(keys <= qrys, st, -jnp.inf)
            scores[hh, c0] = st
        parts = ([], [])
        ones_rows = jnp.ones((16, ATT_KC), BF16)
        for hh, c0 in chains:
            st = scores[hh, c0]
            m_g = jnp.max(st, axis=0, keepdims=True)
            if masked:
                m_g = jnp.maximum(m_g, MASKED_MAX_FLOOR)
            pt = jnp.exp2(st - m_g).astype(BF16)
            rows = slice(hh * MLA_VD, (hh + 1) * MLA_VD)
            v_ones = jnp.concatenate([vt_ref[0, rows, c0:c0 + ATT_KC], ones_rows], axis=0)
            o_g = _mm(v_ones, pt)
            parts[hh].append((m_g, o_g[MLA_VD:MLA_VD + 1, :], o_g[:MLA_VD, :]))
        for hh in range(2):
            rows = slice(hh * MLA_VD, (hh + 1) * MLA_VD)
            m_prev = m_sc[hh]
            m_new = functools.reduce(jnp.maximum, [m_g for m_g, _, _ in parts[hh]], m_prev)
            alpha = jnp.exp2(m_prev - m_new)
            l_new = alpha * l_sc[hh]
            acc = alpha * acc_sc[rows, :]
            for m_g, l_g, o_g in parts[hh]:
                w_g = jnp.exp2(m_g - m_new)
                l_new = l_new + w_g * l_g
                acc = acc + w_g * o_g
            m_sc[hh] = m_new
            l_sc[hh] = l_new
            acc_sc[rows, :] = acc

    crosses_diagonal = (ki + 1) * tk - 1 > qi * tq
    pl.when(crosses_diagonal)(functools.partial(tile, True))
    pl.when(jnp.logical_not(crosses_diagonal))(functools.partial(tile, False))

    @pl.when(last_ref[step] == 1)
    def _():
        out_t = jnp.concatenate([acc_sc[0:MLA_VD, :] / l_sc[0], acc_sc[MLA_VD:, :] / l_sc[1]], axis=0)
        o_ref[0] = out_t.T.astype(o_ref.dtype)


def _attn_prompt(qh, kh, vt):
    b, nh, s, _ = qh.shape
    tq, tk = ATT_TQ, ATT_TK
    qi_l, ki_l, last_l = [], [], []
    for qi in range(s // tq):
        nk = ((qi + 1) * tq + tk - 1) // tk
        for ki in range(nk):
            qi_l.append(qi)
            ki_l.append(ki)
            last_l.append(int(ki == nk - 1))
    nsteps = len(qi_l)
    sched = [jnp.asarray(np.asarray(a, np.int32)) for a in (qi_l, ki_l, last_l)]
    grid_spec = pltpu.PrefetchScalarGridSpec(
        num_scalar_prefetch=3,
        grid=(b, nh // 2, nsteps),
        in_specs=[pl.BlockSpec((1, 2, tq, HEAD_PAD), lambda bi, hp, st, qi, ki, la: (bi, hp, qi[st], 0)),
                  pl.BlockSpec((1, 2, tk, HEAD_PAD), lambda bi, hp, st, qi, ki, la: (bi, hp, ki[st], 0)),
                  pl.BlockSpec((1, 2 * MLA_VD, tk), lambda bi, hp, st, qi, ki, la: (bi, hp, ki[st]))],
        out_specs=pl.BlockSpec((1, tq, LANES), lambda bi, hp, st, qi, ki, la: (bi, qi[st], hp)),
        scratch_shapes=[pltpu.VMEM((2, 1, tq), F32), pltpu.VMEM((2, 1, tq), F32),
                        pltpu.VMEM((2 * MLA_VD, tq), F32)],
    )
    return pl.pallas_call(
        _attn_kernel,
        grid_spec=grid_spec,
        out_shape=jax.ShapeDtypeStruct((b, s, nh * MLA_VD), BF16),
        compiler_params=pltpu.CompilerParams(
            dimension_semantics=("arbitrary", "arbitrary", "arbitrary"),
            vmem_limit_bytes=VMEM_LIMIT),
        name="attn_prompt",
    )(*sched, qh, kh, vt)


def _dec_page_copies(pt_ref, pool_ckv, pool_kr, cbuf, kbuf, sems, step, slot):
    copies = []
    for i in range(DEC_PG):
        page = pt_ref[step * DEC_PG + i]
        keys = pl.ds(i * PAGE_SIZE, PAGE_SIZE)
        copies.append(pltpu.make_async_copy(pool_ckv.at[page], cbuf.at[slot, keys, :], sems.at[slot, 0]))
        copies.append(pltpu.make_async_copy(pool_kr.at[page], kbuf.at[slot, :, keys], sems.at[slot, 1]))
    return copies


def _dec_kernel(pt_ref, qlat_ref, qh_ref, ckvn_ref, krn_ref, pool_ckv, pool_kr,
                o_ref, m_sc, l_sc, acc_sc, padc_sc, padk_sc, cbuf, kbuf, sems):
    j = pl.program_id(1)
    nj = pl.num_programs(1)
    step = pl.program_id(0) * nj + j
    last_step = pl.num_programs(0) * nj - 1
    slot = step % 2
    qlat = qlat_ref[0]
    qh = qh_ref[0]
    ds = ckvn_ref.shape[1]
    copies = functools.partial(_dec_page_copies, pt_ref, pool_ckv, pool_kr, cbuf, kbuf, sems)

    @pl.when(step == 0)
    def _():
        for c in copies(0, 0):
            c.start()

    for c in copies(step + 1, 1 - slot):
        c.start()
    for c in copies(step, slot):
        c.wait()

    @pl.when(j == 0)
    def _():
        padc_sc[...] = jnp.zeros(padc_sc.shape, BF16)
        padk_sc[...] = jnp.zeros(padk_sc.shape, BF16)
        padc_sc[0:ds, :] = ckvn_ref[0].astype(BF16)
        padk_sc[0:ds, :] = krn_ref[0]
        cn = padc_sc[...]
        s = _nt(qlat, cn) + _nt(qh, padk_sc[...])
        tok = lax.broadcasted_iota(jnp.int32, s.shape, 0) // MLA_H
        key = lax.broadcasted_iota(jnp.int32, s.shape, 1)
        s = jnp.where(key <= tok, s, -jnp.inf)
        m = jnp.max(s, axis=-1, keepdims=True)
        p = jnp.exp2(s - m)
        m_sc[...] = m
        l_sc[...] = jnp.sum(p, axis=-1, keepdims=True)
        acc_sc[...] = _mm(p.astype(BF16), cn)

    qr = qh[:, MLA_NOPE:MLA_NOPE + MLA_ROPE]
    pages = cbuf[slot].astype(BF16)
    krt = kbuf[slot].astype(BF16)
    gk = pages.shape[0] // DEC_GROUPS
    groups = [pages[g * gk:(g + 1) * gk] for g in range(DEC_GROUPS)]
    scores, probs, parts = {}, {}, []

    def score(g):
        scores[g] = _nt(qlat, groups[g]) + _mm(qr, krt[:, g * gk:(g + 1) * gk])

    def soft(g):
        m_g = jnp.max(scores[g], axis=-1, keepdims=True)
        p = jnp.exp2(scores[g] - m_g)
        probs[g] = (m_g, jnp.sum(p, axis=-1, keepdims=True), p.astype(BF16))

    def value(g):
        m_g, l_g, pb = probs[g]
        parts.append((m_g, l_g, _mm(pb, groups[g])))

    for g in range(DEC_GROUPS):
        score(g)
    soft(0)
    for g in range(DEC_GROUPS):
        if g + 1 < DEC_GROUPS:
            soft(g + 1)
        value(g)
    m_prev = m_sc[...]
    m_new = functools.reduce(jnp.maximum, [m_g for m_g, _, _ in parts], m_prev)
    alpha = jnp.exp2(m_prev - m_new)
    l_new = alpha * l_sc[...]
    acc = alpha * acc_sc[...]
    for m_g, l_g, o_g in parts:
        w_g = jnp.exp2(m_g - m_new)
        l_new = l_new + w_g * l_g
        acc = acc + w_g * o_g
    m_sc[...] = m_new
    l_sc[...] = l_new
    acc_sc[...] = acc

    @pl.when(j == nj - 1)
    def _():
        o_ref[0] = acc_sc[...] / l_sc[...]

    @pl.when(step == last_step)
    def _():
        for c in copies(step + 1, 1 - slot):
            c.wait()


def _attn_sample(qlat, qh, ckvn, krn, pool_ckv, pool_kr, page_table):
    b, nq, _ = qlat.shape
    ds = ckvn.shape[1]
    n_pages = page_table.shape[1]
    pg = DEC_PG
    pt = page_table.reshape(-1)
    pt = jnp.concatenate([pt, pt[:pg]])

    fixed = lambda w, r: pl.BlockSpec((1, r, w), lambda bi, j, pt_ref: (bi, 0, 0))
    grid_spec = pltpu.PrefetchScalarGridSpec(
        num_scalar_prefetch=1,
        grid=(b, n_pages // pg),
        in_specs=[fixed(KV_LORA, nq), fixed(HEAD_PAD, nq), fixed(KV_LORA, ds), fixed(HEAD_PAD, ds),
                  pl.BlockSpec(memory_space=pl.ANY), pl.BlockSpec(memory_space=pl.ANY)],
        out_specs=fixed(KV_LORA, nq),
        scratch_shapes=[pltpu.VMEM((nq, 1), F32), pltpu.VMEM((nq, 1), F32),
                        pltpu.VMEM((nq, KV_LORA), F32),
                        pltpu.VMEM((PAGE_SIZE, KV_LORA), BF16), pltpu.VMEM((PAGE_SIZE, HEAD_PAD), BF16),
                        pltpu.VMEM((2, pg * PAGE_SIZE, KV_LORA), F32),
                        pltpu.VMEM((2, MLA_ROPE, pg * PAGE_SIZE), F32),
                        pltpu.SemaphoreType.DMA((2, 2))],
    )
    return pl.pallas_call(
        _dec_kernel,
        grid_spec=grid_spec,
        out_shape=jax.ShapeDtypeStruct((b, nq, KV_LORA), F32),
        compiler_params=pltpu.CompilerParams(dimension_semantics=("arbitrary", "arbitrary"),
                                             vmem_limit_bytes=VMEM_LIMIT),
        name="attn_sample",
    )(pt, qlat, qh, ckvn, krn, pool_ckv, pool_kr)


def _back_kernel(absorb, x_ref, gate_ref, hml_ref, mo_ref, mz_ref, o_ref, az_ref, gn_ref, wuv_ref,
                 wout_ref, lng_ref, lnb_ref, y_ref):
    tb, ts, d = x_ref.shape
    tm = tb * ts
    hm = hml_ref[...].astype(F32) * jax.nn.sigmoid(mo_ref[...].astype(F32))
    parts = []
    for hh in range(ML_H):
        t = hm[:, hh * ML_DV:(hh + 1) * ML_DV]
        mu = jnp.mean(t, axis=-1, keepdims=True)
        tc = t - mu
        var = jnp.mean(tc * tc, axis=-1, keepdims=True)
        parts.append(tc * lax.rsqrt(var + EPS))
    mz = mz_ref[...].astype(F32)
    y_ml = jnp.concatenate(parts, axis=1) * gn_ref[...] * (mz * jax.nn.sigmoid(mz))
    if absorb:
        olat = o_ref[...].astype(BF16)
        o_mla = jnp.concatenate(
            [_mm(olat[:, p * 2 * KV_LORA:(p + 1) * 2 * KV_LORA], wuv_ref[p]) for p in range(MLA_H // 2)],
            axis=1)
    else:
        o_mla = o_ref[...].astype(F32)
    az = az_ref[...].astype(F32)
    y_mla = o_mla * (az * jax.nn.sigmoid(az))
    out = _mm(y_ml.astype(BF16), wout_ref[0:ML_W, :]) + _mm(y_mla.astype(BF16), wout_ref[ML_W:, :])
    z = ALPHA * x_ref[...] + gate_ref[...] * out.reshape(tb, ts, d)
    mu = jnp.mean(z, axis=-1, keepdims=True)
    zc = z - mu
    var = jnp.mean(zc * zc, axis=-1, keepdims=True)
    y_ref[...] = zc * lax.rsqrt(var + EPS) * lng_ref[...] + lnb_ref[...]


def _back(absorb, x, mod, hml, mo, mz, o, az, wts):
    b, s, d = x.shape
    n = b * s
    tm = FRONT_TM
    if absorb:
        tb, ts = tm // s, s
        x_map = lambda i: (i, 0, 0)
        gate_map = lambda i: (i, 0, 2)
    else:
        tb, ts = 1, tm
        spb = s // tm
        x_map = lambda i: (i // spb, i % spb, 0)
        gate_map = lambda i: (i // spb, 0, 2)
    const2 = lambda i: (0, 0)
    tok = lambda w: pl.BlockSpec((tm, w), lambda i: (i, 0))
    ow = o.shape[1]
    return pl.pallas_call(
        functools.partial(_back_kernel, absorb),
        grid=(n // tm,),
        in_specs=[pl.BlockSpec((tb, ts, d), x_map),
                  pl.BlockSpec((tb, 1, d), gate_map),
                  tok(ML_W), tok(ML_W), tok(ML_W), tok(ow), tok(MLA_W),
                  pl.BlockSpec((1, ML_W), const2),
                  pl.BlockSpec((MLA_H // 2, 2 * KV_LORA, LANES), lambda i: (0, 0, 0)),
                  pl.BlockSpec((d, d), const2),
                  pl.BlockSpec((1, d), const2),
                  pl.BlockSpec((1, d), const2)],
        out_specs=pl.BlockSpec((tb, ts, d), x_map),
        out_shape=jax.ShapeDtypeStruct((b, s, d), F32),
        compiler_params=pltpu.CompilerParams(dimension_semantics=("arbitrary",),
                                             vmem_limit_bytes=VMEM_LIMIT),
        name="back_sample" if absorb else "back_prompt",
    )(x, mod, hml, mo, mz, o, az, wts["gn"], wts["wuv_pair"], wts["w_out"], wts["ln_g"], wts["ln_b"])


def _prep_weights(l, w_in, ml_b_i, ml_b_f, ml_gn, mla_q_norm, mla_kv_norm, mla_w_uq, mla_w_uk,
                  mla_w_uv, w_out, ln_g, ln_b):
    offs = np.cumsum((0, 512, 512, 512, ML_H, ML_H, 512, 512, Q_LORA, KV_LORA, MLA_ROPE, 512))
    names = ("mq", "mk", "mv", "mi", "mf", "mo", "mz", "cq", "ckv", "kr", "az")
    w = {nm: w_in[l][:, offs[i]:offs[i + 1]] for i, nm in enumerate(names)}
    d = w_in.shape[1]
    z = lambda c: jnp.zeros((d, c), F32)
    kr1, kr2 = w["kr"][:, :ROPE_HALF], w["kr"][:, ROPE_HALF:]
    tail = HEAD_PAD - MLA_NOPE - MLA_ROPE
    w_in_p = jnp.concatenate(
        [w["mq"], w["mk"], w["mv"], w["mi"], w["mf"], z(LANES - 2 * ML_H), w["mo"], w["mz"], w["cq"], w["ckv"],
         z(MLA_NOPE), kr1, kr2, z(tail), z(MLA_NOPE), -kr2, kr1, z(tail), w["az"]], axis=1).astype(BF16)
    b_gate = jnp.concatenate([ml_b_i[l], ml_b_f[l], jnp.zeros((LANES - 2 * ML_H,), F32)]).reshape(1, LANES)
    uq = mla_w_uq[l].reshape(Q_LORA, MLA_H, MLA_NOPE + MLA_ROPE)
    nope, r1, r2 = uq[..., :MLA_NOPE], uq[..., MLA_NOPE:MLA_NOPE + ROPE_HALF], uq[..., MLA_NOPE + ROPE_HALF:]
    zq = lambda c: jnp.zeros((Q_LORA, MLA_H, c), F32)
    wqa = jnp.concatenate([nope, r1, r2, zq(tail)], axis=-1).reshape(Q_LORA, MLA_H * HEAD_PAD).astype(BF16)
    wqr = jnp.concatenate([zq(MLA_NOPE), -r2, r1, zq(tail)], axis=-1).reshape(Q_LORA, MLA_H * HEAD_PAD).astype(BF16)
    uk = mla_w_uk[l].reshape(KV_LORA, MLA_H, MLA_NOPE)
    wk_pad = jnp.concatenate([uk, jnp.zeros((KV_LORA, MLA_H, HEAD_PAD - MLA_NOPE), F32)],
                             axis=-1).reshape(KV_LORA, MLA_H * HEAD_PAD).astype(BF16)
    wukt = jnp.concatenate([jnp.transpose(uk, (1, 2, 0)),
                            jnp.zeros((MLA_H, HEAD_PAD - MLA_NOPE, KV_LORA), F32)], axis=1).astype(BF16)
    uv = mla_w_uv[l].reshape(KV_LORA, MLA_H // 2, 2, MLA_VD)
    zv = jnp.zeros((KV_LORA, MLA_H // 2, MLA_VD), F32)
    wuv_pair = jnp.concatenate(
        [jnp.concatenate([uv[:, :, 0], zv], axis=-1), jnp.concatenate([zv, uv[:, :, 1]], axis=-1)],
        axis=0)
    wuv_pair = jnp.transpose(wuv_pair, (1, 0, 2)).astype(BF16)
    return dict(w_in=w_in_p, b_gate=b_gate, q_norm=mla_q_norm[l].reshape(1, -1),
                kv_norm=mla_kv_norm[l].reshape(1, -1), wqa=wqa, wqr=wqr, wk_pad=wk_pad, wukt=wukt,
                w_uv_t=mla_w_uv[l].T.astype(BF16), wuv_pair=wuv_pair, gn=ml_gn[l].reshape(1, -1),
                w_out=w_out[l].astype(BF16), ln_g=ln_g[l].reshape(1, -1), ln_b=ln_b[l].reshape(1, -1))


def _rope_tables(pos):
    f32 = np.float32
    inv = ROPE_THETA ** (-np.arange(ROPE_HALF, dtype=np.float64) / ROPE_HALF)
    ang = pos.astype(np.float64)[:, None] * inv[None, :]
    n = pos.shape[0]
    tail = HEAD_PAD - MLA_NOPE - MLA_ROPE
    cos, sin = np.cos(ang).astype(f32), np.sin(ang).astype(f32)
    cm = np.concatenate([np.ones((n, MLA_NOPE), f32), cos, cos, np.zeros((n, tail), f32)], 1)
    sm = np.concatenate([np.zeros((n, MLA_NOPE), f32), sin, sin, np.zeros((n, tail), f32)], 1)
    return cm, sm


def kernel(x_prompt, x_sample, c_prompt, c_sample, cache_ckv, cache_krope, state_C, state_n, state_m,
           page_table, w_ada, b_ada, w_in, ml_b_i, ml_b_f, ml_gn, mla_q_norm, mla_kv_norm,
           mla_w_uq, mla_w_uk, mla_w_uv, w_out, ln_g, ln_b):
    bp, sp, d = x_prompt.shape
    bs, ss, _ = x_sample.shape
    past = page_table.shape[1] * PAGE_SIZE
    cm_p, sm_p = (jnp.asarray(t) for t in _rope_tables(np.arange(sp)))
    reps = FRONT_TM // ss
    cm_s, sm_s = (jnp.asarray(np.tile(t, (reps, 1))) for t in _rope_tables(past + np.arange(ss)))
    pad = (-(bp + bs)) % 8
    c_all = jnp.concatenate([c_prompt, c_sample, jnp.zeros((pad, d), F32)], axis=0)

    xp, xs = x_prompt, x_sample
    outs = [[] for _ in range(10)]
    for l in range(DEPTH):
        wts = _prep_weights(l, w_in, ml_b_i, ml_b_f, ml_gn, mla_q_norm, mla_kv_norm, mla_w_uq, mla_w_uk,
                            mla_w_uv, w_out, ln_g, ln_b)
        mod = _ada(c_all, w_ada[l], b_ada[l])
        mod_p = mod[:bp].reshape(bp, 1, 3 * d)
        mod_s = mod[bp:bp + bs].reshape(bs, 1, 3 * d)

        mq, mk, mv, g, mo, mz, az, ckv, kr, qh, kh, vt = _front(False, xp, mod_p, cm_p, sm_p, wts)
        seq3 = lambda t: t.reshape(bp, sp, -1)
        c0 = jnp.zeros((bp, ML_H, ML_DK, ML_DV), F32)
        n0 = jnp.zeros((bp, ML_H, 1, ML_DK), F32)
        m0 = jnp.full((bp, ML_H, 1, LANES), -jnp.inf, F32)
        hml, c_p, n_p, m_p = _mlstm(seq3(mq), seq3(mk), seq3(mv), seq3(g), c0, n0, m0, ML_CHUNK, bp)
        o_p = _attn_prompt(qh, kh, vt)
        yp = _back(False, xp, mod_p, hml.reshape(bp * sp, -1), mo, mz, o_p.reshape(bp * sp, -1), az, wts)
        outs[0].append(ckv.reshape(bp, sp, KV_LORA))
        outs[1].append(kr.reshape(bp, sp, MLA_ROPE))
        outs[2].append(c_p)
        outs[3].append(n_p[:, :, 0, :])
        outs[4].append(m_p[:, :, 0, 0])

        mq, mk, mv, g, mo, mz, az, ckv, kr, qlat, qh, krp = _front(True, xs, mod_s, cm_s, sm_s, wts)
        seq3 = lambda t: t.reshape(bs, ss, -1)
        n0 = state_n[l].astype(F32).reshape(bs, ML_H, 1, ML_DK)
        m0 = jnp.broadcast_to(state_m[l].astype(F32)[:, :, None, None], (bs, ML_H, 1, LANES))
        hml, c_s, n_s, m_s = _mlstm(seq3(mq), seq3(mk), seq3(mv), seq3(g), state_C[l].astype(F32), n0, m0, ss, 8)
        o_s = _attn_sample(qlat.reshape(bs, ss * MLA_H, KV_LORA), qh.reshape(bs, ss * MLA_H, HEAD_PAD),
                           seq3(ckv), seq3(krp), cache_ckv[l], jnp.swapaxes(cache_krope[l], 1, 2), page_table)
        ys = _back(True, xs, mod_s, hml.reshape(bs * ss, -1), mo, mz,
                   o_s.reshape(bs * ss, MLA_H * KV_LORA), az, wts)
        outs[5].append(ckv.reshape(bs, ss, KV_LORA).astype(cache_ckv.dtype))
        outs[6].append(kr.reshape(bs, ss, MLA_ROPE).astype(cache_krope.dtype))
        outs[7].append(c_s.astype(state_C.dtype))
        outs[8].append(n_s[:, :, 0, :].astype(state_n.dtype))
        outs[9].append(m_s[:, :, 0, 0].astype(state_m.dtype))
        xp, xs = yp, ys
    return (xp, xs) + tuple(jnp.stack(o) for o in outs)
```

```python
import functools

import numpy as np
import jax
import jax.numpy as jnp
from jax import lax
from jax.experimental import pallas as pl
from jax.experimental.pallas import tpu as pltpu

F32 = jnp.float32
BF16 = jnp.bfloat16

D_MODEL = 1024
DEPTH = 1
PAGE_SIZE = 128
ML_W = 512
MLA_W = 512
ML_DK = 128
ML_DV = 128
ML_H = 4
MLA_VD = 64
MLA_H = 8
MLA_NOPE = 64
MLA_ROPE = 32
ROPE_HALF = MLA_ROPE // 2
MLA_SCALE = (MLA_NOPE + MLA_ROPE) ** -0.5
Q_SCALE = MLA_SCALE * float(np.log2(np.e))
Q_LORA = 384
KV_LORA = 256
ROPE_THETA = 10000.0
ML_CHUNK = 128
EPS = 1e-6
ALPHA = (2.0 * DEPTH) ** 0.25

LANES = 128
HEAD_PAD = 128
VMEM_LIMIT = 48 * 1024 * 1024

_SEG = {}
_off = 0
for _name, _w in (("mq", 512), ("mk", 512), ("mv", 512), ("gate", 128), ("mo", 512), ("mz", 512),
                  ("cq", Q_LORA), ("ckv", KV_LORA), ("kra", 128), ("krr", 128), ("az", 512)):
    _SEG[_name] = (_off, _off + _w)
    _off += _w
N_IN_PAD = _off

FRONT_TM = 256
ATT_TQ = 512
ATT_TK = 1024
ATT_KC = 256
MASKED_MAX_FLOOR = -1e30
DEC_PG = 64
DEC_GROUPS = 8


def _nt(a, b):
    return lax.dot_general(a, b, (((1,), (1,)), ((), ())), preferred_element_type=F32)


def _tn(a, b):
    return lax.dot_general(a, b, (((0,), (0,)), ((), ())), preferred_element_type=F32)


def _mm(a, b):
    return jnp.dot(a, b, preferred_element_type=F32)


def _mm_exact(a, b):
    return jnp.dot(a, b, preferred_element_type=F32, precision=lax.Precision.HIGHEST)


def _nt_exact(a, b):
    return lax.dot_general(a, b, (((1,), (1,)), ((), ())), preferred_element_type=F32,
                           precision=lax.Precision.HIGHEST)


def _ada_kernel(c_ref, w_ref, b_ref, o_ref):
    o_ref[...] = _mm(c_ref[...].astype(BF16), w_ref[...].astype(BF16)) + b_ref[...]


def _ada(c_all, w_ada, b_ada):
    m = c_all.shape[0]
    tn = 512
    return pl.pallas_call(
        _ada_kernel,
        grid=(3 * D_MODEL // tn,),
        in_specs=[pl.BlockSpec((m, D_MODEL), lambda j: (0, 0)),
                  pl.BlockSpec((D_MODEL, tn), lambda j: (0, j)),
                  pl.BlockSpec((1, tn), lambda j: (0, j))],
        out_specs=pl.BlockSpec((m, tn), lambda j: (0, j)),
        out_shape=jax.ShapeDtypeStruct((m, 3 * D_MODEL), F32),
        name="ada",
    )(c_all, w_ada, b_ada.reshape(1, -1))


def _rms(x, g):
    return x * lax.rsqrt(jnp.mean(x * x, axis=-1, keepdims=True) + EPS) * g


def _log_sigmoid(x):
    return jnp.minimum(x, 0.0) - jnp.log1p(jnp.exp(-jnp.abs(x)))


def _front_kernel(absorb, x_ref, sh_ref, sc_ref, cm_ref, sm_ref, win_ref, bg_ref, qn_ref, kvn_ref,
                  wqa_ref, wqr_ref, wk_ref, wv_ref,
                  mq_ref, mk_ref, mv_ref, g_ref, mo_ref, mz_ref, az_ref, ckv_ref, kr_ref,
                  o1_ref, o2_ref, o3_ref):
    tb, ts, d = x_ref.shape
    tm = tb * ts
    h = x_ref[...] * (1.0 + sc_ref[...]) + sh_ref[...]
    h = h.reshape(tm, d).astype(BF16)

    def seg(name):
        lo, hi = _SEG[name]
        return _mm(h, win_ref[:, lo:hi])

    mq_ref[...] = seg("mq").astype(BF16)
    mk_ref[...] = (seg("mk") * (ML_DK ** -0.5)).astype(BF16)
    mv_ref[...] = seg("mv").astype(BF16)
    gz = seg("gate") + bg_ref[...]
    lane = lax.broadcasted_iota(jnp.int32, gz.shape, 1)
    g_ref[...] = jnp.where(lane < ML_H, gz, jnp.where(lane < 2 * ML_H, _log_sigmoid(gz), 0.0))
    mo_ref[...] = seg("mo").astype(BF16)
    mz_ref[...] = seg("mz").astype(BF16)
    az_ref[...] = seg("az").astype(BF16)

    cm = cm_ref[...]
    sm = sm_ref[...]
    ckvn = _rms(seg("ckv"), kvn_ref[...])
    ckv_ref[...] = ckvn
    ckvn_b = ckvn.astype(BF16)
    krp = seg("kra") * cm + seg("krr") * sm
    kr_ref[...] = krp[:, MLA_NOPE:MLA_NOPE + MLA_ROPE]

    cqn = _rms(seg("cq"), qn_ref[...]).astype(BF16)
    qa = _mm(cqn, wqa_ref[...])
    qr = _mm(cqn, wqr_ref[...])
    for hh in range(MLA_H):
        sl = slice(hh * HEAD_PAD, (hh + 1) * HEAD_PAD)
        qh = ((qa[:, sl] * cm + qr[:, sl] * sm) * Q_SCALE).astype(BF16)
        if absorb:
            o1_ref[:, hh * KV_LORA:(hh + 1) * KV_LORA] = _mm(qh, wk_ref[hh]).astype(BF16)
            o2_ref[:, sl] = qh
        else:
            o1_ref[0, hh] = qh
            o2_ref[0, hh] = (_mm(ckvn_b, wk_ref[:, sl]) + krp).astype(BF16)
    if absorb:
        o3_ref[...] = krp.astype(BF16)
    else:
        o3_ref[0] = _nt(wv_ref[...], ckvn_b).astype(BF16)


def _front(absorb, x, mod, cm, sm, wts):
    b, s, d = x.shape
    n = b * s
    tm = FRONT_TM
    if absorb:
        tb, ts = tm // s, s
        x_map = lambda i: (i, 0, 0)
        mod_map = lambda k: (lambda i: (i, 0, k))
        tab_map = lambda i: (0, 0)
    else:
        tb, ts = 1, tm
        spb = s // tm
        x_map = lambda i: (i // spb, i % spb, 0)
        mod_map = lambda k: (lambda i: (i // spb, 0, k))
        tab_map = lambda i: (i % spb, 0)
    const2 = lambda i: (0, 0)
    tok = lambda w: pl.BlockSpec((tm, w), lambda i: (i, 0))
    in_specs = [
        pl.BlockSpec((tb, ts, d), x_map),
        pl.BlockSpec((tb, 1, d), mod_map(0)),
        pl.BlockSpec((tb, 1, d), mod_map(1)),
        pl.BlockSpec((tm, LANES), tab_map),
        pl.BlockSpec((tm, LANES), tab_map),
        pl.BlockSpec((d, N_IN_PAD), const2),
        pl.BlockSpec((1, LANES), const2),
        pl.BlockSpec((1, Q_LORA), const2),
        pl.BlockSpec((1, KV_LORA), const2),
        pl.BlockSpec((Q_LORA, MLA_H * HEAD_PAD), const2),
        pl.BlockSpec((Q_LORA, MLA_H * HEAD_PAD), const2),
    ]
    out_specs = [tok(512), tok(512), tok(512), tok(LANES), tok(512), tok(512), tok(512),
                 tok(KV_LORA), tok(MLA_ROPE)]
    out_shape = [jax.ShapeDtypeStruct((n, w), dt)
                 for w, dt in ((512, BF16), (512, BF16), (512, BF16), (LANES, F32), (512, BF16), (512, BF16),
                               (512, BF16), (KV_LORA, F32), (MLA_ROPE, F32))]
    if absorb:
        in_specs += [pl.BlockSpec((MLA_H, HEAD_PAD, KV_LORA), lambda i: (0, 0, 0)),
                     pl.BlockSpec((MLA_W, KV_LORA), const2)]
        out_specs += [tok(MLA_H * KV_LORA), tok(MLA_H * HEAD_PAD), tok(HEAD_PAD)]
        out_shape += [jax.ShapeDtypeStruct((n, MLA_H * KV_LORA), BF16),
                      jax.ShapeDtypeStruct((n, MLA_H * HEAD_PAD), BF16),
                      jax.ShapeDtypeStruct((n, HEAD_PAD), BF16)]
        wk = wts["wukt"]
    else:
        head_map = lambda i: (i // spb, 0, i % spb, 0)
        in_specs += [pl.BlockSpec((KV_LORA, MLA_H * HEAD_PAD), const2),
                     pl.BlockSpec((MLA_W, KV_LORA), const2)]
        out_specs += [pl.BlockSpec((1, MLA_H, tm, HEAD_PAD), head_map),
                      pl.BlockSpec((1, MLA_H, tm, HEAD_PAD), head_map),
                      pl.BlockSpec((1, MLA_W, tm), lambda i: (i // spb, 0, i % spb))]
        out_shape += [jax.ShapeDtypeStruct((b, MLA_H, s, HEAD_PAD), BF16),
                      jax.ShapeDtypeStruct((b, MLA_H, s, HEAD_PAD), BF16),
                      jax.ShapeDtypeStruct((b, MLA_W, s), BF16)]
        wk = wts["wk_pad"]
    return pl.pallas_call(
        functools.partial(_front_kernel, absorb),
        grid=(n // tm,),
        in_specs=in_specs,
        out_specs=out_specs,
        out_shape=out_shape,
        compiler_params=pltpu.CompilerParams(dimension_semantics=("arbitrary",),
                                             vmem_limit_bytes=VMEM_LIMIT),
        name="front_sample" if absorb else "front_prompt",
    )(x, mod, mod, cm, sm, wts["w_in"], wts["b_gate"], wts["q_norm"], wts["kv_norm"],
      wts["wqa"], wts["wqr"], wk, wts["w_uv_t"])


def _mlstm_kernel(q_ref, k_ref, v_ref, g_ref, c0_ref, n0_ref, m0_ref,
                  h_ref, c_ref, n_ref, m_ref):
    tb, L, _ = q_ref.shape
    ci = pl.program_id(1)

    @pl.when(ci == 0)
    def _():
        c_ref[...] = c0_ref[...]
        n_ref[...] = n0_ref[...]
        m_ref[...] = m0_ref[...]

    row = lax.broadcasted_iota(jnp.int32, (L, L), 0)
    col = lax.broadcasted_iota(jnp.int32, (L, L), 1)
    causal = col <= row
    tril = causal.astype(F32)
    sel = (lax.broadcasted_iota(jnp.int32, (8, LANES), 0)
           == lax.broadcasted_iota(jnp.int32, (8, LANES), 1)).astype(F32)

    chains = [(t, hh) for t in range(tb) for hh in range(ML_H)]
    sl = lambda hh: slice(hh * ML_DK, (hh + 1) * ML_DK)
    gates = {}
    for t in range(tb):
        g = g_ref[t]
        fcum = _mm_exact(tril, g)
        g_rows = _nt_exact(sel, g)
        f_rows = _nt_exact(sel, fcum)
        gates[t] = (g, fcum, g_rows, f_rows)

    qk, qc, c_prevs = {}, {}, {}
    for t, hh in chains:
        qb = q_ref[t, :, sl(hh)]
        c_prevs[t, hh] = c_ref[t, hh]
        qk[t, hh] = _nt(qb, k_ref[t, :, sl(hh)])
        qc[t, hh] = _mm(qb, c_prevs[t, hh].astype(BF16))

    stab = {}
    for t, hh in chains:
        g, fcum, g_rows, f_rows = gates[t]
        m_prev = m_ref[t, hh][:, :1]
        f_col = fcum[:, ML_H + hh:ML_H + hh + 1]
        f_row = f_rows[ML_H + hh:ML_H + hh + 1, :]
        ig_row = g_rows[hh:hh + 1, :]
        dmat = jnp.where(causal, f_col - f_row + ig_row, -jnp.inf)
        m_inter = jnp.broadcast_to(f_col, (L, LANES)) + m_prev
        m_t = jnp.maximum(m_inter, jnp.max(dmat, axis=-1, keepdims=True))
        stab[t, hh] = (jnp.exp(dmat - m_t[:, :L]), jnp.exp(m_inter - m_t), m_t)

    svs = {}
    ones_cols = jnp.ones((L, LANES), BF16)
    for t, hh in chains:
        sq = (qk[t, hh] * stab[t, hh][0]).astype(BF16)
        svs[t, hh] = _mm(sq, jnp.concatenate([v_ref[t, :, sl(hh)], ones_cols], axis=1))

    for t, hh in chains:
        _, a, m_t = stab[t, hh]
        n_rep = jnp.broadcast_to(n_ref[t, hh], (LANES, ML_DK)).astype(BF16)
        qn = _nt(q_ref[t, :, sl(hh)], n_rep)
        num = svs[t, hh][:, :ML_DV] + a * qc[t, hh]
        den = svs[t, hh][:, ML_DV:] + a * qn
        h_ref[t, :, sl(hh)] = (num / jnp.maximum(jnp.abs(den), jnp.exp(-m_t))).astype(h_ref.dtype)

    for t, hh in chains:
        w, a, m_t = stab[t, hh]
        wl_row = w[L - 1:L, :]
        al = a[L - 1:L, :]
        kb = k_ref[t, :, sl(hh)]
        kw_t = (kb.astype(F32).T * wl_row).astype(BF16)
        c_ref[t, hh] = al[:, :1] * c_prevs[t, hh] + _mm(kw_t, v_ref[t, :, sl(hh)])
        n_ref[t, hh] = al * n_ref[t, hh] + _mm(w[L - 8:L, :].astype(BF16), kb)[7:8, :]
        m_ref[t, hh] = m_t[L - 1:L, :]


def _mlstm(q, k, v, g, c0, n0, m0, chunk, tb):
    b, s, w = q.shape
    nc = s // chunk
    seq = lambda ww: pl.BlockSpec((tb, chunk, ww), lambda i, c: (i, c, 0))
    st_c = pl.BlockSpec((tb, ML_H, ML_DK, ML_DV), lambda i, c: (i, 0, 0, 0))
    st_v = pl.BlockSpec((tb, ML_H, 1, LANES), lambda i, c: (i, 0, 0, 0))
    return pl.pallas_call(
        _mlstm_kernel,
        grid=(b // tb, nc),
        in_specs=[seq(w), seq(w), seq(w), seq(LANES), st_c, st_v, st_v],
        out_specs=[seq(w), st_c, st_v, st_v],
        out_shape=[jax.ShapeDtypeStruct((b, s, w), BF16),
                   jax.ShapeDtypeStruct((b, ML_H, ML_DK, ML_DV), F32),
                   jax.ShapeDtypeStruct((b, ML_H, 1, LANES), F32),
                   jax.ShapeDtypeStruct((b, ML_H, 1, LANES), F32)],
        compiler_params=pltpu.CompilerParams(dimension_semantics=("arbitrary", "arbitrary"),
                                             vmem_limit_bytes=VMEM_LIMIT),
        name="mlstm",
    )(q, k, v, g, c0, n0, m0)


def _attn_kernel(qi_ref, ki_ref, last_ref, q_ref, k_ref, vt_ref, o_ref, m_sc, l_sc, acc_sc):
    step = pl.program_id(2)
    qi = qi_ref[step]
    ki = ki_ref[step]
    tq = q_ref.shape[2]
    tk = k_ref.shape[2]

    @pl.when(ki == 0)
    def _():
        m_sc[...] = jnp.full(m_sc.shape, -jnp.inf, F32)
        l_sc[...] = jnp.zeros(l_sc.shape, F32)
        acc_sc[...] = jnp.zeros(acc_sc.shape, F32)

    def tile(masked):
        chains = [(hh, c0) for hh in range(2) for c0 in range(0, tk, ATT_KC)]
        scores = {}
        for hh, c0 in chains:
            st = _nt(k_ref[0, hh, c0:c0 + ATT_KC, :], q_ref[0, hh])
            if masked:
                keys = ki * tk + c0 + lax.broadcasted_iota(jnp.int32, (ATT_KC, tq), 0)
                qrys = qi * tq + lax.broadcasted_iota(jnp.int32, (ATT_KC, tq), 1)
                st = jnp.where(keys <= qrys, st, -jnp.inf)
            scores[hh, c0] = st
        parts = ([], [])
        ones_rows = jnp.ones((16, ATT_KC), BF16)
        for hh, c0 in chains:
            st = scores[hh, c0]
            m_g = jnp.max(st, axis=0, keepdims=True)
            if masked:
                m_g = jnp.maximum(m_g, MASKED_MAX_FLOOR)
            pt = jnp.exp2(st - m_g).astype(BF16)
            rows = slice(hh * MLA_VD, (hh + 1) * MLA_VD)
            v_ones = jnp.concatenate([vt_ref[0, rows, c0:c0 + ATT_KC], ones_rows], axis=0)
            o_g = _mm(v_ones, pt)
            parts[hh].append((m_g, o_g[MLA_VD:MLA_VD + 1, :], o_g[:MLA_VD, :]))
        for hh in range(2):
            rows = slice(hh * MLA_VD, (hh + 1) * MLA_VD)
            m_prev = m_sc[hh]
            m_new = functools.reduce(jnp.maximum, [m_g for m_g, _, _ in parts[hh]], m_prev)
            alpha = jnp.exp2(m_prev - m_new)
            l_new = alpha * l_sc[hh]
            acc = alpha * acc_sc[rows, :]
            for m_g, l_g, o_g in parts[hh]:
                w_g = jnp.exp2(m_g - m_new)
                l_new = l_new + w_g * l_g
                acc = acc + w_g * o_g
            m_sc[hh] = m_new
            l_sc[hh] = l_new
            acc_sc[rows, :] = acc

    crosses_diagonal = (ki + 1) * tk - 1 > qi * tq
    pl.when(crosses_diagonal)(functools.partial(tile, True))
    pl.when(jnp.logical_not(crosses_diagonal))(functools.partial(tile, False))

    @pl.when(last_ref[step] == 1)
    def _():
        out_t = jnp.concatenate([acc_sc[0:MLA_VD, :] / l_sc[0], acc_sc[MLA_VD:, :] / l_sc[1]], axis=0)
        o_ref[0] = out_t.T.astype(o_ref.dtype)


def _attn_prompt(qh, kh, vt):
    b, nh, s, _ = qh.shape
    tq, tk = ATT_TQ, ATT_TK
    qi_l, ki_l, last_l = [], [], []
    for qi in range(s // tq):
        nk = ((qi + 1) * tq + tk - 1) // tk
        for ki in range(nk):
            qi_l.append(qi)
            ki_l.append(ki)
            last_l.append(int(ki == nk - 1))
    nsteps = len(qi_l)
    sched = [jnp.asarray(np.asarray(a, np.int32)) for a in (qi_l, ki_l, last_l)]
    grid_spec = pltpu.PrefetchScalarGridSpec(
        num_scalar_prefetch=3,
        grid=(b, nh // 2, nsteps),
        in_specs=[pl.BlockSpec((1, 2, tq, HEAD_PAD), lambda bi, hp, st, qi, ki, la: (bi, hp, qi[st], 0)),
                  pl.BlockSpec((1, 2, tk, HEAD_PAD), lambda bi, hp, st, qi, ki, la: (bi, hp, ki[st], 0)),
                  pl.BlockSpec((1, 2 * MLA_VD, tk), lambda bi, hp, st, qi, ki, la: (bi, hp, ki[st]))],
        out_specs=pl.BlockSpec((1, tq, LANES), lambda bi, hp, st, qi, ki, la: (bi, qi[st], hp)),
        scratch_shapes=[pltpu.VMEM((2, 1, tq), F32), pltpu.VMEM((2, 1, tq), F32),
                        pltpu.VMEM((2 * MLA_VD, tq), F32)],
    )
    return pl.pallas_call(
        _attn_kernel,
        grid_spec=grid_spec,
        out_shape=jax.ShapeDtypeStruct((b, s, nh * MLA_VD), BF16),
        compiler_params=pltpu.CompilerParams(
            dimension_semantics=("arbitrary", "arbitrary", "arbitrary"),
            vmem_limit_bytes=VMEM_LIMIT),
        name="attn_prompt",
    )(*sched, qh, kh, vt)


def _dec_page_copies(pt_ref, pool_ckv, pool_kr, cbuf, kbuf, sems, step, slot):
    copies = []
    for i in range(DEC_PG):
        page = pt_ref[step * DEC_PG + i]
        keys = pl.ds(i * PAGE_SIZE, PAGE_SIZE)
        copies.append(pltpu.make_async_copy(pool_ckv.at[page], cbuf.at[slot, keys, :], sems.at[slot, 0]))
        copies.append(pltpu.make_async_copy(pool_kr.at[page], kbuf.at[slot, :, keys], sems.at[slot, 1]))
    return copies


def _dec_kernel(pt_ref, qlat_ref, qh_ref, ckvn_ref, krn_ref, pool_ckv, pool_kr,
                o_ref, m_sc, l_sc, acc_sc, padc_sc, padk_sc, cbuf, kbuf, sems):
    j = pl.program_id(1)
    nj = pl.num_programs(1)
    step = pl.program_id(0) * nj + j
    last_step = pl.num_programs(0) * nj - 1
    slot = step % 2
    qlat = jnp.concatenate([qlat_ref[0, :, hh * KV_LORA:(hh + 1) * KV_LORA] for hh in range(MLA_H)], axis=0)
    qh = jnp.concatenate([qh_ref[0, :, hh * HEAD_PAD:(hh + 1) * HEAD_PAD] for hh in range(MLA_H)], axis=0)
    ds = ckvn_ref.shape[1]
    copies = functools.partial(_dec_page_copies, pt_ref, pool_ckv, pool_kr, cbuf, kbuf, sems)

    @pl.when(step == 0)
    def _():
        for c in copies(0, 0):
            c.start()

    for c in copies(step + 1, 1 - slot):
        c.start()
    for c in copies(step, slot):
        c.wait()

    @pl.when(j == 0)
    def _():
        padc_sc[...] = jnp.zeros(padc_sc.shape, BF16)
        padk_sc[...] = jnp.zeros(padk_sc.shape, BF16)
        padc_sc[0:ds, :] = ckvn_ref[0].astype(BF16)
        padk_sc[0:ds, :] = krn_ref[0]
        cn = padc_sc[...]
        s = _nt(qlat, cn) + _nt(qh, padk_sc[...])
        tok = lax.broadcasted_iota(jnp.int32, (MLA_H, ds, s.shape[1]), 1).reshape(s.shape)
        key = lax.broadcasted_iota(jnp.int32, s.shape, 1)
        s = jnp.where(key <= tok, s, -jnp.inf)
        m = jnp.max(s, axis=-1, keepdims=True)
        p = jnp.exp2(s - m)
        m_sc[...] = m
        l_sc[...] = jnp.sum(p, axis=-1, keepdims=True)
        acc_sc[...] = _mm(p.astype(BF16), cn)

    qr = qh[:, MLA_NOPE:MLA_NOPE + MLA_ROPE]
    pages = cbuf[slot].astype(BF16)
    krt = kbuf[slot].astype(BF16)
    gk = pages.shape[0] // DEC_GROUPS
    groups = [pages[g * gk:(g + 1) * gk] for g in range(DEC_GROUPS)]
    scores, probs, parts = {}, {}, []

    def score(g):
        scores[g] = _nt(qlat, groups[g]) + _mm(qr, krt[:, g * gk:(g + 1) * gk])

    def soft(g):
        m_g = jnp.max(scores[g], axis=-1, keepdims=True)
        p = jnp.exp2(scores[g] - m_g)
        probs[g] = (m_g, jnp.sum(p, axis=-1, keepdims=True), p.astype(BF16))

    def value(g):
        m_g, l_g, pb = probs[g]
        parts.append((m_g, l_g, _mm(pb, groups[g])))

    for g in range(DEC_GROUPS):
        score(g)
    soft(0)
    for g in range(DEC_GROUPS):
        if g + 1 < DEC_GROUPS:
            soft(g + 1)
        value(g)
    m_prev = m_sc[...]
    m_new = functools.reduce(jnp.maximum, [m_g for m_g, _, _ in parts], m_prev)
    alpha = jnp.exp2(m_prev - m_new)
    l_new = alpha * l_sc[...]
    acc = alpha * acc_sc[...]
    for m_g, l_g, o_g in parts:
        w_g = jnp.exp2(m_g - m_new)
        l_new = l_new + w_g * l_g
        acc = acc + w_g * o_g
    m_sc[...] = m_new
    l_sc[...] = l_new
    acc_sc[...] = acc

    @pl.when(j == nj - 1)
    def _():
        out = acc_sc[...] / l_sc[...]
        for hh in range(MLA_H):
            o_ref[0, :, hh * KV_LORA:(hh + 1) * KV_LORA] = out[hh * ds:(hh + 1) * ds, :]

    @pl.when(step == last_step)
    def _():
        for c in copies(step + 1, 1 - slot):
            c.wait()


def _attn_sample(qlat, qh, ckvn, krn, pool_ckv, pool_kr, page_table):
    b, ds, _ = qlat.shape
    nq = ds * MLA_H
    n_pages = page_table.shape[1]
    pg = DEC_PG
    pt = page_table.reshape(-1)
    pt = jnp.concatenate([pt, pt[:pg]])

    fixed = lambda w, r: pl.BlockSpec((1, r, w), lambda bi, j, pt_ref: (bi, 0, 0))
    grid_spec = pltpu.PrefetchScalarGridSpec(
        num_scalar_prefetch=1,
        grid=(b, n_pages // pg),
        in_specs=[fixed(MLA_H * KV_LORA, ds), fixed(MLA_H * HEAD_PAD, ds), fixed(KV_LORA, ds), fixed(HEAD_PAD, ds),
                  pl.BlockSpec(memory_space=pl.ANY), pl.BlockSpec(memory_space=pl.ANY)],
        out_specs=fixed(MLA_H * KV_LORA, ds),
        scratch_shapes=[pltpu.VMEM((nq, 1), F32), pltpu.VMEM((nq, 1), F32),
                        pltpu.VMEM((nq, KV_LORA), F32),
                        pltpu.VMEM((PAGE_SIZE, KV_LORA), BF16), pltpu.VMEM((PAGE_SIZE, HEAD_PAD), BF16),
                        pltpu.VMEM((2, pg * PAGE_SIZE, KV_LORA), F32),
                        pltpu.VMEM((2, MLA_ROPE, pg * PAGE_SIZE), F32),
                        pltpu.SemaphoreType.DMA((2, 2))],
    )
    return pl.pallas_call(
        _dec_kernel,
        grid_spec=grid_spec,
        out_shape=jax.ShapeDtypeStruct((b, ds, MLA_H * KV_LORA), F32),
        compiler_params=pltpu.CompilerParams(dimension_semantics=("arbitrary", "arbitrary"),
                                             vmem_limit_bytes=VMEM_LIMIT),
        name="attn_sample",
    )(pt, qlat, qh, ckvn, krn, pool_ckv, pool_kr)


def _sigmoid(x):
    return 0.5 * jnp.tanh(0.5 * x) + 0.5


def _silu(x):
    h = 0.5 * x
    return h * jnp.tanh(h) + h


def _back_kernel(absorb, x_ref, gate_ref, hml_ref, mo_ref, mz_ref, o_ref, az_ref, gn_ref, wuv_ref,
                 wout_ref, lng_ref, lnb_ref, y_ref):
    tb, ts, d = x_ref.shape
    tm = tb * ts
    hm = hml_ref[...].astype(F32) * _sigmoid(mo_ref[...].astype(F32))
    parts = []
    for hh in range(ML_H):
        t = hm[:, hh * ML_DV:(hh + 1) * ML_DV]
        mu = jnp.mean(t, axis=-1, keepdims=True)
        tc = t - mu
        var = jnp.mean(tc * tc, axis=-1, keepdims=True)
        parts.append(tc * lax.rsqrt(var + EPS))
    mz = mz_ref[...].astype(F32)
    y_ml = jnp.concatenate(parts, axis=1) * gn_ref[...] * _silu(mz)
    if absorb:
        olat = o_ref[...].astype(BF16)
        o_mla = jnp.concatenate(
            [_mm(olat[:, p * 2 * KV_LORA:(p + 1) * 2 * KV_LORA], wuv_ref[p]) for p in range(MLA_H // 2)],
            axis=1)
    else:
        o_mla = o_ref[...].astype(F32)
    az = az_ref[...].astype(F32)
    y_mla = o_mla * _silu(az)
    out = _mm(y_ml.astype(BF16), wout_ref[0:ML_W, :]) + _mm(y_mla.astype(BF16), wout_ref[ML_W:, :])
    z = ALPHA * x_ref[...] + gate_ref[...] * out.reshape(tb, ts, d)
    mu = jnp.mean(z, axis=-1, keepdims=True)
    zc = z - mu
    var = jnp.mean(zc * zc, axis=-1, keepdims=True)
    y_ref[...] = zc * lax.rsqrt(var + EPS) * lng_ref[...] + lnb_ref[...]


def _back(absorb, x, mod, hml, mo, mz, o, az, wts):
    b, s, d = x.shape
    n = b * s
    tm = FRONT_TM
    if absorb:
        tb, ts = tm // s, s
        x_map = lambda i: (i, 0, 0)
        gate_map = lambda i: (i, 0, 2)
    else:
        tb, ts = 1, tm
        spb = s // tm
        x_map = lambda i: (i // spb, i % spb, 0)
        gate_map = lambda i: (i // spb, 0, 2)
    const2 = lambda i: (0, 0)
    tok = lambda w: pl.BlockSpec((tm, w), lambda i: (i, 0))
    ow = o.shape[1]
    return pl.pallas_call(
        functools.partial(_back_kernel, absorb),
        grid=(n // tm,),
        in_specs=[pl.BlockSpec((tb, ts, d), x_map),
                  pl.BlockSpec((tb, 1, d), gate_map),
                  tok(ML_W), tok(ML_W), tok(ML_W), tok(ow), tok(MLA_W),
                  pl.BlockSpec((1, ML_W), const2),
                  pl.BlockSpec((MLA_H // 2, 2 * KV_LORA, LANES), lambda i: (0, 0, 0)),
                  pl.BlockSpec((d, d), const2),
                  pl.BlockSpec((1, d), const2),
                  pl.BlockSpec((1, d), const2)],
        out_specs=pl.BlockSpec((tb, ts, d), x_map),
        out_shape=jax.ShapeDtypeStruct((b, s, d), F32),
        compiler_params=pltpu.CompilerParams(dimension_semantics=("arbitrary",),
                                             vmem_limit_bytes=VMEM_LIMIT),
        name="back_sample" if absorb else "back_prompt",
    )(x, mod, hml, mo, mz, o, az, wts["gn"], wts["wuv_pair"], wts["w_out"], wts["ln_g"], wts["ln_b"])


def _prep_weights(l, w_in, ml_b_i, ml_b_f, ml_gn, mla_q_norm, mla_kv_norm, mla_w_uq, mla_w_uk,
                  mla_w_uv, w_out, ln_g, ln_b):
    offs = np.cumsum((0, 512, 512, 512, ML_H, ML_H, 512, 512, Q_LORA, KV_LORA, MLA_ROPE, 512))
    names = ("mq", "mk", "mv", "mi", "mf", "mo", "mz", "cq", "ckv", "kr", "az")
    w = {nm: w_in[l][:, offs[i]:offs[i + 1]] for i, nm in enumerate(names)}
    d = w_in.shape[1]
    z = lambda c: jnp.zeros((d, c), F32)
    kr1, kr2 = w["kr"][:, :ROPE_HALF], w["kr"][:, ROPE_HALF:]
    tail = HEAD_PAD - MLA_NOPE - MLA_ROPE
    w_in_p = jnp.concatenate(
        [w["mq"], w["mk"], w["mv"], w["mi"], w["mf"], z(LANES - 2 * ML_H), w["mo"], w["mz"], w["cq"], w["ckv"],
         z(MLA_NOPE), kr1, kr2, z(tail), z(MLA_NOPE), -kr2, kr1, z(tail), w["az"]], axis=1).astype(BF16)
    b_gate = jnp.concatenate([ml_b_i[l], ml_b_f[l], jnp.zeros((LANES - 2 * ML_H,), F32)]).reshape(1, LANES)
    uq = mla_w_uq[l].reshape(Q_LORA, MLA_H, MLA_NOPE + MLA_ROPE)
    nope, r1, r2 = uq[..., :MLA_NOPE], uq[..., MLA_NOPE:MLA_NOPE + ROPE_HALF], uq[..., MLA_NOPE + ROPE_HALF:]
    zq = lambda c: jnp.zeros((Q_LORA, MLA_H, c), F32)
    wqa = jnp.concatenate([nope, r1, r2, zq(tail)], axis=-1).reshape(Q_LORA, MLA_H * HEAD_PAD).astype(BF16)
    wqr = jnp.concatenate([zq(MLA_NOPE), -r2, r1, zq(tail)], axis=-1).reshape(Q_LORA, MLA_H * HEAD_PAD).astype(BF16)
    uk = mla_w_uk[l].reshape(KV_LORA, MLA_H, MLA_NOPE)
    wk_pad = jnp.concatenate([uk, jnp.zeros((KV_LORA, MLA_H, HEAD_PAD - MLA_NOPE), F32)],
                             axis=-1).reshape(KV_LORA, MLA_H * HEAD_PAD).astype(BF16)
    wukt = jnp.concatenate([jnp.transpose(uk, (1, 2, 0)),
                            jnp.zeros((MLA_H, HEAD_PAD - MLA_NOPE, KV_LORA), F32)], axis=1).astype(BF16)
    uv = mla_w_uv[l].reshape(KV_LORA, MLA_H // 2, 2, MLA_VD)
    zv = jnp.zeros((KV_LORA, MLA_H // 2, MLA_VD), F32)
    wuv_pair = jnp.concatenate(
        [jnp.concatenate([uv[:, :, 0], zv], axis=-1), jnp.concatenate([zv, uv[:, :, 1]], axis=-1)],
        axis=0)
    wuv_pair = jnp.transpose(wuv_pair, (1, 0, 2)).astype(BF16)
    return dict(w_in=w_in_p, b_gate=b_gate, q_norm=mla_q_norm[l].reshape(1, -1),
                kv_norm=mla_kv_norm[l].reshape(1, -1), wqa=wqa, wqr=wqr, wk_pad=wk_pad, wukt=wukt,
                w_uv_t=mla_w_uv[l].T.astype(BF16), wuv_pair=wuv_pair, gn=ml_gn[l].reshape(1, -1),
                w_out=w_out[l].astype(BF16), ln_g=ln_g[l].reshape(1, -1), ln_b=ln_b[l].reshape(1, -1))


def _rope_tables(pos):
    f32 = np.float32
    inv = ROPE_THETA ** (-np.arange(ROPE_HALF, dtype=np.float64) / ROPE_HALF)
    ang = pos.astype(np.float64)[:, None] * inv[None, :]
    n = pos.shape[0]
    tail = HEAD_PAD - MLA_NOPE - MLA_ROPE
    cos, sin = np.cos(ang).astype(f32), np.sin(ang).astype(f32)
    cm = np.concatenate([np.ones((n, MLA_NOPE), f32), cos, cos, np.zeros((n, tail), f32)], 1)
    sm = np.concatenate([np.zeros((n, MLA_NOPE), f32), sin, sin, np.zeros((n, tail), f32)], 1)
    return cm, sm


def kernel(x_prompt, x_sample, c_prompt, c_sample, cache_ckv, cache_krope, state_C, state_n, state_m,
           page_table, w_ada, b_ada, w_in, ml_b_i, ml_b_f, ml_gn, mla_q_norm, mla_kv_norm,
           mla_w_uq, mla_w_uk, mla_w_uv, w_out, ln_g, ln_b):
    bp, sp, d = x_prompt.shape
    bs, ss, _ = x_sample.shape
    past = page_table.shape[1] * PAGE_SIZE
    cm_p, sm_p = (jnp.asarray(t) for t in _rope_tables(np.arange(sp)))
    reps = FRONT_TM // ss
    cm_s, sm_s = (jnp.asarray(np.tile(t, (reps, 1))) for t in _rope_tables(past + np.arange(ss)))
    pad = (-(bp + bs)) % 8
    c_all = jnp.concatenate([c_prompt, c_sample, jnp.zeros((pad, d), F32)], axis=0)

    xp, xs = x_prompt, x_sample
    outs = [[] for _ in range(10)]
    for l in range(DEPTH):
        wts = _prep_weights(l, w_in, ml_b_i, ml_b_f, ml_gn, mla_q_norm, mla_kv_norm, mla_w_uq, mla_w_uk,
                            mla_w_uv, w_out, ln_g, ln_b)
        mod = _ada(c_all, w_ada[l], b_ada[l])
        mod_p = mod[:bp].reshape(bp, 1, 3 * d)
        mod_s = mod[bp:bp + bs].reshape(bs, 1, 3 * d)

        mq, mk, mv, g, mo, mz, az, ckv, kr, qh, kh, vt = _front(False, xp, mod_p, cm_p, sm_p, wts)
        seq3 = lambda t: t.reshape(bp, sp, -1)
        c0 = jnp.zeros((bp, ML_H, ML_DK, ML_DV), F32)
        n0 = jnp.zeros((bp, ML_H, 1, ML_DK), F32)
        m0 = jnp.full((bp, ML_H, 1, LANES), -jnp.inf, F32)
        hml, c_p, n_p, m_p = _mlstm(seq3(mq), seq3(mk), seq3(mv), seq3(g), c0, n0, m0, ML_CHUNK, bp)
        o_p = _attn_prompt(qh, kh, vt)
        yp = _back(False, xp, mod_p, hml.reshape(bp * sp, -1), mo, mz, o_p.reshape(bp * sp, -1), az, wts)
        outs[0].append(ckv.reshape(bp, sp, KV_LORA))
        outs[1].append(kr.reshape(bp, sp, MLA_ROPE))
        outs[2].append(c_p)
        outs[3].append(n_p[:, :, 0, :])
        outs[4].append(m_p[:, :, 0, 0])

        mq, mk, mv, g, mo, mz, az, ckv, kr, qlat, qh, krp = _front(True, xs, mod_s, cm_s, sm_s, wts)
        seq3 = lambda t: t.reshape(bs, ss, -1)
        n0 = state_n[l].astype(F32).reshape(bs, ML_H, 1, ML_DK)
        m0 = jnp.broadcast_to(state_m[l].astype(F32)[:, :, None, None], (bs, ML_H, 1, LANES))
        hml, c_s, n_s, m_s = _mlstm(seq3(mq), seq3(mk), seq3(mv), seq3(g), state_C[l].astype(F32), n0, m0, ss, 8)
        o_s = _attn_sample(seq3(qlat), seq3(qh), seq3(ckv), seq3(krp), cache_ckv[l],
                           jnp.swapaxes(cache_krope[l], 1, 2), page_table)
        ys = _back(True, xs, mod_s, hml.reshape(bs * ss, -1), mo, mz,
                   o_s.reshape(bs * ss, MLA_H * KV_LORA), az, wts)
        outs[5].append(ckv.reshape(bs, ss, KV_LORA).astype(cache_ckv.dtype))
        outs[6].append(kr.reshape(bs, ss, MLA_ROPE).astype(cache_krope.dtype))
        outs[7].append(c_s.astype(state_C.dtype))
        outs[8].append(n_s[:, :, 0, :].astype(state_n.dtype))
        outs[9].append(m_s[:, :, 0, 0].astype(state_m.dtype))
        xp, xs = yp, ys
    return (xp, xs) + tuple(jnp.stack(o) for o in outs)
```

```python
import functools

import numpy as np
import jax
import jax.numpy as jnp
from jax import lax
from jax.experimental import pallas as pl
from jax.experimental.pallas import tpu as pltpu

F32 = jnp.float32
BF16 = jnp.bfloat16

D_MODEL = 1024
DEPTH = 1
PAGE_SIZE = 128
ML_W = 512
MLA_W = 512
ML_DK = 128
ML_DV = 128
ML_H = 4
MLA_VD = 64
MLA_H = 8
MLA_NOPE = 64
MLA_ROPE = 32
ROPE_HALF = MLA_ROPE // 2
MLA_SCALE = (MLA_NOPE + MLA_ROPE) ** -0.5
Q_SCALE = MLA_SCALE * float(np.log2(np.e))
Q_LORA = 384
KV_LORA = 256
ROPE_THETA = 10000.0
ML_CHUNK = 128
EPS = 1e-6
ALPHA = (2.0 * DEPTH) ** 0.25

LANES = 128
HEAD_PAD = 128
VMEM_LIMIT = 48 * 1024 * 1024

_SEG = {}
_off = 0
for _name, _w in (("mq", 512), ("mk", 512), ("mv", 512), ("gate", 128), ("mo", 512), ("mz", 512),
                  ("cq", Q_LORA), ("ckv", KV_LORA), ("kra", 128), ("krr", 128), ("az", 512)):
    _SEG[_name] = (_off, _off + _w)
    _off += _w
N_IN_PAD = _off

FRONT_TM = 256
BACK_TM = 512
ATT_TQ = 512
ATT_TK = 1024
ATT_KC = 256
MASKED_MAX_FLOOR = -1e30
DEC_PG = 64
DEC_GROUPS = 8
DEC_SLOTS = 2


def _nt(a, b):
    return lax.dot_general(a, b, (((1,), (1,)), ((), ())), preferred_element_type=F32)


def _tn(a, b):
    return lax.dot_general(a, b, (((0,), (0,)), ((), ())), preferred_element_type=F32)


def _mm(a, b):
    return jnp.dot(a, b, preferred_element_type=F32)


def _mm_exact(a, b):
    return jnp.dot(a, b, preferred_element_type=F32, precision=lax.Precision.HIGHEST)


def _nt_exact(a, b):
    return lax.dot_general(a, b, (((1,), (1,)), ((), ())), preferred_element_type=F32,
                           precision=lax.Precision.HIGHEST)


def _ada_kernel(c_ref, w_ref, b_ref, o_ref):
    o_ref[...] = _mm(c_ref[...].astype(BF16), w_ref[...].astype(BF16)) + b_ref[...]


def _ada(c_all, w_ada, b_ada):
    m = c_all.shape[0]
    tn = 512
    return pl.pallas_call(
        _ada_kernel,
        grid=(3 * D_MODEL // tn,),
        in_specs=[pl.BlockSpec((m, D_MODEL), lambda j: (0, 0)),
                  pl.BlockSpec((D_MODEL, tn), lambda j: (0, j)),
                  pl.BlockSpec((1, tn), lambda j: (0, j))],
        out_specs=pl.BlockSpec((m, tn), lambda j: (0, j)),
        out_shape=jax.ShapeDtypeStruct((m, 3 * D_MODEL), F32),
        name="ada",
    )(c_all, w_ada, b_ada.reshape(1, -1))


def _rms(x, g):
    return x * lax.rsqrt(jnp.mean(x * x, axis=-1, keepdims=True) + EPS) * g


def _log_sigmoid(x):
    return jnp.minimum(x, 0.0) - jnp.log1p(jnp.exp(-jnp.abs(x)))


def _front_kernel(absorb, x_ref, sh_ref, sc_ref, cm_ref, sm_ref, win_ref, bg_ref, qn_ref, kvn_ref,
                  wqa_ref, wqr_ref, wk_ref, wv_ref,
                  mq_ref, mk_ref, mv_ref, g_ref, mo_ref, mz_ref, az_ref, ckv_ref, kr_ref,
                  o1_ref, o2_ref, o3_ref):
    tb, ts, d = x_ref.shape
    tm = tb * ts
    h = x_ref[...] * (1.0 + sc_ref[...]) + sh_ref[...]
    h = h.reshape(tm, d).astype(BF16)

    def seg(name):
        lo, hi = _SEG[name]
        return _mm(h, win_ref[:, lo:hi])

    mq_ref[...] = seg("mq").astype(BF16)
    mk_ref[...] = (seg("mk") * (ML_DK ** -0.5)).astype(BF16)
    mv_ref[...] = seg("mv").astype(BF16)
    gz = seg("gate") + bg_ref[...]
    lane = lax.broadcasted_iota(jnp.int32, gz.shape, 1)
    g_ref[...] = jnp.where(lane < ML_H, gz, jnp.where(lane < 2 * ML_H, _log_sigmoid(gz), 0.0))
    mo_ref[...] = seg("mo").astype(BF16)
    mz_ref[...] = seg("mz").astype(BF16)
    az_ref[...] = seg("az").astype(BF16)

    cm = cm_ref[...]
    sm = sm_ref[...]
    ckvn = _rms(seg("ckv"), kvn_ref[...])
    ckv_ref[...] = ckvn
    ckvn_b = ckvn.astype(BF16)
    krp = seg("kra") * cm + seg("krr") * sm
    kr_ref[...] = krp[:, MLA_NOPE:MLA_NOPE + MLA_ROPE]

    cqn = _rms(seg("cq"), qn_ref[...]).astype(BF16)
    qa = _mm(cqn, wqa_ref[...])
    qr = _mm(cqn, wqr_ref[...])
    for hh in range(MLA_H):
        sl = slice(hh * HEAD_PAD, (hh + 1) * HEAD_PAD)
        qh = ((qa[:, sl] * cm + qr[:, sl] * sm) * Q_SCALE).astype(BF16)
        if absorb:
            o1_ref[:, hh * KV_LORA:(hh + 1) * KV_LORA] = _mm(qh, wk_ref[hh]).astype(BF16)
            o2_ref[:, sl] = qh
        else:
            o1_ref[0, hh] = qh
            o2_ref[0, hh] = (_mm(ckvn_b, wk_ref[:, sl]) + krp).astype(BF16)
    if absorb:
        o3_ref[...] = krp.astype(BF16)
    else:
        o3_ref[0] = _nt(wv_ref[...], ckvn_b).astype(BF16)


def _front(absorb, x, mod, cm, sm, wts):
    b, s, d = x.shape
    n = b * s
    tm = FRONT_TM
    if absorb:
        tb, ts = tm // s, s
        x_map = lambda i: (i, 0, 0)
        mod_map = lambda k: (lambda i: (i, 0, k))
        tab_map = lambda i: (0, 0)
    else:
        tb, ts = 1, tm
        spb = s // tm
        x_map = lambda i: (i // spb, i % spb, 0)
        mod_map = lambda k: (lambda i: (i // spb, 0, k))
        tab_map = lambda i: (i % spb, 0)
    const2 = lambda i: (0, 0)
    tok = lambda w: pl.BlockSpec((tm, w), lambda i: (i, 0))
    in_specs = [
        pl.BlockSpec((tb, ts, d), x_map),
        pl.BlockSpec((tb, 1, d), mod_map(0)),
        pl.BlockSpec((tb, 1, d), mod_map(1)),
        pl.BlockSpec((tm, LANES), tab_map),
        pl.BlockSpec((tm, LANES), tab_map),
        pl.BlockSpec((d, N_IN_PAD), const2),
        pl.BlockSpec((1, LANES), const2),
        pl.BlockSpec((1, Q_LORA), const2),
        pl.BlockSpec((1, KV_LORA), const2),
        pl.BlockSpec((Q_LORA, MLA_H * HEAD_PAD), const2),
        pl.BlockSpec((Q_LORA, MLA_H * HEAD_PAD), const2),
    ]
    out_specs = [tok(512), tok(512), tok(512), tok(LANES), tok(512), tok(512), tok(512),
                 tok(KV_LORA), tok(MLA_ROPE)]
    out_shape = [jax.ShapeDtypeStruct((n, w), dt)
                 for w, dt in ((512, BF16), (512, BF16), (512, BF16), (LANES, F32), (512, BF16), (512, BF16),
                               (512, BF16), (KV_LORA, F32), (MLA_ROPE, F32))]
    if absorb:
        in_specs += [pl.BlockSpec((MLA_H, HEAD_PAD, KV_LORA), lambda i: (0, 0, 0)),
                     pl.BlockSpec((MLA_W, KV_LORA), const2)]
        out_specs += [tok(MLA_H * KV_LORA), tok(MLA_H * HEAD_PAD), tok(HEAD_PAD)]
        out_shape += [jax.ShapeDtypeStruct((n, MLA_H * KV_LORA), BF16),
                      jax.ShapeDtypeStruct((n, MLA_H * HEAD_PAD), BF16),
                      jax.ShapeDtypeStruct((n, HEAD_PAD), BF16)]
        wk = wts["wukt"]
    else:
        head_map = lambda i: (i // spb, 0, i % spb, 0)
        in_specs += [pl.BlockSpec((KV_LORA, MLA_H * HEAD_PAD), const2),
                     pl.BlockSpec((MLA_W, KV_LORA), const2)]
        out_specs += [pl.BlockSpec((1, MLA_H, tm, HEAD_PAD), head_map),
                      pl.BlockSpec((1, MLA_H, tm, HEAD_PAD), head_map),
                      pl.BlockSpec((1, MLA_W, tm), lambda i: (i // spb, 0, i % spb))]
        out_shape += [jax.ShapeDtypeStruct((b, MLA_H, s, HEAD_PAD), BF16),
                      jax.ShapeDtypeStruct((b, MLA_H, s, HEAD_PAD), BF16),
                      jax.ShapeDtypeStruct((b, MLA_W, s), BF16)]
        wk = wts["wk_pad"]
    return pl.pallas_call(
        functools.partial(_front_kernel, absorb),
        grid=(n // tm,),
        in_specs=in_specs,
        out_specs=out_specs,
        out_shape=out_shape,
        compiler_params=pltpu.CompilerParams(dimension_semantics=("arbitrary",),
                                             vmem_limit_bytes=VMEM_LIMIT),
        name="front_sample" if absorb else "front_prompt",
    )(x, mod, mod, cm, sm, wts["w_in"], wts["b_gate"], wts["q_norm"], wts["kv_norm"],
      wts["wqa"], wts["wqr"], wk, wts["w_uv_t"])


def _mlstm_kernel(q_ref, k_ref, v_ref, g_ref, c0_ref, n0_ref, m0_ref,
                  h_ref, c_ref, n_ref, m_ref):
    tb, L, _ = q_ref.shape
    ci = pl.program_id(1)

    @pl.when(ci == 0)
    def _():
        c_ref[...] = c0_ref[...]
        n_ref[...] = n0_ref[...]
        m_ref[...] = m0_ref[...]

    row = lax.broadcasted_iota(jnp.int32, (L, L), 0)
    col = lax.broadcasted_iota(jnp.int32, (L, L), 1)
    causal = col <= row
    tril = causal.astype(F32)
    sel = (lax.broadcasted_iota(jnp.int32, (8, LANES), 0)
           == lax.broadcasted_iota(jnp.int32, (8, LANES), 1)).astype(F32)

    chains = [(t, hh) for t in range(tb) for hh in range(ML_H)]
    sl = lambda hh: slice(hh * ML_DK, (hh + 1) * ML_DK)
    gates = {}
    for t in range(tb):
        g = g_ref[t]
        fcum = _mm_exact(tril, g)
        g_rows = _nt_exact(sel, g)
        f_rows = _nt_exact(sel, fcum)
        gates[t] = (g, fcum, g_rows, f_rows)

    qk, qc, c_prevs = {}, {}, {}
    for t, hh in chains:
        qb = q_ref[t, :, sl(hh)]
        c_prevs[t, hh] = c_ref[t, hh]
        qk[t, hh] = _nt(qb, k_ref[t, :, sl(hh)])
        qc[t, hh] = _mm(qb, c_prevs[t, hh].astype(BF16))

    stab = {}
    for t, hh in chains:
        g, fcum, g_rows, f_rows = gates[t]
        m_prev = m_ref[t, hh][:, :1]
        f_col = fcum[:, ML_H + hh:ML_H + hh + 1]
        f_row = f_rows[ML_H + hh:ML_H + hh + 1, :]
        ig_row = g_rows[hh:hh + 1, :]
        dmat = jnp.where(causal, f_col - f_row + ig_row, -jnp.inf)
        m_inter = jnp.broadcast_to(f_col, (L, LANES)) + m_prev
        m_t = jnp.maximum(m_inter, jnp.max(dmat, axis=-1, keepdims=True))
        stab[t, hh] = (jnp.exp(dmat - m_t[:, :L]), jnp.exp(m_inter - m_t), m_t)

    svs = {}
    ones_cols = jnp.ones((L, LANES), BF16)
    for t, hh in chains:
        sq = (qk[t, hh] * stab[t, hh][0]).astype(BF16)
        svs[t, hh] = _mm(sq, jnp.concatenate([v_ref[t, :, sl(hh)], ones_cols], axis=1))

    for t, hh in chains:
        _, a, m_t = stab[t, hh]
        n_rep = jnp.broadcast_to(n_ref[t, hh], (LANES, ML_DK)).astype(BF16)
        qn = _nt(q_ref[t, :, sl(hh)], n_rep)
        num = svs[t, hh][:, :ML_DV] + a * qc[t, hh]
        den = svs[t, hh][:, ML_DV:] + a * qn
        h_ref[t, :, sl(hh)] = (num / jnp.maximum(jnp.abs(den), jnp.exp(-m_t))).astype(h_ref.dtype)

    for t, hh in chains:
        w, a, m_t = stab[t, hh]
        wl_row = w[L - 1:L, :]
        al = a[L - 1:L, :]
        kb = k_ref[t, :, sl(hh)]
        kw_t = (kb.astype(F32).T * wl_row).astype(BF16)
        c_ref[t, hh] = al[:, :1] * c_prevs[t, hh] + _mm(kw_t, v_ref[t, :, sl(hh)])
        n_ref[t, hh] = al * n_ref[t, hh] + _mm(w[L - 8:L, :].astype(BF16), kb)[7:8, :]
        m_ref[t, hh] = m_t[L - 1:L, :]


def _mlstm(q, k, v, g, c0, n0, m0, chunk, tb):
    b, s, w = q.shape
    nc = s // chunk
    seq = lambda ww: pl.BlockSpec((tb, chunk, ww), lambda i, c: (i, c, 0))
    st_c = pl.BlockSpec((tb, ML_H, ML_DK, ML_DV), lambda i, c: (i, 0, 0, 0))
    st_v = pl.BlockSpec((tb, ML_H, 1, LANES), lambda i, c: (i, 0, 0, 0))
    return pl.pallas_call(
        _mlstm_kernel,
        grid=(b // tb, nc),
        in_specs=[seq(w), seq(w), seq(w), seq(LANES), st_c, st_v, st_v],
        out_specs=[seq(w), st_c, st_v, st_v],
        out_shape=[jax.ShapeDtypeStruct((b, s, w), BF16),
                   jax.ShapeDtypeStruct((b, ML_H, ML_DK, ML_DV), F32),
                   jax.ShapeDtypeStruct((b, ML_H, 1, LANES), F32),
                   jax.ShapeDtypeStruct((b, ML_H, 1, LANES), F32)],
        compiler_params=pltpu.CompilerParams(dimension_semantics=("arbitrary", "arbitrary"),
                                             vmem_limit_bytes=VMEM_LIMIT),
        name="mlstm",
    )(q, k, v, g, c0, n0, m0)


def _attn_kernel(qi_ref, ki_ref, last_ref, q_ref, k_ref, vt_ref, o_ref, m_sc, l_sc, acc_sc):
    step = pl.program_id(2)
    qi = qi_ref[step]
    ki = ki_ref[step]
    tq = q_ref.shape[2]
    tk = k_ref.shape[2]

    @pl.when(ki == 0)
    def _():
        m_sc[...] = jnp.full(m_sc.shape, -jnp.inf, F32)
        l_sc[...] = jnp.zeros(l_sc.shape, F32)
        acc_sc[...] = jnp.zeros(acc_sc.shape, F32)

    def tile(diag):
        chains = [(hh, c0) for hh in range(2) for c0 in range(0, min(tk, diag + tq), ATT_KC)]
        scores = {}
        for hh, c0 in chains:
            st = _nt(k_ref[0, hh, c0:c0 + ATT_KC, :], q_ref[0, hh])
            if c0 >= diag:
                keys = c0 - diag + lax.broadcasted_iota(jnp.int32, (ATT_KC, tq), 0)
                qrys = lax.broadcasted_iota(jnp.int32, (ATT_KC, tq), 1)
                st = jnp.where(keys <= qrys, st, -jnp.inf)
            scores[hh, c0] = st
        parts = ([], [])
        ones_rows = jnp.ones((16, ATT_KC), BF16)
        for hh, c0 in chains:
            st = scores[hh, c0]
            m_g = jnp.max(st, axis=0, keepdims=True)
            if c0 >= diag:
                m_g = jnp.maximum(m_g, MASKED_MAX_FLOOR)
            pt = jnp.exp2(st - m_g).astype(BF16)
            rows = slice(hh * MLA_VD, (hh + 1) * MLA_VD)
            v_ones = jnp.concatenate([vt_ref[0, rows, c0:c0 + ATT_KC], ones_rows], axis=0)
            o_g = _mm(v_ones, pt)
            parts[hh].append((m_g, o_g[MLA_VD:MLA_VD + 1, :], o_g[:MLA_VD, :]))
        for hh in range(2):
            rows = slice(hh * MLA_VD, (hh + 1) * MLA_VD)
            m_prev = m_sc[hh]
            m_new = functools.reduce(jnp.maximum, [m_g for m_g, _, _ in parts[hh]], m_prev)
            alpha = jnp.exp2(m_prev - m_new)
            l_new = alpha * l_sc[hh]
            acc = alpha * acc_sc[rows, :]
            for m_g, l_g, o_g in parts[hh]:
                w_g = jnp.exp2(m_g - m_new)
                l_new = l_new + w_g * l_g
                acc = acc + w_g * o_g
            m_sc[hh] = m_new
            l_sc[hh] = l_new
            acc_sc[rows, :] = acc

    first_query = qi * tq - ki * tk
    for diag in range(0, tk, tq):
        pl.when(first_query == diag)(functools.partial(tile, diag))
    pl.when(first_query >= tk)(functools.partial(tile, tk))

    @pl.when(last_ref[step] == 1)
    def _():
        out_t = jnp.concatenate([acc_sc[0:MLA_VD, :] / l_sc[0], acc_sc[MLA_VD:, :] / l_sc[1]], axis=0)
        o_ref[0] = out_t.T.astype(o_ref.dtype)


def _attn_prompt(qh, kh, vt):
    b, nh, s, _ = qh.shape
    tq, tk = ATT_TQ, ATT_TK
    assert tk % tq == 0 and tq % ATT_KC == 0 and s % tk == 0
    qi_l, ki_l, last_l = [], [], []
    for qi in range(s // tq):
        nk = ((qi + 1) * tq + tk - 1) // tk
        for ki in range(nk):
            qi_l.append(qi)
            ki_l.append(ki)
            last_l.append(int(ki == nk - 1))
    nsteps = len(qi_l)
    sched = [jnp.asarray(np.asarray(a, np.int32)) for a in (qi_l, ki_l, last_l)]
    grid_spec = pltpu.PrefetchScalarGridSpec(
        num_scalar_prefetch=3,
        grid=(b, nh // 2, nsteps),
        in_specs=[pl.BlockSpec((1, 2, tq, HEAD_PAD), lambda bi, hp, st, qi, ki, la: (bi, hp, qi[st], 0)),
                  pl.BlockSpec((1, 2, tk, HEAD_PAD), lambda bi, hp, st, qi, ki, la: (bi, hp, ki[st], 0)),
                  pl.BlockSpec((1, 2 * MLA_VD, tk), lambda bi, hp, st, qi, ki, la: (bi, hp, ki[st]))],
        out_specs=pl.BlockSpec((1, tq, LANES), lambda bi, hp, st, qi, ki, la: (bi, qi[st], hp)),
        scratch_shapes=[pltpu.VMEM((2, 1, tq), F32), pltpu.VMEM((2, 1, tq), F32),
                        pltpu.VMEM((2 * MLA_VD, tq), F32)],
    )
    return pl.pallas_call(
        _attn_kernel,
        grid_spec=grid_spec,
        out_shape=jax.ShapeDtypeStruct((b, s, nh * MLA_VD), BF16),
        compiler_params=pltpu.CompilerParams(
            dimension_semantics=("arbitrary", "arbitrary", "arbitrary"),
            vmem_limit_bytes=VMEM_LIMIT),
        name="attn_prompt",
    )(*sched, qh, kh, vt)


def _dec_page_copies(pt_ref, pool_ckv, pool_kr, cbuf, kbuf, sems, step, slot):
    copies = []
    for i in range(DEC_PG):
        page = pt_ref[step * DEC_PG + i]
        keys = pl.ds(i * PAGE_SIZE, PAGE_SIZE)
        copies.append(pltpu.make_async_copy(pool_ckv.at[page], cbuf.at[slot, keys, :], sems.at[slot, 0]))
        copies.append(pltpu.make_async_copy(pool_kr.at[page], kbuf.at[slot, :, keys], sems.at[slot, 1]))
    return copies


def _dec_wait_slot(cbuf, kbuf, sems, slot):
    pltpu.make_async_copy(cbuf.at[slot], cbuf.at[slot], sems.at[slot, 0]).wait()
    pltpu.make_async_copy(kbuf.at[slot], kbuf.at[slot], sems.at[slot, 1]).wait()


def _dec_kernel(pt_ref, qlat_ref, qh_ref, ckvn_ref, krn_ref, pool_ckv, pool_kr,
                o_ref, m_sc, l_sc, acc_sc, padc_sc, padk_sc, cbuf, kbuf, sems):
    j = pl.program_id(1)
    nj = pl.num_programs(1)
    step = pl.program_id(0) * nj + j
    last_step = pl.num_programs(0) * nj - 1
    slot = lax.rem(step, DEC_SLOTS)
    qlat = jnp.concatenate([qlat_ref[0, :, hh * KV_LORA:(hh + 1) * KV_LORA] for hh in range(MLA_H)], axis=0)
    qh = jnp.concatenate([qh_ref[0, :, hh * HEAD_PAD:(hh + 1) * HEAD_PAD] for hh in range(MLA_H)], axis=0)
    ds = ckvn_ref.shape[1]
    copies = functools.partial(_dec_page_copies, pt_ref, pool_ckv, pool_kr, cbuf, kbuf, sems)

    @pl.when(step == 0)
    def _():
        for s0 in range(DEC_SLOTS):
            for c in copies(s0, s0):
                c.start()

    _dec_wait_slot(cbuf, kbuf, sems, slot)
    next_copies = copies(step + DEC_SLOTS, slot)

    @pl.when(j == 0)
    def _():
        padc_sc[...] = jnp.zeros(padc_sc.shape, BF16)
        padk_sc[...] = jnp.zeros(padk_sc.shape, BF16)
        padc_sc[0:ds, :] = ckvn_ref[0].astype(BF16)
        padk_sc[0:ds, :] = krn_ref[0]
        cn = padc_sc[...]
        s = _nt(qlat, cn) + _nt(qh, padk_sc[...])
        tok = lax.broadcasted_iota(jnp.int32, (MLA_H, ds, s.shape[1]), 1).reshape(s.shape)
        key = lax.broadcasted_iota(jnp.int32, s.shape, 1)
        s = jnp.where(key <= tok, s, -jnp.inf)
        m = jnp.max(s, axis=-1, keepdims=True)
        p = jnp.exp2(s - m)
        m_sc[...] = m
        l_sc[...] = jnp.sum(p, axis=-1, keepdims=True)
        acc_sc[...] = _mm(p.astype(BF16), cn)

    qr = qh[:, MLA_NOPE:MLA_NOPE + MLA_ROPE]
    pages = cbuf[slot].astype(BF16)
    krt = kbuf[slot].astype(BF16)
    gk = pages.shape[0] // DEC_GROUPS
    groups = [pages[g * gk:(g + 1) * gk] for g in range(DEC_GROUPS)]
    scores, probs, parts = {}, {}, []

    def score(g):
        scores[g] = _nt(qlat, groups[g]) + _mm(qr, krt[:, g * gk:(g + 1) * gk])

    def soft(g):
        m_g = jnp.max(scores[g], axis=-1, keepdims=True)
        p = jnp.exp2(scores[g] - m_g)
        probs[g] = (m_g, jnp.sum(p, axis=-1, keepdims=True), p.astype(BF16))

    def value(g):
        m_g, l_g, pb = probs[g]
        parts.append((m_g, l_g, _mm(pb, groups[g])))

    per_group = len(next_copies) // DEC_GROUPS
    for g in range(DEC_GROUPS):
        score(g)
    soft(0)
    for g in range(DEC_GROUPS):
        if g + 1 < DEC_GROUPS:
            soft(g + 1)
        value(g)
        for c in next_copies[g * per_group:(g + 1) * per_group]:
            c.start()
    m_prev = m_sc[...]
    m_new = functools.reduce(jnp.maximum, [m_g for m_g, _, _ in parts], m_prev)
    alpha = jnp.exp2(m_prev - m_new)
    l_new = alpha * l_sc[...]
    acc = alpha * acc_sc[...]
    for m_g, l_g, o_g in parts:
        w_g = jnp.exp2(m_g - m_new)
        l_new = l_new + w_g * l_g
        acc = acc + w_g * o_g
    m_sc[...] = m_new
    l_sc[...] = l_new
    acc_sc[...] = acc

    @pl.when(j == nj - 1)
    def _():
        out = acc_sc[...] / l_sc[...]
        for hh in range(MLA_H):
            o_ref[0, :, hh * KV_LORA:(hh + 1) * KV_LORA] = out[hh * ds:(hh + 1) * ds, :]

    @pl.when(step == last_step)
    def _():
        for s0 in range(DEC_SLOTS):
            _dec_wait_slot(cbuf, kbuf, sems, s0)


def _attn_sample(qlat, qh, ckvn, krn, pool_ckv, pool_kr, page_table):
    b, ds, _ = qlat.shape
    nq = ds * MLA_H
    n_pages = page_table.shape[1]
    pg = DEC_PG
    pt = page_table.reshape(-1)
    pt = jnp.concatenate([pt, pt[:DEC_SLOTS * pg]])

    fixed = lambda w, r: pl.BlockSpec((1, r, w), lambda bi, j, pt_ref: (bi, 0, 0))
    grid_spec = pltpu.PrefetchScalarGridSpec(
        num_scalar_prefetch=1,
        grid=(b, n_pages // pg),
        in_specs=[fixed(MLA_H * KV_LORA, ds), fixed(MLA_H * HEAD_PAD, ds), fixed(KV_LORA, ds), fixed(HEAD_PAD, ds),
                  pl.BlockSpec(memory_space=pl.ANY), pl.BlockSpec(memory_space=pl.ANY)],
        out_specs=fixed(MLA_H * KV_LORA, ds),
        scratch_shapes=[pltpu.VMEM((nq, 1), F32), pltpu.VMEM((nq, 1), F32),
                        pltpu.VMEM((nq, KV_LORA), F32),
                        pltpu.VMEM((PAGE_SIZE, KV_LORA), BF16), pltpu.VMEM((PAGE_SIZE, HEAD_PAD), BF16),
                        pltpu.VMEM((DEC_SLOTS, pg * PAGE_SIZE, KV_LORA), F32),
                        pltpu.VMEM((DEC_SLOTS, MLA_ROPE, pg * PAGE_SIZE), F32),
                        pltpu.SemaphoreType.DMA((DEC_SLOTS, 2))],
    )
    return pl.pallas_call(
        _dec_kernel,
        grid_spec=grid_spec,
        out_shape=jax.ShapeDtypeStruct((b, ds, MLA_H * KV_LORA), F32),
        compiler_params=pltpu.CompilerParams(dimension_semantics=("arbitrary", "arbitrary"),
                                             vmem_limit_bytes=VMEM_LIMIT),
        name="attn_sample",
    )(pt, qlat, qh, ckvn, krn, pool_ckv, pool_kr)


def _sigmoid(x):
    return 0.5 * jnp.tanh(0.5 * x) + 0.5


def _silu(x):
    h = 0.5 * x
    return h * jnp.tanh(h) + h


def _back_kernel(absorb, x_ref, gate_ref, hml_ref, mo_ref, mz_ref, o_ref, az_ref, gn_ref, wuv_ref,
                 wout_ref, lng_ref, lnb_ref, y_ref):
    tb, ts, d = x_ref.shape
    tm = tb * ts
    hm = hml_ref[...].astype(F32) * _sigmoid(mo_ref[...].astype(F32))
    parts = []
    for hh in range(ML_H):
        t = hm[:, hh * ML_DV:(hh + 1) * ML_DV]
        mu = jnp.mean(t, axis=-1, keepdims=True)
        tc = t - mu
        var = jnp.mean(tc * tc, axis=-1, keepdims=True)
        parts.append(tc * lax.rsqrt(var + EPS))
    mz = mz_ref[...].astype(F32)
    y_ml = jnp.concatenate(parts, axis=1) * gn_ref[...] * _silu(mz)
    if absorb:
        olat = o_ref[...].astype(BF16)
        o_mla = jnp.concatenate(
            [_mm(olat[:, p * 2 * KV_LORA:(p + 1) * 2 * KV_LORA], wuv_ref[p]) for p in range(MLA_H // 2)],
            axis=1)
    else:
        o_mla = o_ref[...].astype(F32)
    az = az_ref[...].astype(F32)
    y_mla = o_mla * _silu(az)
    out = _mm(y_ml.astype(BF16), wout_ref[0:ML_W, :]) + _mm(y_mla.astype(BF16), wout_ref[ML_W:, :])
    z = ALPHA * x_ref[...] + gate_ref[...] * out.reshape(tb, ts, d)
    mu = jnp.mean(z, axis=-1, keepdims=True)
    zc = z - mu
    var = jnp.mean(zc * zc, axis=-1, keepdims=True)
    y_ref[...] = zc * lax.rsqrt(var + EPS) * lng_ref[...] + lnb_ref[...]


def _back(absorb, x, mod, hml, mo, mz, o, az, wts):
    b, s, d = x.shape
    n = b * s
    tm = BACK_TM
    if absorb:
        tb, ts = tm // s, s
        x_map = lambda i: (i, 0, 0)
        gate_map = lambda i: (i, 0, 2)
    else:
        tb, ts = 1, tm
        spb = s // tm
        x_map = lambda i: (i // spb, i % spb, 0)
        gate_map = lambda i: (i // spb, 0, 2)
    const2 = lambda i: (0, 0)
    tok = lambda w: pl.BlockSpec((tm, w), lambda i: (i, 0))
    ow = o.shape[1]
    return pl.pallas_call(
        functools.partial(_back_kernel, absorb),
        grid=(n // tm,),
        in_specs=[pl.BlockSpec((tb, ts, d), x_map),
                  pl.BlockSpec((tb, 1, d), gate_map),
                  tok(ML_W), tok(ML_W), tok(ML_W), tok(ow), tok(MLA_W),
                  pl.BlockSpec((1, ML_W), const2),
                  pl.BlockSpec((MLA_H // 2, 2 * KV_LORA, LANES), lambda i: (0, 0, 0)),
                  pl.BlockSpec((d, d), const2),
                  pl.BlockSpec((1, d), const2),
                  pl.BlockSpec((1, d), const2)],
        out_specs=pl.BlockSpec((tb, ts, d), x_map),
        out_shape=jax.ShapeDtypeStruct((b, s, d), F32),
        compiler_params=pltpu.CompilerParams(dimension_semantics=("arbitrary",),
                                             vmem_limit_bytes=VMEM_LIMIT),
        name="back_sample" if absorb else "back_prompt",
    )(x, mod, hml, mo, mz, o, az, wts["gn"], wts["wuv_pair"], wts["w_out"], wts["ln_g"], wts["ln_b"])


def _prep_weights(l, w_in, ml_b_i, ml_b_f, ml_gn, mla_q_norm, mla_kv_norm, mla_w_uq, mla_w_uk,
                  mla_w_uv, w_out, ln_g, ln_b):
    offs = np.cumsum((0, 512, 512, 512, ML_H, ML_H, 512, 512, Q_LORA, KV_LORA, MLA_ROPE, 512))
    names = ("mq", "mk", "mv", "mi", "mf", "mo", "mz", "cq", "ckv", "kr", "az")
    w = {nm: w_in[l][:, offs[i]:offs[i + 1]] for i, nm in enumerate(names)}
    d = w_in.shape[1]
    z = lambda c: jnp.zeros((d, c), F32)
    kr1, kr2 = w["kr"][:, :ROPE_HALF], w["kr"][:, ROPE_HALF:]
    tail = HEAD_PAD - MLA_NOPE - MLA_ROPE
    w_in_p = jnp.concatenate(
        [w["mq"], w["mk"], w["mv"], w["mi"], w["mf"], z(LANES - 2 * ML_H), w["mo"], w["mz"], w["cq"], w["ckv"],
         z(MLA_NOPE), kr1, kr2, z(tail), z(MLA_NOPE), -kr2, kr1, z(tail), w["az"]], axis=1).astype(BF16)
    b_gate = jnp.concatenate([ml_b_i[l], ml_b_f[l], jnp.zeros((LANES - 2 * ML_H,), F32)]).reshape(1, LANES)
    uq = mla_w_uq[l].reshape(Q_LORA, MLA_H, MLA_NOPE + MLA_ROPE)
    nope, r1, r2 = uq[..., :MLA_NOPE], uq[..., MLA_NOPE:MLA_NOPE + ROPE_HALF], uq[..., MLA_NOPE + ROPE_HALF:]
    zq = lambda c: jnp.zeros((Q_LORA, MLA_H, c), F32)
    wqa = jnp.concatenate([nope, r1, r2, zq(tail)], axis=-1).reshape(Q_LORA, MLA_H * HEAD_PAD).astype(BF16)
    wqr = jnp.concatenate([zq(MLA_NOPE), -r2, r1, zq(tail)], axis=-1).reshape(Q_LORA, MLA_H * HEAD_PAD).astype(BF16)
    uk = mla_w_uk[l].reshape(KV_LORA, MLA_H, MLA_NOPE)
    wk_pad = jnp.concatenate([uk, jnp.zeros((KV_LORA, MLA_H, HEAD_PAD - MLA_NOPE), F32)],
                             axis=-1).reshape(KV_LORA, MLA_H * HEAD_PAD).astype(BF16)
    wukt = jnp.concatenate([jnp.transpose(uk, (1, 2, 0)),
                            jnp.zeros((MLA_H, HEAD_PAD - MLA_NOPE, KV_LORA), F32)], axis=1).astype(BF16)
    uv = mla_w_uv[l].reshape(KV_LORA, MLA_H // 2, 2, MLA_VD)
    zv = jnp.zeros((KV_LORA, MLA_H // 2, MLA_VD), F32)
    wuv_pair = jnp.concatenate(
        [jnp.concatenate([uv[:, :, 0], zv], axis=-1), jnp.concatenate([zv, uv[:, :, 1]], axis=-1)],
        axis=0)
    wuv_pair = jnp.transpose(wuv_pair, (1, 0, 2)).astype(BF16)
    return dict(w_in=w_in_p, b_gate=b_gate, q_norm=mla_q_norm[l].reshape(1, -1),
                kv_norm=mla_kv_norm[l].reshape(1, -1), wqa=wqa, wqr=wqr, wk_pad=wk_pad, wukt=wukt,
                w_uv_t=mla_w_uv[l].T.astype(BF16), wuv_pair=wuv_pair, gn=ml_gn[l].reshape(1, -1),
                w_out=w_out[l].astype(BF16), ln_g=ln_g[l].reshape(1, -1), ln_b=ln_b[l].reshape(1, -1))


def _rope_tables(pos):
    f32 = np.float32
    inv = ROPE_THETA ** (-np.arange(ROPE_HALF, dtype=np.float64) / ROPE_HALF)
    ang = pos.astype(np.float64)[:, None] * inv[None, :]
    n = pos.shape[0]
    tail = HEAD_PAD - MLA_NOPE - MLA_ROPE
    cos, sin = np.cos(ang).astype(f32), np.sin(ang).astype(f32)
    cm = np.concatenate([np.ones((n, MLA_NOPE), f32), cos, cos, np.zeros((n, tail), f32)], 1)
    sm = np.concatenate([np.zeros((n, MLA_NOPE), f32), sin, sin, np.zeros((n, tail), f32)], 1)
    return cm, sm


def kernel(x_prompt, x_sample, c_prompt, c_sample, cache_ckv, cache_krope, state_C, state_n, state_m,
           page_table, w_ada, b_ada, w_in, ml_b_i, ml_b_f, ml_gn, mla_q_norm, mla_kv_norm,
           mla_w_uq, mla_w_uk, mla_w_uv, w_out, ln_g, ln_b):
    bp, sp, d = x_prompt.shape
    bs, ss, _ = x_sample.shape
    past = page_table.shape[1] * PAGE_SIZE
    cm_p, sm_p = (jnp.asarray(t) for t in _rope_tables(np.arange(sp)))
    reps = FRONT_TM // ss
    cm_s, sm_s = (jnp.asarray(np.tile(t, (reps, 1))) for t in _rope_tables(past + np.arange(ss)))
    pad = (-(bp + bs)) % 8
    c_all = jnp.concatenate([c_prompt, c_sample, jnp.zeros((pad, d), F32)], axis=0)

    xp, xs = x_prompt, x_sample
    outs = [[] for _ in range(10)]
    for l in range(DEPTH):
        wts = _prep_weights(l, w_in, ml_b_i, ml_b_f, ml_gn, mla_q_norm, mla_kv_norm, mla_w_uq, mla_w_uk,
                            mla_w_uv, w_out, ln_g, ln_b)
        mod = _ada(c_all, w_ada[l], b_ada[l])
        mod_p = mod[:bp].reshape(bp, 1, 3 * d)
        mod_s = mod[bp:bp + bs].reshape(bs, 1, 3 * d)

        mq, mk, mv, g, mo, mz, az, ckv, kr, qh, kh, vt = _front(False, xp, mod_p, cm_p, sm_p, wts)
        seq3 = lambda t: t.reshape(bp, sp, -1)
        c0 = jnp.zeros((bp, ML_H, ML_DK, ML_DV), F32)
        n0 = jnp.zeros((bp, ML_H, 1, ML_DK), F32)
        m0 = jnp.full((bp, ML_H, 1, LANES), -jnp.inf, F32)
        hml, c_p, n_p, m_p = _mlstm(seq3(mq), seq3(mk), seq3(mv), seq3(g), c0, n0, m0, ML_CHUNK, bp)
        o_p = _attn_prompt(qh, kh, vt)
        yp = _back(False, xp, mod_p, hml.reshape(bp * sp, -1), mo, mz, o_p.reshape(bp * sp, -1), az, wts)
        outs[0].append(ckv.reshape(bp, sp, KV_LORA))
        outs[1].append(kr.reshape(bp, sp, MLA_ROPE))
        outs[2].append(c_p)
        outs[3].append(n_p[:, :, 0, :])
        outs[4].append(m_p[:, :, 0, 0])

        mq, mk, mv, g, mo, mz, az, ckv, kr, qlat, qh, krp = _front(True, xs, mod_s, cm_s, sm_s, wts)
        seq3 = lambda t: t.reshape(bs, ss, -1)
        n0 = state_n[l].astype(F32).reshape(bs, ML_H, 1, ML_DK)
        m0 = jnp.broadcast_to(state_m[l].astype(F32)[:, :, None, None], (bs, ML_H, 1, LANES))
        hml, c_s, n_s, m_s = _mlstm(seq3(mq), seq3(mk), seq3(mv), seq3(g), state_C[l].astype(F32), n0, m0, ss, 8)
        o_s = _attn_sample(seq3(qlat), seq3(qh), seq3(ckv), seq3(krp), cache_ckv[l],
                           jnp.swapaxes(cache_krope[l], 1, 2), page_table)
        ys = _back(True, xs, mod_s, hml.reshape(bs * ss, -1), mo, mz,
                   o_s.reshape(bs * ss, MLA_H * KV_LORA), az, wts)
        outs[5].append(ckv.reshape(bs, ss, KV_LORA).astype(cache_ckv.dtype))
        outs[6].append(kr.reshape(bs, ss, MLA_ROPE).astype(cache_krope.dtype))
        outs[7].append(c_s.astype(state_C.dtype))
        outs[8].append(n_s[:, :, 0, :].astype(state_n.dtype))
        outs[9].append(m_s[:, :, 0, 0].astype(state_m.dtype))
        xp, xs = yp, ys
    return (xp, xs) + tuple(jnp.stack(o) for o in outs)
```

```python
import functools

import numpy as np
import jax
import jax.numpy as jnp
from jax import lax
from jax.experimental import pallas as pl
from jax.experimental.pallas import tpu as pltpu

F32 = jnp.float32
BF16 = jnp.bfloat16

D_MODEL = 1024
DEPTH = 1
PAGE_SIZE = 128
ML_W = 512
MLA_W = 512
ML_DK = 128
ML_DV = 128
ML_H = 4
MLA_VD = 64
MLA_H = 8
MLA_NOPE = 64
MLA_ROPE = 32
ROPE_HALF = MLA_ROPE // 2
MLA_SCALE = (MLA_NOPE + MLA_ROPE) ** -0.5
Q_SCALE = MLA_SCALE * float(np.log2(np.e))
Q_LORA = 384
KV_LORA = 256
ROPE_THETA = 10000.0
ML_CHUNK = 128
EPS = 1e-6
ALPHA = (2.0 * DEPTH) ** 0.25

LANES = 128
HEAD_PAD = 128
VMEM_LIMIT = 48 * 1024 * 1024

_SEG = {}
_off = 0
for _name, _w in (("mq", 512), ("mk", 512), ("mv", 512), ("gk", 256), ("mo", 512), ("mz", 512),
                  ("cq", Q_LORA), ("ckv", KV_LORA), ("az", 512)):
    _SEG[_name] = (_off, _off + _w)
    _off += _w
N_IN_PAD = _off

FRONT_TM = 256
BACK_TM = 512
ATT_TQ = 512
ATT_TK = 1024
ATT_KC = 256
MASKED_MAX_FLOOR = -1e30
DEC_PG = 64
DEC_GROUPS = 8
DEC_SLOTS = 2


def _nt(a, b):
    return lax.dot_general(a, b, (((1,), (1,)), ((), ())), preferred_element_type=F32)


def _tn(a, b):
    return lax.dot_general(a, b, (((0,), (0,)), ((), ())), preferred_element_type=F32)


def _mm(a, b):
    return jnp.dot(a, b, preferred_element_type=F32)


def _mm_exact(a, b):
    return jnp.dot(a, b, preferred_element_type=F32, precision=lax.Precision.HIGHEST)


def _nt_exact(a, b):
    return lax.dot_general(a, b, (((1,), (1,)), ((), ())), preferred_element_type=F32,
                           precision=lax.Precision.HIGHEST)


def _ada_kernel(c_ref, w_ref, b_ref, o_ref):
    o_ref[...] = _mm(c_ref[...].astype(BF16), w_ref[...].astype(BF16)) + b_ref[...]


def _ada(c_all, w_ada, b_ada):
    m = c_all.shape[0]
    tn = 512
    return pl.pallas_call(
        _ada_kernel,
        grid=(3 * D_MODEL // tn,),
        in_specs=[pl.BlockSpec((m, D_MODEL), lambda j: (0, 0)),
                  pl.BlockSpec((D_MODEL, tn), lambda j: (0, j)),
                  pl.BlockSpec((1, tn), lambda j: (0, j))],
        out_specs=pl.BlockSpec((m, tn), lambda j: (0, j)),
        out_shape=jax.ShapeDtypeStruct((m, 3 * D_MODEL), F32),
        name="ada",
    )(c_all, w_ada, b_ada.reshape(1, -1))


def _rms(x, g):
    return x * lax.rsqrt(jnp.mean(x * x, axis=-1, keepdims=True) + EPS) * g


def _log_sigmoid(x):
    return jnp.minimum(x, 0.0) - jnp.log1p(jnp.exp(-jnp.abs(x)))


def _rotary(a, tab_ref):
    cm = tab_ref[:, 0:LANES]
    s_lo = tab_ref[:, LANES:2 * LANES]
    s_hi = tab_ref[:, 2 * LANES:3 * LANES]
    return a * cm + pltpu.roll(a, LANES - ROPE_HALF, 1) * s_lo + pltpu.roll(a, ROPE_HALF, 1) * s_hi


def _front_kernel(absorb, x_ref, sh_ref, sc_ref, tab_ref, win_ref, bg_ref, qn_ref, kvn_ref,
                  wqa_ref, wk_ref, wv_ref,
                  mq_ref, mk_ref, mv_ref, g_ref, mo_ref, mz_ref, az_ref, ckv_ref, kr_ref,
                  o1_ref, o2_ref, o3_ref):
    tb, ts, d = x_ref.shape
    tm = tb * ts
    h = x_ref[...] * (1.0 + sc_ref[...]) + sh_ref[...]
    h = h.reshape(tm, d).astype(BF16)

    def seg(name):
        lo, hi = _SEG[name]
        return _mm(h, win_ref[:, lo:hi])

    mq_ref[...] = seg("mq").astype(BF16)
    mk_ref[...] = (seg("mk") * (ML_DK ** -0.5)).astype(BF16)
    mv_ref[...] = seg("mv").astype(BF16)
    gk = seg("gk")
    gz = gk[:, :LANES] + bg_ref[...]
    lane = lax.broadcasted_iota(jnp.int32, gz.shape, 1)
    g_ref[...] = jnp.where(lane < ML_H, gz, jnp.where(lane < 2 * ML_H, _log_sigmoid(gz), 0.0))
    mo_ref[...] = seg("mo").astype(BF16)
    mz_ref[...] = seg("mz").astype(BF16)
    az_ref[...] = seg("az").astype(BF16)

    ckvn = _rms(seg("ckv"), kvn_ref[...])
    ckv_ref[...] = ckvn
    ckvn_b = ckvn.astype(BF16)
    krp = _rotary(gk[:, LANES:], tab_ref)
    kr_ref[...] = krp[:, MLA_NOPE:MLA_NOPE + MLA_ROPE]

    cqn = _rms(seg("cq"), qn_ref[...]).astype(BF16)
    qa = _mm(cqn, wqa_ref[...])
    for hh in range(MLA_H):
        sl = slice(hh * HEAD_PAD, (hh + 1) * HEAD_PAD)
        qh = (_rotary(qa[:, sl], tab_ref) * Q_SCALE).astype(BF16)
        if absorb:
            o1_ref[:, hh * KV_LORA:(hh + 1) * KV_LORA] = _mm(qh, wk_ref[hh]).astype(BF16)
            o2_ref[:, sl] = qh
        else:
            o1_ref[0, hh] = qh
            o2_ref[0, hh] = (_mm(ckvn_b, wk_ref[:, sl]) + krp).astype(BF16)
    if absorb:
        o3_ref[...] = krp.astype(BF16)
    else:
        o3_ref[0] = _nt(wv_ref[...], ckvn_b).astype(BF16)


def _front(absorb, x, mod, tab, wts):
    b, s, d = x.shape
    n = b * s
    tm = FRONT_TM
    if absorb:
        tb, ts = tm // s, s
        x_map = lambda i: (i, 0, 0)
        mod_map = lambda k: (lambda i: (i, 0, k))
        tab_map = lambda i: (0, 0)
    else:
        tb, ts = 1, tm
        spb = s // tm
        x_map = lambda i: (i // spb, i % spb, 0)
        mod_map = lambda k: (lambda i: (i // spb, 0, k))
        tab_map = lambda i: (i % spb, 0)
    const2 = lambda i: (0, 0)
    tok = lambda w: pl.BlockSpec((tm, w), lambda i: (i, 0))
    in_specs = [
        pl.BlockSpec((tb, ts, d), x_map),
        pl.BlockSpec((tb, 1, d), mod_map(0)),
        pl.BlockSpec((tb, 1, d), mod_map(1)),
        pl.BlockSpec((tm, 3 * LANES), tab_map),
        pl.BlockSpec((d, N_IN_PAD), const2),
        pl.BlockSpec((1, LANES), const2),
        pl.BlockSpec((1, Q_LORA), const2),
        pl.BlockSpec((1, KV_LORA), const2),
        pl.BlockSpec((Q_LORA, MLA_H * HEAD_PAD), const2),
    ]
    out_specs = [tok(512), tok(512), tok(512), tok(LANES), tok(512), tok(512), tok(512),
                 tok(KV_LORA), tok(MLA_ROPE)]
    out_shape = [jax.ShapeDtypeStruct((n, w), dt)
                 for w, dt in ((512, BF16), (512, BF16), (512, BF16), (LANES, F32), (512, BF16), (512, BF16),
                               (512, BF16), (KV_LORA, F32), (MLA_ROPE, F32))]
    if absorb:
        in_specs += [pl.BlockSpec((MLA_H, HEAD_PAD, KV_LORA), lambda i: (0, 0, 0)),
                     pl.BlockSpec((MLA_W, KV_LORA), const2)]
        out_specs += [tok(MLA_H * KV_LORA), tok(MLA_H * HEAD_PAD), tok(HEAD_PAD)]
        out_shape += [jax.ShapeDtypeStruct((n, MLA_H * KV_LORA), BF16),
                      jax.ShapeDtypeStruct((n, MLA_H * HEAD_PAD), BF16),
                      jax.ShapeDtypeStruct((n, HEAD_PAD), BF16)]
        wk = wts["wukt"]
    else:
        head_map = lambda i: (i // spb, 0, i % spb, 0)
        in_specs += [pl.BlockSpec((KV_LORA, MLA_H * HEAD_PAD), const2),
                     pl.BlockSpec((MLA_W, KV_LORA), const2)]
        out_specs += [pl.BlockSpec((1, MLA_H, tm, HEAD_PAD), head_map),
                      pl.BlockSpec((1, MLA_H, tm, HEAD_PAD), head_map),
                      pl.BlockSpec((1, MLA_W, tm), lambda i: (i // spb, 0, i % spb))]
        out_shape += [jax.ShapeDtypeStruct((b, MLA_H, s, HEAD_PAD), BF16),
                      jax.ShapeDtypeStruct((b, MLA_H, s, HEAD_PAD), BF16),
                      jax.ShapeDtypeStruct((b, MLA_W, s), BF16)]
        wk = wts["wk_pad"]
    return pl.pallas_call(
        functools.partial(_front_kernel, absorb),
        grid=(n // tm,),
        in_specs=in_specs,
        out_specs=out_specs,
        out_shape=out_shape,
        compiler_params=pltpu.CompilerParams(dimension_semantics=("arbitrary",),
                                             vmem_limit_bytes=VMEM_LIMIT),
        name="front_sample" if absorb else "front_prompt",
    )(x, mod, mod, tab, wts["w_in"], wts["b_gate"], wts["q_norm"], wts["kv_norm"],
      wts["wqa"], wk, wts["w_uv_t"])


def _mlstm_kernel(q_ref, k_ref, v_ref, g_ref, c0_ref, n0_ref, m0_ref,
                  h_ref, c_ref, n_ref, m_ref):
    tb, L, _ = q_ref.shape
    ci = pl.program_id(1)

    @pl.when(ci == 0)
    def _():
        c_ref[...] = c0_ref[...]
        n_ref[...] = n0_ref[...]
        m_ref[...] = m0_ref[...]

    row = lax.broadcasted_iota(jnp.int32, (L, L), 0)
    col = lax.broadcasted_iota(jnp.int32, (L, L), 1)
    causal = col <= row
    tril = causal.astype(F32)
    sel = (lax.broadcasted_iota(jnp.int32, (8, LANES), 0)
           == lax.broadcasted_iota(jnp.int32, (8, LANES), 1)).astype(F32)

    chains = [(t, hh) for t in range(tb) for hh in range(ML_H)]
    sl = lambda hh: slice(hh * ML_DK, (hh + 1) * ML_DK)
    gates = {}
    for t in range(tb):
        g = g_ref[t]
        fcum = _mm_exact(tril, g)
        g_rows = _nt_exact(sel, g)
        f_rows = _nt_exact(sel, fcum)
        gates[t] = (g, fcum, g_rows, f_rows)

    qk, qc, c_prevs = {}, {}, {}
    for t, hh in chains:
        qb = q_ref[t, :, sl(hh)]
        c_prevs[t, hh] = c_ref[t, hh]
        qk[t, hh] = _nt(qb, k_ref[t, :, sl(hh)])
        qc[t, hh] = _mm(qb, c_prevs[t, hh].astype(BF16))

    stab = {}
    for t, hh in chains:
        g, fcum, g_rows, f_rows = gates[t]
        m_prev = m_ref[t, hh][:, :1]
        f_col = fcum[:, ML_H + hh:ML_H + hh + 1]
        f_row = f_rows[ML_H + hh:ML_H + hh + 1, :]
        ig_row = g_rows[hh:hh + 1, :]
        dmat = jnp.where(causal, f_col - f_row + ig_row, -jnp.inf)
        m_inter = jnp.broadcast_to(f_col, (L, LANES)) + m_prev
        m_t = jnp.maximum(m_inter, jnp.max(dmat, axis=-1, keepdims=True))
        stab[t, hh] = (jnp.exp(dmat - m_t[:, :L]), jnp.exp(m_inter - m_t), m_t)

    svs = {}
    ones_cols = jnp.ones((L, LANES), BF16)
    for t, hh in chains:
        sq = (qk[t, hh] * stab[t, hh][0]).astype(BF16)
        svs[t, hh] = _mm(sq, jnp.concatenate([v_ref[t, :, sl(hh)], ones_cols], axis=1))

    for t, hh in chains:
        _, a, m_t = stab[t, hh]
        n_rep = jnp.broadcast_to(n_ref[t, hh], (LANES, ML_DK)).astype(BF16)
        qn = _nt(q_ref[t, :, sl(hh)], n_rep)
        num = svs[t, hh][:, :ML_DV] + a * qc[t, hh]
        den = svs[t, hh][:, ML_DV:] + a * qn
        h_ref[t, :, sl(hh)] = (num / jnp.maximum(jnp.abs(den), jnp.exp(-m_t))).astype(h_ref.dtype)

    for t, hh in chains:
        w, a, m_t = stab[t, hh]
        wl_row = w[L - 1:L, :]
        al = a[L - 1:L, :]
        kb = k_ref[t, :, sl(hh)]
        kw_t = (kb.astype(F32).T * wl_row).astype(BF16)
        c_ref[t, hh] = al[:, :1] * c_prevs[t, hh] + _mm(kw_t, v_ref[t, :, sl(hh)])
        n_ref[t, hh] = al * n_ref[t, hh] + _mm(w[L - 8:L, :].astype(BF16), kb)[7:8, :]
        m_ref[t, hh] = m_t[L - 1:L, :]


def _mlstm(q, k, v, g, c0, n0, m0, chunk, tb):
    b, s, w = q.shape
    nc = s // chunk
    seq = lambda ww: pl.BlockSpec((tb, chunk, ww), lambda i, c: (i, c, 0))
    st_c = pl.BlockSpec((tb, ML_H, ML_DK, ML_DV), lambda i, c: (i, 0, 0, 0))
    st_v = pl.BlockSpec((tb, ML_H, 1, LANES), lambda i, c: (i, 0, 0, 0))
    return pl.pallas_call(
        _mlstm_kernel,
        grid=(b // tb, nc),
        in_specs=[seq(w), seq(w), seq(w), seq(LANES), st_c, st_v, st_v],
        out_specs=[seq(w), st_c, st_v, st_v],
        out_shape=[jax.ShapeDtypeStruct((b, s, w), BF16),
                   jax.ShapeDtypeStruct((b, ML_H, ML_DK, ML_DV), F32),
                   jax.ShapeDtypeStruct((b, ML_H, 1, LANES), F32),
                   jax.ShapeDtypeStruct((b, ML_H, 1, LANES), F32)],
        compiler_params=pltpu.CompilerParams(dimension_semantics=("arbitrary", "arbitrary"),
                                             vmem_limit_bytes=VMEM_LIMIT),
        name="mlstm",
    )(q, k, v, g, c0, n0, m0)


def _attn_kernel(qi_ref, ki_ref, last_ref, q_ref, k_ref, vt_ref, o_ref, m_sc, l_sc, acc_sc):
    step = pl.program_id(2)
    qi = qi_ref[step]
    ki = ki_ref[step]
    tq = q_ref.shape[2]
    tk = k_ref.shape[2]

    @pl.when(ki == 0)
    def _():
        m_sc[...] = jnp.full(m_sc.shape, -jnp.inf, F32)
        l_sc[...] = jnp.zeros(l_sc.shape, F32)
        acc_sc[...] = jnp.zeros(acc_sc.shape, F32)

    def tile(diag):
        chains = [(hh, c0) for hh in range(2) for c0 in range(0, min(tk, diag + tq), ATT_KC)]
        scores = {}
        for hh, c0 in chains:
            st = _nt(k_ref[0, hh, c0:c0 + ATT_KC, :], q_ref[0, hh])
            if c0 >= diag:
                keys = c0 - diag + lax.broadcasted_iota(jnp.int32, (ATT_KC, tq), 0)
                qrys = lax.broadcasted_iota(jnp.int32, (ATT_KC, tq), 1)
                st = jnp.where(keys <= qrys, st, -jnp.inf)
            scores[hh, c0] = st
        parts = ([], [])
        ones_rows = jnp.ones((16, ATT_KC), BF16)
        for hh, c0 in chains:
            st = scores[hh, c0]
            m_g = jnp.max(st, axis=0, keepdims=True)
            if c0 >= diag:
                m_g = jnp.maximum(m_g, MASKED_MAX_FLOOR)
            pt = jnp.exp2(st - m_g).astype(BF16)
            rows = slice(hh * MLA_VD, (hh + 1) * MLA_VD)
            v_ones = jnp.concatenate([vt_ref[0, rows, c0:c0 + ATT_KC], ones_rows], axis=0)
            o_g = _mm(v_ones, pt)
            parts[hh].append((m_g, o_g[MLA_VD:MLA_VD + 1, :], o_g[:MLA_VD, :]))
        for hh in range(2):
            rows = slice(hh * MLA_VD, (hh + 1) * MLA_VD)
            m_prev = m_sc[hh]
            m_new = functools.reduce(jnp.maximum, [m_g for m_g, _, _ in parts[hh]], m_prev)
            alpha = jnp.exp2(m_prev - m_new)
            l_new = alpha * l_sc[hh]
            acc = alpha * acc_sc[rows, :]
            for m_g, l_g, o_g in parts[hh]:
                w_g = jnp.exp2(m_g - m_new)
                l_new = l_new + w_g * l_g
                acc = acc + w_g * o_g
            m_sc[hh] = m_new
            l_sc[hh] = l_new
            acc_sc[rows, :] = acc

    first_query = qi * tq - ki * tk
    for diag in range(0, tk, tq):
        pl.when(first_query == diag)(functools.partial(tile, diag))
    pl.when(first_query >= tk)(functools.partial(tile, tk))

    @pl.when(last_ref[step] == 1)
    def _():
        out_t = jnp.concatenate([acc_sc[0:MLA_VD, :] / l_sc[0], acc_sc[MLA_VD:, :] / l_sc[1]], axis=0)
        o_ref[0] = out_t.T.astype(o_ref.dtype)


def _attn_prompt(qh, kh, vt):
    b, nh, s, _ = qh.shape
    tq, tk = ATT_TQ, ATT_TK
    assert tk % tq == 0 and tq % ATT_KC == 0 and s % tk == 0
    qi_l, ki_l, last_l = [], [], []
    for qi in range(s // tq):
        nk = ((qi + 1) * tq + tk - 1) // tk
        for ki in range(nk):
            qi_l.append(qi)
            ki_l.append(ki)
            last_l.append(int(ki == nk - 1))
    nsteps = len(qi_l)
    sched = [jnp.asarray(np.asarray(a, np.int32)) for a in (qi_l, ki_l, last_l)]
    grid_spec = pltpu.PrefetchScalarGridSpec(
        num_scalar_prefetch=3,
        grid=(b, nh // 2, nsteps),
        in_specs=[pl.BlockSpec((1, 2, tq, HEAD_PAD), lambda bi, hp, st, qi, ki, la: (bi, hp, qi[st], 0)),
                  pl.BlockSpec((1, 2, tk, HEAD_PAD), lambda bi, hp, st, qi, ki, la: (bi, hp, ki[st], 0)),
                  pl.BlockSpec((1, 2 * MLA_VD, tk), lambda bi, hp, st, qi, ki, la: (bi, hp, ki[st]))],
        out_specs=pl.BlockSpec((1, tq, LANES), lambda bi, hp, st, qi, ki, la: (bi, qi[st], hp)),
        scratch_shapes=[pltpu.VMEM((2, 1, tq), F32), pltpu.VMEM((2, 1, tq), F32),
                        pltpu.VMEM((2 * MLA_VD, tq), F32)],
    )
    return pl.pallas_call(
        _attn_kernel,
        grid_spec=grid_spec,
        out_shape=jax.ShapeDtypeStruct((b, s, nh * MLA_VD), BF16),
        compiler_params=pltpu.CompilerParams(
            dimension_semantics=("arbitrary", "arbitrary", "arbitrary"),
            vmem_limit_bytes=VMEM_LIMIT),
        name="attn_prompt",
    )(*sched, qh, kh, vt)


def _dec_page_copies(pt_ref, pool_ckv, pool_kr, cbuf, kbuf, sems, step, slot):
    copies = []
    for i in range(DEC_PG):
        page = pt_ref[step * DEC_PG + i]
        keys = pl.ds(i * PAGE_SIZE, PAGE_SIZE)
        copies.append(pltpu.make_async_copy(pool_ckv.at[page], cbuf.at[slot, keys, :], sems.at[slot, 0]))
        copies.append(pltpu.make_async_copy(pool_kr.at[page], kbuf.at[slot, :, keys], sems.at[slot, 1]))
    return copies


def _dec_wait_slot(cbuf, kbuf, sems, slot):
    pltpu.make_async_copy(cbuf.at[slot], cbuf.at[slot], sems.at[slot, 0]).wait()
    pltpu.make_async_copy(kbuf.at[slot], kbuf.at[slot], sems.at[slot, 1]).wait()


def _dec_kernel(pt_ref, qlat_ref, qh_ref, ckvn_ref, krn_ref, pool_ckv, pool_kr,
                o_ref, m_sc, l_sc, acc_sc, padc_sc, padk_sc, cbuf, kbuf, sems):
    j = pl.program_id(1)
    nj = pl.num_programs(1)
    step = pl.program_id(0) * nj + j
    last_step = pl.num_programs(0) * nj - 1
    slot = lax.rem(step, DEC_SLOTS)
    qlat = jnp.concatenate([qlat_ref[0, :, hh * KV_LORA:(hh + 1) * KV_LORA] for hh in range(MLA_H)], axis=0)
    qh = jnp.concatenate([qh_ref[0, :, hh * HEAD_PAD:(hh + 1) * HEAD_PAD] for hh in range(MLA_H)], axis=0)
    ds = ckvn_ref.shape[1]
    copies = functools.partial(_dec_page_copies, pt_ref, pool_ckv, pool_kr, cbuf, kbuf, sems)

    @pl.when(step == 0)
    def _():
        for s0 in range(DEC_SLOTS):
            for c in copies(s0, s0):
                c.start()

    _dec_wait_slot(cbuf, kbuf, sems, slot)
    next_copies = copies(step + DEC_SLOTS, slot)

    @pl.when(j == 0)
    def _():
        padc_sc[...] = jnp.zeros(padc_sc.shape, BF16)
        padk_sc[...] = jnp.zeros(padk_sc.shape, BF16)
        padc_sc[0:ds, :] = ckvn_ref[0].astype(BF16)
        padk_sc[0:ds, :] = krn_ref[0]
        cn = padc_sc[...]
        s = _nt(qlat, cn) + _nt(qh, padk_sc[...])
        tok = lax.broadcasted_iota(jnp.int32, (MLA_H, ds, s.shape[1]), 1).reshape(s.shape)
        key = lax.broadcasted_iota(jnp.int32, s.shape, 1)
        s = jnp.where(key <= tok, s, -jnp.inf)
        m = jnp.max(s, axis=-1, keepdims=True)
        p = jnp.exp2(s - m)
        m_sc[...] = m
        l_sc[...] = jnp.sum(p, axis=-1, keepdims=True)
        acc_sc[...] = _mm(p.astype(BF16), cn)

    qr = qh[:, MLA_NOPE:MLA_NOPE + MLA_ROPE]
    pages = cbuf[slot].astype(BF16)
    krt = kbuf[slot].astype(BF16)
    gk = pages.shape[0] // DEC_GROUPS
    groups = [pages[g * gk:(g + 1) * gk] for g in range(DEC_GROUPS)]
    scores, probs, parts = {}, {}, []

    def score(g):
        scores[g] = _nt(qlat, groups[g]) + _mm(qr, krt[:, g * gk:(g + 1) * gk])

    def soft(g):
        m_g = jnp.max(scores[g], axis=-1, keepdims=True)
        p = jnp.exp2(scores[g] - m_g)
        probs[g] = (m_g, jnp.sum(p, axis=-1, keepdims=True), p.astype(BF16))

    def value(g):
        m_g, l_g, pb = probs[g]
        parts.append((m_g, l_g, _mm(pb, groups[g])))

    per_group = len(next_copies) // DEC_GROUPS
    for g in range(DEC_GROUPS):
        score(g)
    soft(0)
    for g in range(DEC_GROUPS):
        if g + 1 < DEC_GROUPS:
            soft(g + 1)
        value(g)
        for c in next_copies[g * per_group:(g + 1) * per_group]:
            c.start()
    m_prev = m_sc[...]
    m_new = functools.reduce(jnp.maximum, [m_g for m_g, _, _ in parts], m_prev)
    alpha = jnp.exp2(m_prev - m_new)
    l_new = alpha * l_sc[...]
    acc = alpha * acc_sc[...]
    for m_g, l_g, o_g in parts:
        w_g = jnp.exp2(m_g - m_new)
        l_new = l_new + w_g * l_g
        acc = acc + w_g * o_g
    m_sc[...] = m_new
    l_sc[...] = l_new
    acc_sc[...] = acc

    @pl.when(j == nj - 1)
    def _():
        out = acc_sc[...] / l_sc[...]
        for hh in range(MLA_H):
            o_ref[0, :, hh * KV_LORA:(hh + 1) * KV_LORA] = out[hh * ds:(hh + 1) * ds, :]

    @pl.when(step == last_step)
    def _():
        for s0 in range(DEC_SLOTS):
            _dec_wait_slot(cbuf, kbuf, sems, s0)


def _attn_sample(qlat, qh, ckvn, krn, pool_ckv, pool_kr, page_table):
    b, ds, _ = qlat.shape
    nq = ds * MLA_H
    n_pages = page_table.shape[1]
    pg = DEC_PG
    pt = page_table.reshape(-1)
    pt = jnp.concatenate([pt, pt[:DEC_SLOTS * pg]])

    fixed = lambda w, r: pl.BlockSpec((1, r, w), lambda bi, j, pt_ref: (bi, 0, 0))
    grid_spec = pltpu.PrefetchScalarGridSpec(
        num_scalar_prefetch=1,
        grid=(b, n_pages // pg),
        in_specs=[fixed(MLA_H * KV_LORA, ds), fixed(MLA_H * HEAD_PAD, ds), fixed(KV_LORA, ds), fixed(HEAD_PAD, ds),
                  pl.BlockSpec(memory_space=pl.ANY), pl.BlockSpec(memory_space=pl.ANY)],
        out_specs=fixed(MLA_H * KV_LORA, ds),
        scratch_shapes=[pltpu.VMEM((nq, 1), F32), pltpu.VMEM((nq, 1), F32),
                        pltpu.VMEM((nq, KV_LORA), F32),
                        pltpu.VMEM((PAGE_SIZE, KV_LORA), BF16), pltpu.VMEM((PAGE_SIZE, HEAD_PAD), BF16),
                        pltpu.VMEM((DEC_SLOTS, pg * PAGE_SIZE, KV_LORA), F32),
                        pltpu.VMEM((DEC_SLOTS, MLA_ROPE, pg * PAGE_SIZE), F32),
                        pltpu.SemaphoreType.DMA((DEC_SLOTS, 2))],
    )
    return pl.pallas_call(
        _dec_kernel,
        grid_spec=grid_spec,
        out_shape=jax.ShapeDtypeStruct((b, ds, MLA_H * KV_LORA), F32),
        compiler_params=pltpu.CompilerParams(dimension_semantics=("arbitrary", "arbitrary"),
                                             vmem_limit_bytes=VMEM_LIMIT),
        name="attn_sample",
    )(pt, qlat, qh, ckvn, krn, pool_ckv, pool_kr)


def _sigmoid(x):
    return 0.5 * jnp.tanh(0.5 * x) + 0.5


def _silu(x):
    h = 0.5 * x
    return h * jnp.tanh(h) + h


def _back_kernel(absorb, x_ref, gate_ref, hml_ref, mo_ref, mz_ref, o_ref, az_ref, gn_ref, wuv_ref,
                 wout_ref, lng_ref, lnb_ref, y_ref):
    tb, ts, d = x_ref.shape
    tm = tb * ts
    hm = hml_ref[...].astype(F32) * _sigmoid(mo_ref[...].astype(F32))
    parts = []
    for hh in range(ML_H):
        t = hm[:, hh * ML_DV:(hh + 1) * ML_DV]
        mu = jnp.mean(t, axis=-1, keepdims=True)
        tc = t - mu
        var = jnp.mean(tc * tc, axis=-1, keepdims=True)
        parts.append(tc * lax.rsqrt(var + EPS))
    mz = mz_ref[...].astype(F32)
    y_ml = jnp.concatenate(parts, axis=1) * gn_ref[...] * _silu(mz)
    if absorb:
        olat = o_ref[...].astype(BF16)
        o_mla = jnp.concatenate(
            [_mm(olat[:, p * 2 * KV_LORA:(p + 1) * 2 * KV_LORA], wuv_ref[p]) for p in range(MLA_H // 2)],
            axis=1)
    else:
        o_mla = o_ref[...].astype(F32)
    az = az_ref[...].astype(F32)
    y_mla = o_mla * _silu(az)
    out = _mm(y_ml.astype(BF16), wout_ref[0:ML_W, :]) + _mm(y_mla.astype(BF16), wout_ref[ML_W:, :])
    z = ALPHA * x_ref[...] + gate_ref[...] * out.reshape(tb, ts, d)
    mu = jnp.mean(z, axis=-1, keepdims=True)
    zc = z - mu
    var = jnp.mean(zc * zc, axis=-1, keepdims=True)
    y_ref[...] = zc * lax.rsqrt(var + EPS) * lng_ref[...] + lnb_ref[...]


def _back(absorb, x, mod, hml, mo, mz, o, az, wts):
    b, s, d = x.shape
    n = b * s
    tm = BACK_TM
    if absorb:
        tb, ts = tm // s, s
        x_map = lambda i: (i, 0, 0)
        gate_map = lambda i: (i, 0, 2)
    else:
        tb, ts = 1, tm
        spb = s // tm
        x_map = lambda i: (i // spb, i % spb, 0)
        gate_map = lambda i: (i // spb, 0, 2)
    const2 = lambda i: (0, 0)
    tok = lambda w: pl.BlockSpec((tm, w), lambda i: (i, 0))
    ow = o.shape[1]
    return pl.pallas_call(
        functools.partial(_back_kernel, absorb),
        grid=(n // tm,),
        in_specs=[pl.BlockSpec((tb, ts, d), x_map),
                  pl.BlockSpec((tb, 1, d), gate_map),
                  tok(ML_W), tok(ML_W), tok(ML_W), tok(ow), tok(MLA_W),
                  pl.BlockSpec((1, ML_W), const2),
                  pl.BlockSpec((MLA_H // 2, 2 * KV_LORA, LANES), lambda i: (0, 0, 0)),
                  pl.BlockSpec((d, d), const2),
                  pl.BlockSpec((1, d), const2),
                  pl.BlockSpec((1, d), const2)],
        out_specs=pl.BlockSpec((tb, ts, d), x_map),
        out_shape=jax.ShapeDtypeStruct((b, s, d), F32),
        compiler_params=pltpu.CompilerParams(dimension_semantics=("arbitrary",),
                                             vmem_limit_bytes=VMEM_LIMIT),
        name="back_sample" if absorb else "back_prompt",
    )(x, mod, hml, mo, mz, o, az, wts["gn"], wts["wuv_pair"], wts["w_out"], wts["ln_g"], wts["ln_b"])


def _prep_weights(l, w_in, ml_b_i, ml_b_f, ml_gn, mla_q_norm, mla_kv_norm, mla_w_uq, mla_w_uk,
                  mla_w_uv, w_out, ln_g, ln_b):
    offs = np.cumsum((0, 512, 512, 512, ML_H, ML_H, 512, 512, Q_LORA, KV_LORA, MLA_ROPE, 512))
    names = ("mq", "mk", "mv", "mi", "mf", "mo", "mz", "cq", "ckv", "kr", "az")
    w = {nm: w_in[l][:, offs[i]:offs[i + 1]] for i, nm in enumerate(names)}
    d = w_in.shape[1]
    z = lambda c: jnp.zeros((d, c), F32)
    kr1, kr2 = w["kr"][:, :ROPE_HALF], w["kr"][:, ROPE_HALF:]
    tail = HEAD_PAD - MLA_NOPE - MLA_ROPE
    w_in_p = jnp.concatenate(
        [w["mq"], w["mk"], w["mv"], w["mi"], w["mf"], z(LANES - 2 * ML_H), z(MLA_NOPE), kr1, kr2, z(tail),
         w["mo"], w["mz"], w["cq"], w["ckv"], w["az"]], axis=1).astype(BF16)
    b_gate = jnp.concatenate([ml_b_i[l], ml_b_f[l], jnp.zeros((LANES - 2 * ML_H,), F32)]).reshape(1, LANES)
    uq = mla_w_uq[l].reshape(Q_LORA, MLA_H, MLA_NOPE + MLA_ROPE)
    nope, r1, r2 = uq[..., :MLA_NOPE], uq[..., MLA_NOPE:MLA_NOPE + ROPE_HALF], uq[..., MLA_NOPE + ROPE_HALF:]
    zq = lambda c: jnp.zeros((Q_LORA, MLA_H, c), F32)
    wqa = jnp.concatenate([nope, r1, r2, zq(tail)], axis=-1).reshape(Q_LORA, MLA_H * HEAD_PAD).astype(BF16)
    uk = mla_w_uk[l].reshape(KV_LORA, MLA_H, MLA_NOPE)
    wk_pad = jnp.concatenate([uk, jnp.zeros((KV_LORA, MLA_H, HEAD_PAD - MLA_NOPE), F32)],
                             axis=-1).reshape(KV_LORA, MLA_H * HEAD_PAD).astype(BF16)
    wukt = jnp.concatenate([jnp.transpose(uk, (1, 2, 0)),
                            jnp.zeros((MLA_H, HEAD_PAD - MLA_NOPE, KV_LORA), F32)], axis=1).astype(BF16)
    uv = mla_w_uv[l].reshape(KV_LORA, MLA_H // 2, 2, MLA_VD)
    zv = jnp.zeros((KV_LORA, MLA_H // 2, MLA_VD), F32)
    wuv_pair = jnp.concatenate(
        [jnp.concatenate([uv[:, :, 0], zv], axis=-1), jnp.concatenate([zv, uv[:, :, 1]], axis=-1)],
        axis=0)
    wuv_pair = jnp.transpose(wuv_pair, (1, 0, 2)).astype(BF16)
    return dict(w_in=w_in_p, b_gate=b_gate, q_norm=mla_q_norm[l].reshape(1, -1),
                kv_norm=mla_kv_norm[l].reshape(1, -1), wqa=wqa, wk_pad=wk_pad, wukt=wukt,
                w_uv_t=mla_w_uv[l].T.astype(BF16), wuv_pair=wuv_pair, gn=ml_gn[l].reshape(1, -1),
                w_out=w_out[l].astype(BF16), ln_g=ln_g[l].reshape(1, -1), ln_b=ln_b[l].reshape(1, -1))


def _rope_tables(pos):
    f32 = np.float32
    inv = ROPE_THETA ** (-np.arange(ROPE_HALF, dtype=np.float64) / ROPE_HALF)
    ang = pos.astype(np.float64)[:, None] * inv[None, :]
    n = pos.shape[0]
    tail = HEAD_PAD - MLA_NOPE - MLA_ROPE
    cos, sin = np.cos(ang).astype(f32), np.sin(ang).astype(f32)
    zeros = lambda c: np.zeros((n, c), f32)
    cm = np.concatenate([np.ones((n, MLA_NOPE), f32), cos, cos, zeros(tail)], 1)
    s_lo = np.concatenate([zeros(MLA_NOPE), -sin, zeros(ROPE_HALF), zeros(tail)], 1)
    s_hi = np.concatenate([zeros(MLA_NOPE), zeros(ROPE_HALF), sin, zeros(tail)], 1)
    return np.concatenate([cm, s_lo, s_hi], 1)


def kernel(x_prompt, x_sample, c_prompt, c_sample, cache_ckv, cache_krope, state_C, state_n, state_m,
           page_table, w_ada, b_ada, w_in, ml_b_i, ml_b_f, ml_gn, mla_q_norm, mla_kv_norm,
           mla_w_uq, mla_w_uk, mla_w_uv, w_out, ln_g, ln_b):
    bp, sp, d = x_prompt.shape
    bs, ss, _ = x_sample.shape
    past = page_table.shape[1] * PAGE_SIZE
    tab_p = jnp.asarray(_rope_tables(np.arange(sp)))
    tab_s = jnp.asarray(np.tile(_rope_tables(past + np.arange(ss)), (FRONT_TM // ss, 1)))
    pad = (-(bp + bs)) % 8
    c_all = jnp.concatenate([c_prompt, c_sample, jnp.zeros((pad, d), F32)], axis=0)

    xp, xs = x_prompt, x_sample
    outs = [[] for _ in range(10)]
    for l in range(DEPTH):
        wts = _prep_weights(l, w_in, ml_b_i, ml_b_f, ml_gn, mla_q_norm, mla_kv_norm, mla_w_uq, mla_w_uk,
                            mla_w_uv, w_out, ln_g, ln_b)
        mod = _ada(c_all, w_ada[l], b_ada[l])
        mod_p = mod[:bp].reshape(bp, 1, 3 * d)
        mod_s = mod[bp:bp + bs].reshape(bs, 1, 3 * d)

        mq, mk, mv, g, mo, mz, az, ckv, kr, qh, kh, vt = _front(False, xp, mod_p, tab_p, wts)
        seq3 = lambda t: t.reshape(bp, sp, -1)
        c0 = jnp.zeros((bp, ML_H, ML_DK, ML_DV), F32)
        n0 = jnp.zeros((bp, ML_H, 1, ML_DK), F32)
        m0 = jnp.full((bp, ML_H, 1, LANES), -jnp.inf, F32)
        hml, c_p, n_p, m_p = _mlstm(seq3(mq), seq3(mk), seq3(mv), seq3(g), c0, n0, m0, ML_CHUNK, bp)
        o_p = _attn_prompt(qh, kh, vt)
        yp = _back(False, xp, mod_p, hml.reshape(bp * sp, -1), mo, mz, o_p.reshape(bp * sp, -1), az, wts)
        outs[0].append(ckv.reshape(bp, sp, KV_LORA))
        outs[1].append(kr.reshape(bp, sp, MLA_ROPE))
        outs[2].append(c_p)
        outs[3].append(n_p[:, :, 0, :])
        outs[4].append(m_p[:, :, 0, 0])

        mq, mk, mv, g, mo, mz, az, ckv, kr, qlat, qh, krp = _front(True, xs, mod_s, tab_s, wts)
        seq3 = lambda t: t.reshape(bs, ss, -1)
        n0 = state_n[l].astype(F32).reshape(bs, ML_H, 1, ML_DK)
        m0 = jnp.broadcast_to(state_m[l].astype(F32)[:, :, None, None], (bs, ML_H, 1, LANES))
        hml, c_s, n_s, m_s = _mlstm(seq3(mq), seq3(mk), seq3(mv), seq3(g), state_C[l].astype(F32), n0, m0, ss, 8)
        o_s = _attn_sample(seq3(qlat), seq3(qh), seq3(ckv), seq3(krp), cache_ckv[l],
                           jnp.swapaxes(cache_krope[l], 1, 2), page_table)
        ys = _back(True, xs, mod_s, hml.reshape(bs * ss, -1), mo, mz,
                   o_s.reshape(bs * ss, MLA_H * KV_LORA), az, wts)
        outs[5].append(ckv.reshape(bs, ss, KV_LORA).astype(cache_ckv.dtype))
        outs[6].append(kr.reshape(bs, ss, MLA_ROPE).astype(cache_krope.dtype))
        outs[7].append(c_s.astype(state_C.dtype))
        outs[8].append(n_s[:, :, 0, :].astype(state_n.dtype))
        outs[9].append(m_s[:, :, 0, 0].astype(state_m.dtype))
        xp, xs = yp, ys
    return (xp, xs) + tuple(jnp.stack(o) for o in outs)
```

```python
import functools

import numpy as np
import jax
import jax.numpy as jnp
from jax import lax
from jax.experimental import pallas as pl
from jax.experimental.pallas import tpu as pltpu

F32 = jnp.float32
BF16 = jnp.bfloat16

D_MODEL = 1024
DEPTH = 1
PAGE_SIZE = 128
ML_W = 512
MLA_W = 512
ML_DK = 128
ML_DV = 128
ML_H = 4
MLA_VD = 64
MLA_H = 8
MLA_NOPE = 64
MLA_ROPE = 32
ROPE_HALF = MLA_ROPE // 2
MLA_SCALE = (MLA_NOPE + MLA_ROPE) ** -0.5
Q_SCALE = MLA_SCALE * float(np.log2(np.e))
Q_LORA = 384
KV_LORA = 256
ROPE_THETA = 10000.0
ML_CHUNK = 128
EPS = 1e-6
ALPHA = (2.0 * DEPTH) ** 0.25

LANES = 128
HEAD_PAD = 128
VMEM_LIMIT = 48 * 1024 * 1024

_SEG = {}
_off = 0
for _name, _w in (("mq", 512), ("mk", 512), ("mv", 512), ("gk", 256), ("mo", 512), ("mz", 512),
                  ("cq", Q_LORA), ("ckv", KV_LORA), ("az", 512)):
    _SEG[_name] = (_off, _off + _w)
    _off += _w
N_IN_PAD = _off

FRONT_TM = 512
BACK_TM = 512
ATT_TQ = 512
ATT_TK = 2048
ATT_KC = 256
MASKED_MAX_FLOOR = -1e30
DEC_PG = 64
DEC_GROUPS = 8
DEC_SLOTS = 2


def _nt(a, b):
    return lax.dot_general(a, b, (((1,), (1,)), ((), ())), preferred_element_type=F32)


def _tn(a, b):
    return lax.dot_general(a, b, (((0,), (0,)), ((), ())), preferred_element_type=F32)


def _mm(a, b):
    return jnp.dot(a, b, preferred_element_type=F32)


def _mm_exact(a, b):
    return jnp.dot(a, b, preferred_element_type=F32, precision=lax.Precision.HIGHEST)


def _nt_exact(a, b):
    return lax.dot_general(a, b, (((1,), (1,)), ((), ())), preferred_element_type=F32,
                           precision=lax.Precision.HIGHEST)


def _ada_kernel(c_ref, w_ref, b_ref, o_ref):
    o_ref[...] = _mm(c_ref[...].astype(BF16), w_ref[...].astype(BF16)) + b_ref[...]


def _ada(c_all, w_ada, b_ada):
    m = c_all.shape[0]
    tn = 512
    return pl.pallas_call(
        _ada_kernel,
        grid=(3 * D_MODEL // tn,),
        in_specs=[pl.BlockSpec((m, D_MODEL), lambda j: (0, 0)),
                  pl.BlockSpec((D_MODEL, tn), lambda j: (0, j)),
                  pl.BlockSpec((1, tn), lambda j: (0, j))],
        out_specs=pl.BlockSpec((m, tn), lambda j: (0, j)),
        out_shape=jax.ShapeDtypeStruct((m, 3 * D_MODEL), F32),
        name="ada",
    )(c_all, w_ada, b_ada.reshape(1, -1))


def _rms(x, g):
    return x * lax.rsqrt(jnp.mean(x * x, axis=-1, keepdims=True) + EPS) * g


def _log_sigmoid(x):
    return jnp.minimum(x, 0.0) - jnp.log1p(jnp.exp(-jnp.abs(x)))


def _rotary(a, tab_ref):
    cm = tab_ref[:, 0:LANES]
    s_lo = tab_ref[:, LANES:2 * LANES]
    s_hi = tab_ref[:, 2 * LANES:3 * LANES]
    return a * cm + pltpu.roll(a, LANES - ROPE_HALF, 1) * s_lo + pltpu.roll(a, ROPE_HALF, 1) * s_hi


def _front_kernel(absorb, x_ref, sh_ref, sc_ref, tab_ref, win_ref, bg_ref, qn_ref, kvn_ref,
                  wqa_ref, wk_ref, wv_ref,
                  mq_ref, mk_ref, mv_ref, g_ref, mo_ref, mz_ref, az_ref, ckv_ref, kr_ref,
                  o1_ref, o2_ref, o3_ref):
    tb, ts, d = x_ref.shape
    tm = tb * ts
    h = x_ref[...] * (1.0 + sc_ref[...]) + sh_ref[...]
    h = h.reshape(tm, d).astype(BF16)

    def seg(name):
        lo, hi = _SEG[name]
        return _mm(h, win_ref[:, lo:hi])

    mq_ref[...] = seg("mq").astype(BF16)
    mk_ref[...] = (seg("mk") * (ML_DK ** -0.5)).astype(BF16)
    mv_ref[...] = seg("mv").astype(BF16)
    gk = seg("gk")
    gz = gk[:, :LANES] + bg_ref[...]
    lane = lax.broadcasted_iota(jnp.int32, gz.shape, 1)
    g_ref[...] = jnp.where(lane < ML_H, gz, jnp.where(lane < 2 * ML_H, _log_sigmoid(gz), 0.0))
    mo_ref[...] = seg("mo").astype(BF16)
    mz_ref[...] = seg("mz").astype(BF16)
    az_ref[...] = seg("az").astype(BF16)

    ckvn = _rms(seg("ckv"), kvn_ref[...])
    ckv_ref[...] = ckvn
    ckvn_b = ckvn.astype(BF16)
    krp = _rotary(gk[:, LANES:], tab_ref)
    kr_ref[...] = krp[:, MLA_NOPE:MLA_NOPE + MLA_ROPE]

    cqn = _rms(seg("cq"), qn_ref[...]).astype(BF16)
    qa = _mm(cqn, wqa_ref[...])
    for hh in range(MLA_H):
        sl = slice(hh * HEAD_PAD, (hh + 1) * HEAD_PAD)
        qh = (_rotary(qa[:, sl], tab_ref) * Q_SCALE).astype(BF16)
        if absorb:
            o1_ref[:, hh * KV_LORA:(hh + 1) * KV_LORA] = _mm(qh, wk_ref[hh]).astype(BF16)
            o2_ref[:, sl] = qh
        else:
            o1_ref[0, hh] = qh
            o2_ref[0, hh] = (_mm(ckvn_b, wk_ref[:, sl]) + krp).astype(BF16)
    if absorb:
        o3_ref[...] = krp.astype(BF16)
    else:
        o3_ref[0] = _nt(wv_ref[...], ckvn_b).astype(BF16)


def _front(absorb, x, mod, tab, wts):
    b, s, d = x.shape
    n = b * s
    tm = FRONT_TM
    if absorb:
        tb, ts = tm // s, s
        x_map = lambda i: (i, 0, 0)
        mod_map = lambda k: (lambda i: (i, 0, k))
        tab_map = lambda i: (0, 0)
    else:
        tb, ts = 1, tm
        spb = s // tm
        x_map = lambda i: (i // spb, i % spb, 0)
        mod_map = lambda k: (lambda i: (i // spb, 0, k))
        tab_map = lambda i: (i % spb, 0)
    const2 = lambda i: (0, 0)
    tok = lambda w: pl.BlockSpec((tm, w), lambda i: (i, 0))
    in_specs = [
        pl.BlockSpec((tb, ts, d), x_map),
        pl.BlockSpec((tb, 1, d), mod_map(0)),
        pl.BlockSpec((tb, 1, d), mod_map(1)),
        pl.BlockSpec((tm, 3 * LANES), tab_map),
        pl.BlockSpec((d, N_IN_PAD), const2),
        pl.BlockSpec((1, LANES), const2),
        pl.BlockSpec((1, Q_LORA), const2),
        pl.BlockSpec((1, KV_LORA), const2),
        pl.BlockSpec((Q_LORA, MLA_H * HEAD_PAD), const2),
    ]
    out_specs = [tok(512), tok(512), tok(512), tok(LANES), tok(512), tok(512), tok(512),
                 tok(KV_LORA), tok(MLA_ROPE)]
    out_shape = [jax.ShapeDtypeStruct((n, w), dt)
                 for w, dt in ((512, BF16), (512, BF16), (512, BF16), (LANES, F32), (512, BF16), (512, BF16),
                               (512, BF16), (KV_LORA, F32), (MLA_ROPE, F32))]
    if absorb:
        in_specs += [pl.BlockSpec((MLA_H, HEAD_PAD, KV_LORA), lambda i: (0, 0, 0)),
                     pl.BlockSpec((MLA_W, KV_LORA), const2)]
        out_specs += [tok(MLA_H * KV_LORA), tok(MLA_H * HEAD_PAD), tok(HEAD_PAD)]
        out_shape += [jax.ShapeDtypeStruct((n, MLA_H * KV_LORA), BF16),
                      jax.ShapeDtypeStruct((n, MLA_H * HEAD_PAD), BF16),
                      jax.ShapeDtypeStruct((n, HEAD_PAD), BF16)]
        wk = wts["wukt"]
    else:
        head_map = lambda i: (i // spb, 0, i % spb, 0)
        in_specs += [pl.BlockSpec((KV_LORA, MLA_H * HEAD_PAD), const2),
                     pl.BlockSpec((MLA_W, KV_LORA), const2)]
        out_specs += [pl.BlockSpec((1, MLA_H, tm, HEAD_PAD), head_map),
                      pl.BlockSpec((1, MLA_H, tm, HEAD_PAD), head_map),
                      pl.BlockSpec((1, MLA_W, tm), lambda i: (i // spb, 0, i % spb))]
        out_shape += [jax.ShapeDtypeStruct((b, MLA_H, s, HEAD_PAD), BF16),
                      jax.ShapeDtypeStruct((b, MLA_H, s, HEAD_PAD), BF16),
                      jax.ShapeDtypeStruct((b, MLA_W, s), BF16)]
        wk = wts["wk_pad"]
    return pl.pallas_call(
        functools.partial(_front_kernel, absorb),
        grid=(n // tm,),
        in_specs=in_specs,
        out_specs=out_specs,
        out_shape=out_shape,
        compiler_params=pltpu.CompilerParams(dimension_semantics=("arbitrary",),
                                             vmem_limit_bytes=VMEM_LIMIT),
        name="front_sample" if absorb else "front_prompt",
    )(x, mod, mod, tab, wts["w_in"], wts["b_gate"], wts["q_norm"], wts["kv_norm"],
      wts["wqa"], wk, wts["w_uv_t"])


def _mlstm_kernel(q_ref, k_ref, v_ref, g_ref, c0_ref, n0_ref, m0_ref,
                  h_ref, c_ref, n_ref, m_ref):
    tb, L, _ = q_ref.shape
    ci = pl.program_id(1)

    @pl.when(ci == 0)
    def _():
        c_ref[...] = c0_ref[...]
        n_ref[...] = n0_ref[...]
        m_ref[...] = m0_ref[...]

    row = lax.broadcasted_iota(jnp.int32, (L, L), 0)
    col = lax.broadcasted_iota(jnp.int32, (L, L), 1)
    causal = col <= row
    tril = causal.astype(F32)
    sel = (lax.broadcasted_iota(jnp.int32, (8, LANES), 0)
           == lax.broadcasted_iota(jnp.int32, (8, LANES), 1)).astype(F32)

    chains = [(t, hh) for t in range(tb) for hh in range(ML_H)]
    sl = lambda hh: slice(hh * ML_DK, (hh + 1) * ML_DK)
    gates = {}
    for t in range(tb):
        g = g_ref[t]
        fcum = _mm_exact(tril, g)
        g_rows = _nt_exact(sel, g)
        f_rows = _nt_exact(sel, fcum)
        gates[t] = (g, fcum, g_rows, f_rows)

    qk, qc, c_prevs = {}, {}, {}
    for t, hh in chains:
        qb = q_ref[t, :, sl(hh)]
        c_prevs[t, hh] = c_ref[t, hh]
        qk[t, hh] = _nt(qb, k_ref[t, :, sl(hh)])
        qc[t, hh] = _mm(qb, c_prevs[t, hh].astype(BF16))

    stab = {}
    for t, hh in chains:
        g, fcum, g_rows, f_rows = gates[t]
        m_prev = m_ref[t, hh][:, :1]
        f_col = fcum[:, ML_H + hh:ML_H + hh + 1]
        f_row = f_rows[ML_H + hh:ML_H + hh + 1, :]
        ig_row = g_rows[hh:hh + 1, :]
        dmat = jnp.where(causal, f_col - f_row + ig_row, -jnp.inf)
        m_inter = jnp.broadcast_to(f_col, (L, LANES)) + m_prev
        m_t = jnp.maximum(m_inter, jnp.max(dmat, axis=-1, keepdims=True))
        stab[t, hh] = (jnp.exp(dmat - m_t[:, :L]), jnp.exp(m_inter - m_t), m_t)

    svs = {}
    ones_cols = jnp.ones((L, LANES), BF16)
    for t, hh in chains:
        sq = (qk[t, hh] * stab[t, hh][0]).astype(BF16)
        svs[t, hh] = _mm(sq, jnp.concatenate([v_ref[t, :, sl(hh)], ones_cols], axis=1))

    for t, hh in chains:
        _, a, m_t = stab[t, hh]
        n_rep = jnp.broadcast_to(n_ref[t, hh], (LANES, ML_DK)).astype(BF16)
        qn = _nt(q_ref[t, :, sl(hh)], n_rep)
        num = svs[t, hh][:, :ML_DV] + a * qc[t, hh]
        den = svs[t, hh][:, ML_DV:] + a * qn
        h_ref[t, :, sl(hh)] = (num / jnp.maximum(jnp.abs(den), jnp.exp(-m_t))).astype(h_ref.dtype)

    for t, hh in chains:
        w, a, m_t = stab[t, hh]
        wl_row = w[L - 1:L, :]
        al = a[L - 1:L, :]
        kb = k_ref[t, :, sl(hh)]
        kw_t = (kb.astype(F32).T * wl_row).astype(BF16)
        c_ref[t, hh] = al[:, :1] * c_prevs[t, hh] + _mm(kw_t, v_ref[t, :, sl(hh)])
        n_ref[t, hh] = al * n_ref[t, hh] + _mm(w[L - 8:L, :].astype(BF16), kb)[7:8, :]
        m_ref[t, hh] = m_t[L - 1:L, :]


def _mlstm(q, k, v, g, c0, n0, m0, chunk, tb):
    b, s, w = q.shape
    nc = s // chunk
    seq = lambda ww: pl.BlockSpec((tb, chunk, ww), lambda i, c: (i, c, 0))
    st_c = pl.BlockSpec((tb, ML_H, ML_DK, ML_DV), lambda i, c: (i, 0, 0, 0))
    st_v = pl.BlockSpec((tb, ML_H, 1, LANES), lambda i, c: (i, 0, 0, 0))
    return pl.pallas_call(
        _mlstm_kernel,
        grid=(b // tb, nc),
        in_specs=[seq(w), seq(w), seq(w), seq(LANES), st_c, st_v, st_v],
        out_specs=[seq(w), st_c, st_v, st_v],
        out_shape=[jax.ShapeDtypeStruct((b, s, w), BF16),
                   jax.ShapeDtypeStruct((b, ML_H, ML_DK, ML_DV), F32),
                   jax.ShapeDtypeStruct((b, ML_H, 1, LANES), F32),
                   jax.ShapeDtypeStruct((b, ML_H, 1, LANES), F32)],
        compiler_params=pltpu.CompilerParams(dimension_semantics=("arbitrary", "arbitrary"),
                                             vmem_limit_bytes=VMEM_LIMIT),
        name="mlstm",
    )(q, k, v, g, c0, n0, m0)


def _attn_kernel(qi_ref, ki_ref, last_ref, q_ref, k_ref, vt_ref, o_ref, m_sc, l_sc, acc_sc):
    step = pl.program_id(2)
    qi = qi_ref[step]
    ki = ki_ref[step]
    tq = q_ref.shape[2]
    tk = k_ref.shape[2]

    @pl.when(ki == 0)
    def _():
        m_sc[...] = jnp.full(m_sc.shape, -jnp.inf, F32)
        l_sc[...] = jnp.zeros(l_sc.shape, F32)
        acc_sc[...] = jnp.zeros(acc_sc.shape, F32)

    def tile(diag):
        chains = [(hh, c0) for hh in range(2) for c0 in range(0, min(tk, diag + tq), ATT_KC)]
        scores = {}
        for hh, c0 in chains:
            st = _nt(k_ref[0, hh, c0:c0 + ATT_KC, :], q_ref[0, hh])
            if c0 >= diag:
                keys = c0 - diag + lax.broadcasted_iota(jnp.int32, (ATT_KC, tq), 0)
                qrys = lax.broadcasted_iota(jnp.int32, (ATT_KC, tq), 1)
                st = jnp.where(keys <= qrys, st, -jnp.inf)
            scores[hh, c0] = st
        parts = ([], [])
        ones_rows = jnp.ones((16, ATT_KC), BF16)
        for hh, c0 in chains:
            st = scores[hh, c0]
            m_g = jnp.max(st, axis=0, keepdims=True)
            if c0 >= diag:
                m_g = jnp.maximum(m_g, MASKED_MAX_FLOOR)
            pt = jnp.exp2(st - m_g).astype(BF16)
            rows = slice(hh * MLA_VD, (hh + 1) * MLA_VD)
            v_ones = jnp.concatenate([vt_ref[0, rows, c0:c0 + ATT_KC], ones_rows], axis=0)
            o_g = _mm(v_ones, pt)
            parts[hh].append((m_g, o_g[MLA_VD:MLA_VD + 1, :], o_g[:MLA_VD, :]))
        for hh in range(2):
            rows = slice(hh * MLA_VD, (hh + 1) * MLA_VD)
            m_prev = m_sc[hh]
            m_new = functools.reduce(jnp.maximum, [m_g for m_g, _, _ in parts[hh]], m_prev)
            alpha = jnp.exp2(m_prev - m_new)
            l_new = alpha * l_sc[hh]
            acc = alpha * acc_sc[rows, :]
            for m_g, l_g, o_g in parts[hh]:
                w_g = jnp.exp2(m_g - m_new)
                l_new = l_new + w_g * l_g
                acc = acc + w_g * o_g
            m_sc[hh] = m_new
            l_sc[hh] = l_new
            acc_sc[rows, :] = acc

    first_query = qi * tq - ki * tk
    for diag in range(0, tk, tq):
        pl.when(first_query == diag)(functools.partial(tile, diag))
    pl.when(first_query >= tk)(functools.partial(tile, tk))

    @pl.when(last_ref[step] == 1)
    def _():
        out_t = jnp.concatenate([acc_sc[0:MLA_VD, :] / l_sc[0], acc_sc[MLA_VD:, :] / l_sc[1]], axis=0)
        o_ref[0] = out_t.T.astype(o_ref.dtype)


def _attn_prompt(qh, kh, vt):
    b, nh, s, _ = qh.shape
    tq, tk = ATT_TQ, ATT_TK
    assert tk % tq == 0 and tq % ATT_KC == 0 and s % tk == 0
    qi_l, ki_l, last_l = [], [], []
    for qi in range(s // tq):
        nk = ((qi + 1) * tq + tk - 1) // tk
        for ki in range(nk):
            qi_l.append(qi)
            ki_l.append(ki)
            last_l.append(int(ki == nk - 1))
    nsteps = len(qi_l)
    sched = [jnp.asarray(np.asarray(a, np.int32)) for a in (qi_l, ki_l, last_l)]
    grid_spec = pltpu.PrefetchScalarGridSpec(
        num_scalar_prefetch=3,
        grid=(b, nh // 2, nsteps),
        in_specs=[pl.BlockSpec((1, 2, tq, HEAD_PAD), lambda bi, hp, st, qi, ki, la: (bi, hp, qi[st], 0)),
                  pl.BlockSpec((1, 2, tk, HEAD_PAD), lambda bi, hp, st, qi, ki, la: (bi, hp, ki[st], 0)),
                  pl.BlockSpec((1, 2 * MLA_VD, tk), lambda bi, hp, st, qi, ki, la: (bi, hp, ki[st]))],
        out_specs=pl.BlockSpec((1, tq, LANES), lambda bi, hp, st, qi, ki, la: (bi, qi[st], hp)),
        scratch_shapes=[pltpu.VMEM((2, 1, tq), F32), pltpu.VMEM((2, 1, tq), F32),
                        pltpu.VMEM((2 * MLA_VD, tq), F32)],
    )
    return pl.pallas_call(
        _attn_kernel,
        grid_spec=grid_spec,
        out_shape=jax.ShapeDtypeStruct((b, s, nh * MLA_VD), BF16),
        compiler_params=pltpu.CompilerParams(
            dimension_semantics=("arbitrary", "arbitrary", "arbitrary"),
            vmem_limit_bytes=VMEM_LIMIT),
        name="attn_prompt",
    )(*sched, qh, kh, vt)


def _dec_page_copies(pt_ref, pool_ckv, pool_kr, cbuf, kbuf, sems, step, slot):
    copies = []
    for i in range(DEC_PG):
        page = pt_ref[step * DEC_PG + i]
        keys = pl.ds(i * PAGE_SIZE, PAGE_SIZE)
        copies.append(pltpu.make_async_copy(pool_ckv.at[page], cbuf.at[slot, keys, :], sems.at[slot, 0]))
        copies.append(pltpu.make_async_copy(pool_kr.at[page], kbuf.at[slot, :, keys], sems.at[slot, 1]))
    return copies


def _dec_wait_slot(cbuf, kbuf, sems, slot):
    pltpu.make_async_copy(cbuf.at[slot], cbuf.at[slot], sems.at[slot, 0]).wait()
    pltpu.make_async_copy(kbuf.at[slot], kbuf.at[slot], sems.at[slot, 1]).wait()


def _dec_kernel(pt_ref, qlat_ref, qh_ref, ckvn_ref, krn_ref, pool_ckv, pool_kr,
                o_ref, m_sc, l_sc, acc_sc, padc_sc, padk_sc, cbuf, kbuf, sems):
    j = pl.program_id(1)
    nj = pl.num_programs(1)
    step = pl.program_id(0) * nj + j
    last_step = pl.num_programs(0) * nj - 1
    slot = lax.rem(step, DEC_SLOTS)
    qlat = jnp.concatenate([qlat_ref[0, :, hh * KV_LORA:(hh + 1) * KV_LORA] for hh in range(MLA_H)], axis=0)
    qh = jnp.concatenate([qh_ref[0, :, hh * HEAD_PAD:(hh + 1) * HEAD_PAD] for hh in range(MLA_H)], axis=0)
    ds = ckvn_ref.shape[1]
    copies = functools.partial(_dec_page_copies, pt_ref, pool_ckv, pool_kr, cbuf, kbuf, sems)

    @pl.when(step == 0)
    def _():
        for s0 in range(DEC_SLOTS):
            for c in copies(s0, s0):
                c.start()

    _dec_wait_slot(cbuf, kbuf, sems, slot)
    next_copies = copies(step + DEC_SLOTS, slot)

    @pl.when(j == 0)
    def _():
        padc_sc[...] = jnp.zeros(padc_sc.shape, BF16)
        padk_sc[...] = jnp.zeros(padk_sc.shape, BF16)
        padc_sc[0:ds, :] = ckvn_ref[0].astype(BF16)
        padk_sc[0:ds, :] = krn_ref[0]
        cn = padc_sc[...]
        s = _nt(qlat, cn) + _nt(qh, padk_sc[...])
        tok = lax.broadcasted_iota(jnp.int32, (MLA_H, ds, s.shape[1]), 1).reshape(s.shape)
        key = lax.broadcasted_iota(jnp.int32, s.shape, 1)
        s = jnp.where(key <= tok, s, -jnp.inf)
        m = jnp.max(s, axis=-1, keepdims=True)
        p = jnp.exp2(s - m)
        m_sc[...] = m
        l_sc[...] = jnp.sum(p, axis=-1, keepdims=True)
        acc_sc[...] = _mm(p.astype(BF16), cn)

    qr = qh[:, MLA_NOPE:MLA_NOPE + MLA_ROPE]
    pages = cbuf[slot].astype(BF16)
    krt = kbuf[slot].astype(BF16)
    gk = pages.shape[0] // DEC_GROUPS
    groups = [pages[g * gk:(g + 1) * gk] for g in range(DEC_GROUPS)]
    scores, probs, parts = {}, {}, []

    def score(g):
        scores[g] = _nt(qlat, groups[g]) + _mm(qr, krt[:, g * gk:(g + 1) * gk])

    def soft(g):
        m_g = jnp.max(scores[g], axis=-1, keepdims=True)
        p = jnp.exp2(scores[g] - m_g)
        probs[g] = (m_g, jnp.sum(p, axis=-1, keepdims=True), p.astype(BF16))

    def value(g):
        m_g, l_g, pb = probs[g]
        parts.append((m_g, l_g, _mm(pb, groups[g])))

    per_group = len(next_copies) // DEC_GROUPS
    for g in range(DEC_GROUPS):
        score(g)
    soft(0)
    for g in range(DEC_GROUPS):
        if g + 1 < DEC_GROUPS:
            soft(g + 1)
        value(g)
        for c in next_copies[g * per_group:(g + 1) * per_group]:
            c.start()
    m_prev = m_sc[...]
    m_new = functools.reduce(jnp.maximum, [m_g for m_g, _, _ in parts], m_prev)
    alpha = jnp.exp2(m_prev - m_new)
    l_new = alpha * l_sc[...]
    acc = alpha * acc_sc[...]
    for m_g, l_g, o_g in parts:
        w_g = jnp.exp2(m_g - m_new)
        l_new = l_new + w_g * l_g
        acc = acc + w_g * o_g
    m_sc[...] = m_new
    l_sc[...] = l_new
    acc_sc[...] = acc

    @pl.when(j == nj - 1)
    def _():
        out = acc_sc[...] / l_sc[...]
        for hh in range(MLA_H):
            o_ref[0, :, hh * KV_LORA:(hh + 1) * KV_LORA] = out[hh * ds:(hh + 1) * ds, :]

    @pl.when(step == last_step)
    def _():
        for s0 in range(DEC_SLOTS):
            _dec_wait_slot(cbuf, kbuf, sems, s0)


def _attn_sample(qlat, qh, ckvn, krn, pool_ckv, pool_kr, page_table):
    b, ds, _ = qlat.shape
    nq = ds * MLA_H
    n_pages = page_table.shape[1]
    pg = DEC_PG
    pt = page_table.reshape(-1)
    pt = jnp.concatenate([pt, pt[:DEC_SLOTS * pg]])

    fixed = lambda w, r: pl.BlockSpec((1, r, w), lambda bi, j, pt_ref: (bi, 0, 0))
    grid_spec = pltpu.PrefetchScalarGridSpec(
        num_scalar_prefetch=1,
        grid=(b, n_pages // pg),
        in_specs=[fixed(MLA_H * KV_LORA, ds), fixed(MLA_H * HEAD_PAD, ds), fixed(KV_LORA, ds), fixed(HEAD_PAD, ds),
                  pl.BlockSpec(memory_space=pl.ANY), pl.BlockSpec(memory_space=pl.ANY)],
        out_specs=fixed(MLA_H * KV_LORA, ds),
        scratch_shapes=[pltpu.VMEM((nq, 1), F32), pltpu.VMEM((nq, 1), F32),
                        pltpu.VMEM((nq, KV_LORA), F32),
                        pltpu.VMEM((PAGE_SIZE, KV_LORA), BF16), pltpu.VMEM((PAGE_SIZE, HEAD_PAD), BF16),
                        pltpu.VMEM((DEC_SLOTS, pg * PAGE_SIZE, KV_LORA), F32),
                        pltpu.VMEM((DEC_SLOTS, MLA_ROPE, pg * PAGE_SIZE), F32),
                        pltpu.SemaphoreType.DMA((DEC_SLOTS, 2))],
    )
    return pl.pallas_call(
        _dec_kernel,
        grid_spec=grid_spec,
        out_shape=jax.ShapeDtypeStruct((b, ds, MLA_H * KV_LORA), F32),
        compiler_params=pltpu.CompilerParams(dimension_semantics=("arbitrary", "arbitrary"),
                                             vmem_limit_bytes=VMEM_LIMIT),
        name="attn_sample",
    )(pt, qlat, qh, ckvn, krn, pool_ckv, pool_kr)


def _sigmoid(x):
    return 0.5 * jnp.tanh(0.5 * x) + 0.5


def _silu(x):
    h = 0.5 * x
    return h * jnp.tanh(h) + h


def _back_kernel(absorb, x_ref, gate_ref, hml_ref, mo_ref, mz_ref, o_ref, az_ref, gn_ref, wuv_ref,
                 wout_ref, lng_ref, lnb_ref, y_ref):
    tb, ts, d = x_ref.shape
    tm = tb * ts
    hm = hml_ref[...].astype(F32) * _sigmoid(mo_ref[...].astype(F32))
    parts = []
    for hh in range(ML_H):
        t = hm[:, hh * ML_DV:(hh + 1) * ML_DV]
        mu = jnp.mean(t, axis=-1, keepdims=True)
        tc = t - mu
        var = jnp.mean(tc * tc, axis=-1, keepdims=True)
        parts.append(tc * lax.rsqrt(var + EPS))
    mz = mz_ref[...].astype(F32)
    y_ml = jnp.concatenate(parts, axis=1) * gn_ref[...] * _silu(mz)
    if absorb:
        olat = o_ref[...].astype(BF16)
        o_mla = jnp.concatenate(
            [_mm(olat[:, p * 2 * KV_LORA:(p + 1) * 2 * KV_LORA], wuv_ref[p]) for p in range(MLA_H // 2)],
            axis=1)
    else:
        o_mla = o_ref[...].astype(F32)
    az = az_ref[...].astype(F32)
    y_mla = o_mla * _silu(az)
    out = _mm(y_ml.astype(BF16), wout_ref[0:ML_W, :]) + _mm(y_mla.astype(BF16), wout_ref[ML_W:, :])
    z = ALPHA * x_ref[...] + gate_ref[...] * out.reshape(tb, ts, d)
    mu = jnp.mean(z, axis=-1, keepdims=True)
    zc = z - mu
    var = jnp.mean(zc * zc, axis=-1, keepdims=True)
    y_ref[...] = zc * lax.rsqrt(var + EPS) * lng_ref[...] + lnb_ref[...]


def _back(absorb, x, mod, hml, mo, mz, o, az, wts):
    b, s, d = x.shape
    n = b * s
    tm = BACK_TM
    if absorb:
        tb, ts = tm // s, s
        x_map = lambda i: (i, 0, 0)
        gate_map = lambda i: (i, 0, 2)
    else:
        tb, ts = 1, tm
        spb = s // tm
        x_map = lambda i: (i // spb, i % spb, 0)
        gate_map = lambda i: (i // spb, 0, 2)
    const2 = lambda i: (0, 0)
    tok = lambda w: pl.BlockSpec((tm, w), lambda i: (i, 0))
    ow = o.shape[1]
    return pl.pallas_call(
        functools.partial(_back_kernel, absorb),
        grid=(n // tm,),
        in_specs=[pl.BlockSpec((tb, ts, d), x_map),
                  pl.BlockSpec((tb, 1, d), gate_map),
                  tok(ML_W), tok(ML_W), tok(ML_W), tok(ow), tok(MLA_W),
                  pl.BlockSpec((1, ML_W), const2),
                  pl.BlockSpec((MLA_H // 2, 2 * KV_LORA, LANES), lambda i: (0, 0, 0)),
                  pl.BlockSpec((d, d), const2),
                  pl.BlockSpec((1, d), const2),
                  pl.BlockSpec((1, d), const2)],
        out_specs=pl.BlockSpec((tb, ts, d), x_map),
        out_shape=jax.ShapeDtypeStruct((b, s, d), F32),
        compiler_params=pltpu.CompilerParams(dimension_semantics=("arbitrary",),
                                             vmem_limit_bytes=VMEM_LIMIT),
        name="back_sample" if absorb else "back_prompt",
    )(x, mod, hml, mo, mz, o, az, wts["gn"], wts["wuv_pair"], wts["w_out"], wts["ln_g"], wts["ln_b"])


def _prep_weights(l, w_in, ml_b_i, ml_b_f, ml_gn, mla_q_norm, mla_kv_norm, mla_w_uq, mla_w_uk,
                  mla_w_uv, w_out, ln_g, ln_b):
    offs = np.cumsum((0, 512, 512, 512, ML_H, ML_H, 512, 512, Q_LORA, KV_LORA, MLA_ROPE, 512))
    names = ("mq", "mk", "mv", "mi", "mf", "mo", "mz", "cq", "ckv", "kr", "az")
    w = {nm: w_in[l][:, offs[i]:offs[i + 1]] for i, nm in enumerate(names)}
    d = w_in.shape[1]
    z = lambda c: jnp.zeros((d, c), F32)
    kr1, kr2 = w["kr"][:, :ROPE_HALF], w["kr"][:, ROPE_HALF:]
    tail = HEAD_PAD - MLA_NOPE - MLA_ROPE
    w_in_p = jnp.concatenate(
        [w["mq"], w["mk"], w["mv"], w["mi"], w["mf"], z(LANES - 2 * ML_H), z(MLA_NOPE), kr1, kr2, z(tail),
         w["mo"], w["mz"], w["cq"], w["ckv"], w["az"]], axis=1).astype(BF16)
    b_gate = jnp.concatenate([ml_b_i[l], ml_b_f[l], jnp.zeros((LANES - 2 * ML_H,), F32)]).reshape(1, LANES)
    uq = mla_w_uq[l].reshape(Q_LORA, MLA_H, MLA_NOPE + MLA_ROPE)
    nope, r1, r2 = uq[..., :MLA_NOPE], uq[..., MLA_NOPE:MLA_NOPE + ROPE_HALF], uq[..., MLA_NOPE + ROPE_HALF:]
    zq = lambda c: jnp.zeros((Q_LORA, MLA_H, c), F32)
    wqa = jnp.concatenate([nope, r1, r2, zq(tail)], axis=-1).reshape(Q_LORA, MLA_H * HEAD_PAD).astype(BF16)
    uk = mla_w_uk[l].reshape(KV_LORA, MLA_H, MLA_NOPE)
    wk_pad = jnp.concatenate([uk, jnp.zeros((KV_LORA, MLA_H, HEAD_PAD - MLA_NOPE), F32)],
                             axis=-1).reshape(KV_LORA, MLA_H * HEAD_PAD).astype(BF16)
    wukt = jnp.concatenate([jnp.transpose(uk, (1, 2, 0)),
                            jnp.zeros((MLA_H, HEAD_PAD - MLA_NOPE, KV_LORA), F32)], axis=1).astype(BF16)
    uv = mla_w_uv[l].reshape(KV_LORA, MLA_H // 2, 2, MLA_VD)
    zv = jnp.zeros((KV_LORA, MLA_H // 2, MLA_VD), F32)
    wuv_pair = jnp.concatenate(
        [jnp.concatenate([uv[:, :, 0], zv], axis=-1), jnp.concatenate([zv, uv[:, :, 1]], axis=-1)],
        axis=0)
    wuv_pair = jnp.transpose(wuv_pair, (1, 0, 2)).astype(BF16)
    return dict(w_in=w_in_p, b_gate=b_gate, q_norm=mla_q_norm[l].reshape(1, -1),
                kv_norm=mla_kv_norm[l].reshape(1, -1), wqa=wqa, wk_pad=wk_pad, wukt=wukt,
                w_uv_t=mla_w_uv[l].T.astype(BF16), wuv_pair=wuv_pair, gn=ml_gn[l].reshape(1, -1),
                w_out=w_out[l].astype(BF16), ln_g=ln_g[l].reshape(1, -1), ln_b=ln_b[l].reshape(1, -1))


def _rope_tables(pos):
    f32 = np.float32
    inv = ROPE_THETA ** (-np.arange(ROPE_HALF, dtype=np.float64) / ROPE_HALF)
    ang = pos.astype(np.float64)[:, None] * inv[None, :]
    n = pos.shape[0]
    tail = HEAD_PAD - MLA_NOPE - MLA_ROPE
    cos, sin = np.cos(ang).astype(f32), np.sin(ang).astype(f32)
    zeros = lambda c: np.zeros((n, c), f32)
    cm = np.concatenate([np.ones((n, MLA_NOPE), f32), cos, cos, zeros(tail)], 1)
    s_lo = np.concatenate([zeros(MLA_NOPE), -sin, zeros(ROPE_HALF), zeros(tail)], 1)
    s_hi = np.concatenate([zeros(MLA_NOPE), zeros(ROPE_HALF), sin, zeros(tail)], 1)
    return np.concatenate([cm, s_lo, s_hi], 1)


def kernel(x_prompt, x_sample, c_prompt, c_sample, cache_ckv, cache_krope, state_C, state_n, state_m,
           page_table, w_ada, b_ada, w_in, ml_b_i, ml_b_f, ml_gn, mla_q_norm, mla_kv_norm,
           mla_w_uq, mla_w_uk, mla_w_uv, w_out, ln_g, ln_b):
    bp, sp, d = x_prompt.shape
    bs, ss, _ = x_sample.shape
    past = page_table.shape[1] * PAGE_SIZE
    tab_p = jnp.asarray(_rope_tables(np.arange(sp)))
    tab_s = jnp.asarray(np.tile(_rope_tables(past + np.arange(ss)), (FRONT_TM // ss, 1)))
    pad = (-(bp + bs)) % 8
    c_all = jnp.concatenate([c_prompt, c_sample, jnp.zeros((pad, d), F32)], axis=0)

    xp, xs = x_prompt, x_sample
    outs = [[] for _ in range(10)]
    for l in range(DEPTH):
        wts = _prep_weights(l, w_in, ml_b_i, ml_b_f, ml_gn, mla_q_norm, mla_kv_norm, mla_w_uq, mla_w_uk,
                            mla_w_uv, w_out, ln_g, ln_b)
        mod = _ada(c_all, w_ada[l], b_ada[l])
        mod_p = mod[:bp].reshape(bp, 1, 3 * d)
        mod_s = mod[bp:bp + bs].reshape(bs, 1, 3 * d)

        mq, mk, mv, g, mo, mz, az, ckv, kr, qh, kh, vt = _front(False, xp, mod_p, tab_p, wts)
        seq3 = lambda t: t.reshape(bp, sp, -1)
        c0 = jnp.zeros((bp, ML_H, ML_DK, ML_DV), F32)
        n0 = jnp.zeros((bp, ML_H, 1, ML_DK), F32)
        m0 = jnp.full((bp, ML_H, 1, LANES), -jnp.inf, F32)
        hml, c_p, n_p, m_p = _mlstm(seq3(mq), seq3(mk), seq3(mv), seq3(g), c0, n0, m0, ML_CHUNK, bp)
        o_p = _attn_prompt(qh, kh, vt)
        yp = _back(False, xp, mod_p, hml.reshape(bp * sp, -1), mo, mz, o_p.reshape(bp * sp, -1), az, wts)
        outs[0].append(ckv.reshape(bp, sp, KV_LORA))
        outs[1].append(kr.reshape(bp, sp, MLA_ROPE))
        outs[2].append(c_p)
        outs[3].append(n_p[:, :, 0, :])
        outs[4].append(m_p[:, :, 0, 0])

        mq, mk, mv, g, mo, mz, az, ckv, kr, qlat, qh, krp = _front(True, xs, mod_s, tab_s, wts)
        seq3 = lambda t: t.reshape(bs, ss, -1)
        n0 = state_n[l].astype(F32).reshape(bs, ML_H, 1, ML_DK)
        m0 = jnp.broadcast_to(state_m[l].astype(F32)[:, :, None, None], (bs, ML_H, 1, LANES))
        hml, c_s, n_s, m_s = _mlstm(seq3(mq), seq3(mk), seq3(mv), seq3(g), state_C[l].astype(F32), n0, m0, ss, 8)
        o_s = _attn_sample(seq3(qlat), seq3(qh), seq3(ckv), seq3(krp), cache_ckv[l],
                           jnp.swapaxes(cache_krope[l], 1, 2), page_table)
        ys = _back(True, xs, mod_s, hml.reshape(bs * ss, -1), mo, mz,
                   o_s.reshape(bs * ss, MLA_H * KV_LORA), az, wts)
        outs[5].append(ckv.reshape(bs, ss, KV_LORA).astype(cache_ckv.dtype))
        outs[6].append(kr.reshape(bs, ss, MLA_ROPE).astype(cache_krope.dtype))
        outs[7].append(c_s.astype(state_C.dtype))
        outs[8].append(n_s[:, :, 0, :].astype(state_n.dtype))
        outs[9].append(m_s[:, :, 0, 0].astype(state_m.dtype))
        xp, xs = yp, ys
    return (xp, xs) + tuple(jnp.stack(o) for o in outs)
```

```python
import functools

import numpy as np
import jax
import jax.numpy as jnp
from jax import lax
from jax.experimental import pallas as pl
from jax.experimental.pallas import tpu as pltpu

F32 = jnp.float32
BF16 = jnp.bfloat16

D_MODEL = 1024
DEPTH = 1
PAGE_SIZE = 128
ML_W = 512
MLA_W = 512
ML_DK = 128
ML_DV = 128
ML_H = 4
MLA_VD = 64
MLA_H = 8
MLA_NOPE = 64
MLA_ROPE = 32
ROPE_HALF = MLA_ROPE // 2
MLA_SCALE = (MLA_NOPE + MLA_ROPE) ** -0.5
Q_SCALE = MLA_SCALE * float(np.log2(np.e))
Q_LORA = 384
KV_LORA = 256
ROPE_THETA = 10000.0
ML_CHUNK = 128
EPS = 1e-6
ALPHA = (2.0 * DEPTH) ** 0.25

LANES = 128
HEAD_PAD = 128
VMEM_LIMIT = 48 * 1024 * 1024

_SEG = {}
_off = 0
for _name, _w in (("mq", 512), ("mk", 512), ("mv", 512), ("gk", 256), ("mo", 512), ("mz", 512),
                  ("cq", Q_LORA), ("ckv", KV_LORA), ("az", 512)):
    _SEG[_name] = (_off, _off + _w)
    _off += _w
N_IN_PAD = _off

FRONT_TM = 512
BACK_TM = 512
ATT_TQ = 512
ATT_TK = 4096
ATT_KC = 256
MASKED_MAX_FLOOR = -1e30
DEC_PG = 64
DEC_GROUPS = 8
DEC_SLOTS = 2


def _nt(a, b):
    return lax.dot_general(a, b, (((1,), (1,)), ((), ())), preferred_element_type=F32)


def _tn(a, b):
    return lax.dot_general(a, b, (((0,), (0,)), ((), ())), preferred_element_type=F32)


def _mm(a, b):
    return jnp.dot(a, b, preferred_element_type=F32)


def _mm_exact(a, b):
    return jnp.dot(a, b, preferred_element_type=F32, precision=lax.Precision.HIGHEST)


def _nt_exact(a, b):
    return lax.dot_general(a, b, (((1,), (1,)), ((), ())), preferred_element_type=F32,
                           precision=lax.Precision.HIGHEST)


def _ada_kernel(c_ref, w_ref, b_ref, o_ref):
    o_ref[...] = _mm(c_ref[...].astype(BF16), w_ref[...].astype(BF16)) + b_ref[...]


def _ada(c_all, w_ada, b_ada):
    m = c_all.shape[0]
    tn = 512
    return pl.pallas_call(
        _ada_kernel,
        grid=(3 * D_MODEL // tn,),
        in_specs=[pl.BlockSpec((m, D_MODEL), lambda j: (0, 0)),
                  pl.BlockSpec((D_MODEL, tn), lambda j: (0, j)),
                  pl.BlockSpec((1, tn), lambda j: (0, j))],
        out_specs=pl.BlockSpec((m, tn), lambda j: (0, j)),
        out_shape=jax.ShapeDtypeStruct((m, 3 * D_MODEL), F32),
        name="ada",
    )(c_all, w_ada, b_ada.reshape(1, -1))


_IN_ROWS = {}
_off = 0
for _name, _w in (("mq", 512), ("mk", 512), ("mv", 512), ("gates", 2 * ML_H), ("mo", 512), ("mz", 512),
                  ("cq", Q_LORA), ("ckv", KV_LORA), ("kr", MLA_ROPE), ("az", 512)):
    _IN_ROWS[_name] = (_off, _off + _w)
    _off += _w
N_IN = _off


def _wprep_kernel(wt_ref, o_ref):
    dk = wt_ref.shape[1]
    rows = lambda name: wt_ref[_IN_ROWS[name][0]:_IN_ROWS[name][1], :]
    zeros = lambda r: jnp.zeros((r, dk), F32)
    tail = HEAD_PAD - MLA_NOPE - MLA_ROPE
    gk = jnp.concatenate([rows("gates"), zeros(LANES - 2 * ML_H), zeros(MLA_NOPE), rows("kr"), zeros(tail)], axis=0)
    for name in _SEG:
        lo, hi = _SEG[name]
        o_ref[:, lo:hi] = (gk if name == "gk" else rows(name)).T.astype(BF16)


def _wprep(w_in_t):
    n_in, d = w_in_t.shape
    dk = 256
    return pl.pallas_call(
        _wprep_kernel,
        grid=(d // dk,),
        in_specs=[pl.BlockSpec((n_in, dk), lambda i: (0, i))],
        out_specs=pl.BlockSpec((dk, N_IN_PAD), lambda i: (i, 0)),
        out_shape=jax.ShapeDtypeStruct((d, N_IN_PAD), BF16),
        compiler_params=pltpu.CompilerParams(dimension_semantics=("arbitrary",),
                                             vmem_limit_bytes=VMEM_LIMIT),
        name="wprep",
    )(w_in_t)


def _rms(x, g):
    return x * lax.rsqrt(jnp.mean(x * x, axis=-1, keepdims=True) + EPS) * g


def _log_sigmoid(x):
    return jnp.minimum(x, 0.0) - jnp.log1p(jnp.exp(-jnp.abs(x)))


def _rotary(a, tab_ref):
    cm = tab_ref[:, 0:LANES]
    s_lo = tab_ref[:, LANES:2 * LANES]
    s_hi = tab_ref[:, 2 * LANES:3 * LANES]
    return a * cm + pltpu.roll(a, LANES - ROPE_HALF, 1) * s_lo + pltpu.roll(a, ROPE_HALF, 1) * s_hi


def _front_kernel(absorb, x_ref, sh_ref, sc_ref, tab_ref, win_ref, bg_ref, qn_ref, kvn_ref,
                  wqa_ref, wk_ref, wv_ref,
                  mq_ref, mk_ref, mv_ref, g_ref, mo_ref, mz_ref, az_ref, ckv_ref, kr_ref,
                  o1_ref, o2_ref, o3_ref):
    tb, ts, d = x_ref.shape
    tm = tb * ts
    h = x_ref[...] * (1.0 + sc_ref[...]) + sh_ref[...]
    h = h.reshape(tm, d).astype(BF16)

    def seg(name):
        lo, hi = _SEG[name]
        return _mm(h, win_ref[:, lo:hi])

    mq_ref[...] = seg("mq").astype(BF16)
    mk_ref[...] = (seg("mk") * (ML_DK ** -0.5)).astype(BF16)
    mv_ref[...] = seg("mv").astype(BF16)
    gk = seg("gk")
    gz = gk[:, :LANES] + bg_ref[...]
    lane = lax.broadcasted_iota(jnp.int32, gz.shape, 1)
    g_ref[...] = jnp.where(lane < ML_H, gz, jnp.where(lane < 2 * ML_H, _log_sigmoid(gz), 0.0))
    mo_ref[...] = seg("mo").astype(BF16)
    mz_ref[...] = seg("mz").astype(BF16)
    az_ref[...] = seg("az").astype(BF16)

    ckvn = _rms(seg("ckv"), kvn_ref[...])
    ckv_ref[...] = ckvn
    ckvn_b = ckvn.astype(BF16)
    krp = _rotary(gk[:, LANES:], tab_ref)
    kr_ref[...] = krp[:, MLA_NOPE:MLA_NOPE + MLA_ROPE]

    cqn = _rms(seg("cq"), qn_ref[...]).astype(BF16)
    qa = _mm(cqn, wqa_ref[...])
    for hh in range(MLA_H):
        sl = slice(hh * HEAD_PAD, (hh + 1) * HEAD_PAD)
        qh = (_rotary(qa[:, sl], tab_ref) * Q_SCALE).astype(BF16)
        if absorb:
            o1_ref[:, hh * KV_LORA:(hh + 1) * KV_LORA] = _mm(qh, wk_ref[hh]).astype(BF16)
            o2_ref[:, sl] = qh
        else:
            o1_ref[0, hh] = qh
            o2_ref[0, hh] = (_mm(ckvn_b, wk_ref[:, sl]) + krp).astype(BF16)
    if absorb:
        o3_ref[...] = krp.astype(BF16)
    else:
        o3_ref[0] = _nt(wv_ref[...], ckvn_b).astype(BF16)


def _front(absorb, x, mod, tab, wts):
    b, s, d = x.shape
    n = b * s
    tm = FRONT_TM
    if absorb:
        tb, ts = tm // s, s
        x_map = lambda i: (i, 0, 0)
        mod_map = lambda k: (lambda i: (i, 0, k))
        tab_map = lambda i: (0, 0)
    else:
        tb, ts = 1, tm
        spb = s // tm
        x_map = lambda i: (i // spb, i % spb, 0)
        mod_map = lambda k: (lambda i: (i // spb, 0, k))
        tab_map = lambda i: (i % spb, 0)
    const2 = lambda i: (0, 0)
    tok = lambda w: pl.BlockSpec((tm, w), lambda i: (i, 0))
    in_specs = [
        pl.BlockSpec((tb, ts, d), x_map),
        pl.BlockSpec((tb, 1, d), mod_map(0)),
        pl.BlockSpec((tb, 1, d), mod_map(1)),
        pl.BlockSpec((tm, 3 * LANES), tab_map),
        pl.BlockSpec((d, N_IN_PAD), const2),
        pl.BlockSpec((1, LANES), const2),
        pl.BlockSpec((1, Q_LORA), const2),
        pl.BlockSpec((1, KV_LORA), const2),
        pl.BlockSpec((Q_LORA, MLA_H * HEAD_PAD), const2),
    ]
    out_specs = [tok(512), tok(512), tok(512), tok(LANES), tok(512), tok(512), tok(512),
                 tok(KV_LORA), tok(MLA_ROPE)]
    out_shape = [jax.ShapeDtypeStruct((n, w), dt)
                 for w, dt in ((512, BF16), (512, BF16), (512, BF16), (LANES, F32), (512, BF16), (512, BF16),
                               (512, BF16), (KV_LORA, F32), (MLA_ROPE, F32))]
    if absorb:
        in_specs += [pl.BlockSpec((MLA_H, HEAD_PAD, KV_LORA), lambda i: (0, 0, 0)),
                     pl.BlockSpec((MLA_W, KV_LORA), const2)]
        out_specs += [tok(MLA_H * KV_LORA), tok(MLA_H * HEAD_PAD), tok(HEAD_PAD)]
        out_shape += [jax.ShapeDtypeStruct((n, MLA_H * KV_LORA), BF16),
                      jax.ShapeDtypeStruct((n, MLA_H * HEAD_PAD), BF16),
                      jax.ShapeDtypeStruct((n, HEAD_PAD), BF16)]
        wk = wts["wukt"]
    else:
        head_map = lambda i: (i // spb, 0, i % spb, 0)
        in_specs += [pl.BlockSpec((KV_LORA, MLA_H * HEAD_PAD), const2),
                     pl.BlockSpec((MLA_W, KV_LORA), const2)]
        out_specs += [pl.BlockSpec((1, MLA_H, tm, HEAD_PAD), head_map),
                      pl.BlockSpec((1, MLA_H, tm, HEAD_PAD), head_map),
                      pl.BlockSpec((1, MLA_W, tm), lambda i: (i // spb, 0, i % spb))]
        out_shape += [jax.ShapeDtypeStruct((b, MLA_H, s, HEAD_PAD), BF16),
                      jax.ShapeDtypeStruct((b, MLA_H, s, HEAD_PAD), BF16),
                      jax.ShapeDtypeStruct((b, MLA_W, s), BF16)]
        wk = wts["wk_pad"]
    return pl.pallas_call(
        functools.partial(_front_kernel, absorb),
        grid=(n // tm,),
        in_specs=in_specs,
        out_specs=out_specs,
        out_shape=out_shape,
        compiler_params=pltpu.CompilerParams(dimension_semantics=("arbitrary",),
                                             vmem_limit_bytes=VMEM_LIMIT),
        name="front_sample" if absorb else "front_prompt",
    )(x, mod, mod, tab, wts["w_in"], wts["b_gate"], wts["q_norm"], wts["kv_norm"],
      wts["wqa"], wk, wts["w_uv_t"])


def _mlstm_kernel(q_ref, k_ref, v_ref, g_ref, c0_ref, n0_ref, m0_ref,
                  h_ref, c_ref, n_ref, m_ref):
    tb, L, _ = q_ref.shape
    ci = pl.program_id(1)

    @pl.when(ci == 0)
    def _():
        c_ref[...] = c0_ref[...]
        n_ref[...] = n0_ref[...]
        m_ref[...] = m0_ref[...]

    row = lax.broadcasted_iota(jnp.int32, (L, L), 0)
    col = lax.broadcasted_iota(jnp.int32, (L, L), 1)
    causal = col <= row
    tril = causal.astype(F32)
    sel = (lax.broadcasted_iota(jnp.int32, (8, LANES), 0)
           == lax.broadcasted_iota(jnp.int32, (8, LANES), 1)).astype(F32)

    chains = [(t, hh) for t in range(tb) for hh in range(ML_H)]
    sl = lambda hh: slice(hh * ML_DK, (hh + 1) * ML_DK)
    gates = {}
    for t in range(tb):
        g = g_ref[t]
        fcum = _mm_exact(tril, g)
        g_rows = _nt_exact(sel, g)
        f_rows = _nt_exact(sel, fcum)
        gates[t] = (g, fcum, g_rows, f_rows)

    qk, qc, c_prevs = {}, {}, {}
    for t, hh in chains:
        qb = q_ref[t, :, sl(hh)]
        c_prevs[t, hh] = c_ref[t, hh]
        qk[t, hh] = _nt(qb, k_ref[t, :, sl(hh)])
        qc[t, hh] = _mm(qb, c_prevs[t, hh].astype(BF16))

    stab = {}
    for t, hh in chains:
        g, fcum, g_rows, f_rows = gates[t]
        m_prev = m_ref[t, hh][:, :1]
        f_col = fcum[:, ML_H + hh:ML_H + hh + 1]
        f_row = f_rows[ML_H + hh:ML_H + hh + 1, :]
        ig_row = g_rows[hh:hh + 1, :]
        dmat = jnp.where(causal, f_col - f_row + ig_row, -jnp.inf)
        m_inter = jnp.broadcast_to(f_col, (L, LANES)) + m_prev
        m_t = jnp.maximum(m_inter, jnp.max(dmat, axis=-1, keepdims=True))
        stab[t, hh] = (jnp.exp(dmat - m_t[:, :L]), jnp.exp(m_inter - m_t), m_t)

    svs = {}
    ones_cols = jnp.ones((L, LANES), BF16)
    for t, hh in chains:
        sq = (qk[t, hh] * stab[t, hh][0]).astype(BF16)
        svs[t, hh] = _mm(sq, jnp.concatenate([v_ref[t, :, sl(hh)], ones_cols], axis=1))

    for t, hh in chains:
        _, a, m_t = stab[t, hh]
        n_rep = jnp.broadcast_to(n_ref[t, hh], (LANES, ML_DK)).astype(BF16)
        qn = _nt(q_ref[t, :, sl(hh)], n_rep)
        num = svs[t, hh][:, :ML_DV] + a * qc[t, hh]
        den = svs[t, hh][:, ML_DV:] + a * qn
        h_ref[t, :, sl(hh)] = (num / jnp.maximum(jnp.abs(den), jnp.exp(-m_t))).astype(h_ref.dtype)

    for t, hh in chains:
        w, a, m_t = stab[t, hh]
        wl_row = w[L - 1:L, :]
        al = a[L - 1:L, :]
        kb = k_ref[t, :, sl(hh)]
        kw_t = (kb.astype(F32).T * wl_row).astype(BF16)
        c_ref[t, hh] = al[:, :1] * c_prevs[t, hh] + _mm(kw_t, v_ref[t, :, sl(hh)])
        n_ref[t, hh] = al * n_ref[t, hh] + _mm(w[L - 8:L, :].astype(BF16), kb)[7:8, :]
        m_ref[t, hh] = m_t[L - 1:L, :]


def _mlstm(q, k, v, g, c0, n0, m0, chunk, tb):
    b, s, w = q.shape
    nc = s // chunk
    seq = lambda ww: pl.BlockSpec((tb, chunk, ww), lambda i, c: (i, c, 0))
    st_c = pl.BlockSpec((tb, ML_H, ML_DK, ML_DV), lambda i, c: (i, 0, 0, 0))
    st_v = pl.BlockSpec((tb, ML_H, 1, LANES), lambda i, c: (i, 0, 0, 0))
    return pl.pallas_call(
        _mlstm_kernel,
        grid=(b // tb, nc),
        in_specs=[seq(w), seq(w), seq(w), seq(LANES), st_c, st_v, st_v],
        out_specs=[seq(w), st_c, st_v, st_v],
        out_shape=[jax.ShapeDtypeStruct((b, s, w), BF16),
                   jax.ShapeDtypeStruct((b, ML_H, ML_DK, ML_DV), F32),
                   jax.ShapeDtypeStruct((b, ML_H, 1, LANES), F32),
                   jax.ShapeDtypeStruct((b, ML_H, 1, LANES), F32)],
        compiler_params=pltpu.CompilerParams(dimension_semantics=("arbitrary", "arbitrary"),
                                             vmem_limit_bytes=VMEM_LIMIT),
        name="mlstm",
    )(q, k, v, g, c0, n0, m0)


def _attn_kernel(qi_ref, ki_ref, last_ref, q_ref, k_ref, vt_ref, o_ref, m_sc, l_sc, acc_sc):
    step = pl.program_id(2)
    qi = qi_ref[step]
    ki = ki_ref[step]
    tq = q_ref.shape[2]
    tk = k_ref.shape[2]

    @pl.when(ki == 0)
    def _():
        m_sc[...] = jnp.full(m_sc.shape, -jnp.inf, F32)
        l_sc[...] = jnp.zeros(l_sc.shape, F32)
        acc_sc[...] = jnp.zeros(acc_sc.shape, F32)

    def tile(diag):
        chains = [(hh, c0) for hh in range(2) for c0 in range(0, min(tk, diag + tq), ATT_KC)]
        scores = {}
        for hh, c0 in chains:
            st = _nt(k_ref[0, hh, c0:c0 + ATT_KC, :], q_ref[0, hh])
            if c0 >= diag:
                keys = c0 - diag + lax.broadcasted_iota(jnp.int32, (ATT_KC, tq), 0)
                qrys = lax.broadcasted_iota(jnp.int32, (ATT_KC, tq), 1)
                st = jnp.where(keys <= qrys, st, -jnp.inf)
            scores[hh, c0] = st
        parts = ([], [])
        ones_rows = jnp.ones((16, ATT_KC), BF16)
        for hh, c0 in chains:
            st = scores[hh, c0]
            m_g = jnp.max(st, axis=0, keepdims=True)
            if c0 >= diag:
                m_g = jnp.maximum(m_g, MASKED_MAX_FLOOR)
            pt = jnp.exp2(st - m_g).astype(BF16)
            rows = slice(hh * MLA_VD, (hh + 1) * MLA_VD)
            v_ones = jnp.concatenate([vt_ref[0, rows, c0:c0 + ATT_KC], ones_rows], axis=0)
            o_g = _mm(v_ones, pt)
            parts[hh].append((m_g, o_g[MLA_VD:MLA_VD + 1, :], o_g[:MLA_VD, :]))
        for hh in range(2):
            rows = slice(hh * MLA_VD, (hh + 1) * MLA_VD)
            m_prev = m_sc[hh]
            m_new = functools.reduce(jnp.maximum, [m_g for m_g, _, _ in parts[hh]], m_prev)
            alpha = jnp.exp2(m_prev - m_new)
            l_new = alpha * l_sc[hh]
            acc = alpha * acc_sc[rows, :]
            for m_g, l_g, o_g in parts[hh]:
                w_g = jnp.exp2(m_g - m_new)
                l_new = l_new + w_g * l_g
                acc = acc + w_g * o_g
            m_sc[hh] = m_new
            l_sc[hh] = l_new
            acc_sc[rows, :] = acc

    first_query = qi * tq - ki * tk
    for diag in range(0, tk, tq):
        pl.when(first_query == diag)(functools.partial(tile, diag))
    pl.when(first_query >= tk)(functools.partial(tile, tk))

    @pl.when(last_ref[step] == 1)
    def _():
        out_t = jnp.concatenate([acc_sc[0:MLA_VD, :] / l_sc[0], acc_sc[MLA_VD:, :] / l_sc[1]], axis=0)
        o_ref[0] = out_t.T.astype(o_ref.dtype)


def _attn_prompt(qh, kh, vt):
    b, nh, s, _ = qh.shape
    tq, tk = ATT_TQ, ATT_TK
    assert tk % tq == 0 and tq % ATT_KC == 0 and s % tk == 0
    qi_l, ki_l, last_l = [], [], []
    for qi in range(s // tq):
        nk = ((qi + 1) * tq + tk - 1) // tk
        for ki in range(nk):
            qi_l.append(qi)
            ki_l.append(ki)
            last_l.append(int(ki == nk - 1))
    nsteps = len(qi_l)
    sched = [jnp.asarray(np.asarray(a, np.int32)) for a in (qi_l, ki_l, last_l)]
    grid_spec = pltpu.PrefetchScalarGridSpec(
        num_scalar_prefetch=3,
        grid=(b, nh // 2, nsteps),
        in_specs=[pl.BlockSpec((1, 2, tq, HEAD_PAD), lambda bi, hp, st, qi, ki, la: (bi, hp, qi[st], 0)),
                  pl.BlockSpec((1, 2, tk, HEAD_PAD), lambda bi, hp, st, qi, ki, la: (bi, hp, ki[st], 0)),
                  pl.BlockSpec((1, 2 * MLA_VD, tk), lambda bi, hp, st, qi, ki, la: (bi, hp, ki[st]))],
        out_specs=pl.BlockSpec((1, tq, LANES), lambda bi, hp, st, qi, ki, la: (bi, qi[st], hp)),
        scratch_shapes=[pltpu.VMEM((2, 1, tq), F32), pltpu.VMEM((2, 1, tq), F32),
                        pltpu.VMEM((2 * MLA_VD, tq), F32)],
    )
    return pl.pallas_call(
        _attn_kernel,
        grid_spec=grid_spec,
        out_shape=jax.ShapeDtypeStruct((b, s, nh * MLA_VD), BF16),
        compiler_params=pltpu.CompilerParams(
            dimension_semantics=("arbitrary", "arbitrary", "arbitrary"),
            vmem_limit_bytes=VMEM_LIMIT),
        name="attn_prompt",
    )(*sched, qh, kh, vt)


def _dec_page_copies(pt_ref, pool_ckv, pool_kr, cbuf, kbuf, sems, step, slot):
    copies = []
    for i in range(DEC_PG):
        page = pt_ref[step * DEC_PG + i]
        keys = pl.ds(i * PAGE_SIZE, PAGE_SIZE)
        copies.append(pltpu.make_async_copy(pool_ckv.at[page], cbuf.at[slot, keys, :], sems.at[slot, 0]))
        copies.append(pltpu.make_async_copy(pool_kr.at[page], kbuf.at[slot, :, keys], sems.at[slot, 1]))
    return copies


def _dec_wait_slot(cbuf, kbuf, sems, slot):
    pltpu.make_async_copy(cbuf.at[slot], cbuf.at[slot], sems.at[slot, 0]).wait()
    pltpu.make_async_copy(kbuf.at[slot], kbuf.at[slot], sems.at[slot, 1]).wait()


def _dec_kernel(pt_ref, qlat_ref, qh_ref, ckvn_ref, krn_ref, pool_ckv, pool_kr,
                o_ref, m_sc, l_sc, acc_sc, padc_sc, padk_sc, cbuf, kbuf, sems):
    j = pl.program_id(1)
    nj = pl.num_programs(1)
    step = pl.program_id(0) * nj + j
    last_step = pl.num_programs(0) * nj - 1
    slot = lax.rem(step, DEC_SLOTS)
    qlat = jnp.concatenate([qlat_ref[0, :, hh * KV_LORA:(hh + 1) * KV_LORA] for hh in range(MLA_H)], axis=0)
    qh = jnp.concatenate([qh_ref[0, :, hh * HEAD_PAD:(hh + 1) * HEAD_PAD] for hh in range(MLA_H)], axis=0)
    ds = ckvn_ref.shape[1]
    copies = functools.partial(_dec_page_copies, pt_ref, pool_ckv, pool_kr, cbuf, kbuf, sems)

    @pl.when(step == 0)
    def _():
        for s0 in range(DEC_SLOTS):
            for c in copies(s0, s0):
                c.start()

    _dec_wait_slot(cbuf, kbuf, sems, slot)
    next_copies = copies(step + DEC_SLOTS, slot)

    @pl.when(j == 0)
    def _():
        padc_sc[...] = jnp.zeros(padc_sc.shape, BF16)
        padk_sc[...] = jnp.zeros(padk_sc.shape, BF16)
        padc_sc[0:ds, :] = ckvn_ref[0].astype(BF16)
        padk_sc[0:ds, :] = krn_ref[0]
        cn = padc_sc[...]
        s = _nt(qlat, cn) + _nt(qh, padk_sc[...])
        tok = lax.broadcasted_iota(jnp.int32, (MLA_H, ds, s.shape[1]), 1).reshape(s.shape)
        key = lax.broadcasted_iota(jnp.int32, s.shape, 1)
        s = jnp.where(key <= tok, s, -jnp.inf)
        m = jnp.max(s, axis=-1, keepdims=True)
        p = jnp.exp2(s - m)
        m_sc[...] = m
        l_sc[...] = jnp.sum(p, axis=-1, keepdims=True)
        acc_sc[...] = _mm(p.astype(BF16), cn)

    qr = qh[:, MLA_NOPE:MLA_NOPE + MLA_ROPE]
    pages = cbuf[slot].astype(BF16)
    krt = kbuf[slot].astype(BF16)
    gk = pages.shape[0] // DEC_GROUPS
    groups = [pages[g * gk:(g + 1) * gk] for g in range(DEC_GROUPS)]
    scores, probs, parts = {}, {}, []

    def score(g):
        scores[g] = _nt(qlat, groups[g]) + _mm(qr, krt[:, g * gk:(g + 1) * gk])

    def soft(g):
        m_g = jnp.max(scores[g], axis=-1, keepdims=True)
        p = jnp.exp2(scores[g] - m_g)
        probs[g] = (m_g, jnp.sum(p, axis=-1, keepdims=True), p.astype(BF16))

    def value(g):
        m_g, l_g, pb = probs[g]
        parts.append((m_g, l_g, _mm(pb, groups[g])))

    per_group = len(next_copies) // DEC_GROUPS
    for g in range(DEC_GROUPS):
        score(g)
    soft(0)
    for g in range(DEC_GROUPS):
        if g + 1 < DEC_GROUPS:
            soft(g + 1)
        value(g)
        for c in next_copies[g * per_group:(g + 1) * per_group]:
            c.start()
    m_prev = m_sc[...]
    m_new = functools.reduce(jnp.maximum, [m_g for m_g, _, _ in parts], m_prev)
    alpha = jnp.exp2(m_prev - m_new)
    l_new = alpha * l_sc[...]
    acc = alpha * acc_sc[...]
    for m_g, l_g, o_g in parts:
        w_g = jnp.exp2(m_g - m_new)
        l_new = l_new + w_g * l_g
        acc = acc + w_g * o_g
    m_sc[...] = m_new
    l_sc[...] = l_new
    acc_sc[...] = acc

    @pl.when(j == nj - 1)
    def _():
        out = acc_sc[...] / l_sc[...]
        for hh in range(MLA_H):
            o_ref[0, :, hh * KV_LORA:(hh + 1) * KV_LORA] = out[hh * ds:(hh + 1) * ds, :]

    @pl.when(step == last_step)
    def _():
        for s0 in range(DEC_SLOTS):
            _dec_wait_slot(cbuf, kbuf, sems, s0)


def _attn_sample(qlat, qh, ckvn, krn, pool_ckv, pool_kr, page_table):
    b, ds, _ = qlat.shape
    nq = ds * MLA_H
    n_pages = page_table.shape[1]
    pg = DEC_PG
    pt = page_table.reshape(-1)
    pt = jnp.concatenate([pt, pt[:DEC_SLOTS * pg]])

    fixed = lambda w, r: pl.BlockSpec((1, r, w), lambda bi, j, pt_ref: (bi, 0, 0))
    grid_spec = pltpu.PrefetchScalarGridSpec(
        num_scalar_prefetch=1,
        grid=(b, n_pages // pg),
        in_specs=[fixed(MLA_H * KV_LORA, ds), fixed(MLA_H * HEAD_PAD, ds), fixed(KV_LORA, ds), fixed(HEAD_PAD, ds),
                  pl.BlockSpec(memory_space=pl.ANY), pl.BlockSpec(memory_space=pl.ANY)],
        out_specs=fixed(MLA_H * KV_LORA, ds),
        scratch_shapes=[pltpu.VMEM((nq, 1), F32), pltpu.VMEM((nq, 1), F32),
                        pltpu.VMEM((nq, KV_LORA), F32),
                        pltpu.VMEM((PAGE_SIZE, KV_LORA), BF16), pltpu.VMEM((PAGE_SIZE, HEAD_PAD), BF16),
                        pltpu.VMEM((DEC_SLOTS, pg * PAGE_SIZE, KV_LORA), F32),
                        pltpu.VMEM((DEC_SLOTS, MLA_ROPE, pg * PAGE_SIZE), F32),
                        pltpu.SemaphoreType.DMA((DEC_SLOTS, 2))],
    )
    return pl.pallas_call(
        _dec_kernel,
        grid_spec=grid_spec,
        out_shape=jax.ShapeDtypeStruct((b, ds, MLA_H * KV_LORA), F32),
        compiler_params=pltpu.CompilerParams(dimension_semantics=("arbitrary", "arbitrary"),
                                             vmem_limit_bytes=VMEM_LIMIT),
        name="attn_sample",
    )(pt, qlat, qh, ckvn, krn, pool_ckv, pool_kr)


def _sigmoid(x):
    return 0.5 * jnp.tanh(0.5 * x) + 0.5


def _silu(x):
    h = 0.5 * x
    return h * jnp.tanh(h) + h


def _back_kernel(absorb, x_ref, gate_ref, hml_ref, mo_ref, mz_ref, o_ref, az_ref, gn_ref, wuv_ref,
                 wout_ref, lng_ref, lnb_ref, y_ref):
    tb, ts, d = x_ref.shape
    tm = tb * ts
    hm = hml_ref[...].astype(F32) * _sigmoid(mo_ref[...].astype(F32))
    parts = []
    for hh in range(ML_H):
        t = hm[:, hh * ML_DV:(hh + 1) * ML_DV]
        mu = jnp.mean(t, axis=-1, keepdims=True)
        tc = t - mu
        var = jnp.mean(tc * tc, axis=-1, keepdims=True)
        parts.append(tc * lax.rsqrt(var + EPS))
    mz = mz_ref[...].astype(F32)
    y_ml = jnp.concatenate(parts, axis=1) * gn_ref[...] * _silu(mz)
    if absorb:
        olat = o_ref[...].astype(BF16)
        o_mla = jnp.concatenate(
            [_mm(olat[:, p * 2 * KV_LORA:(p + 1) * 2 * KV_LORA], wuv_ref[p]) for p in range(MLA_H // 2)],
            axis=1)
    else:
        o_mla = o_ref[...].astype(F32)
    az = az_ref[...].astype(F32)
    y_mla = o_mla * _silu(az)
    out = _mm(y_ml.astype(BF16), wout_ref[0:ML_W, :]) + _mm(y_mla.astype(BF16), wout_ref[ML_W:, :])
    z = ALPHA * x_ref[...] + gate_ref[...] * out.reshape(tb, ts, d)
    mu = jnp.mean(z, axis=-1, keepdims=True)
    zc = z - mu
    var = jnp.mean(zc * zc, axis=-1, keepdims=True)
    y_ref[...] = zc * lax.rsqrt(var + EPS) * lng_ref[...] + lnb_ref[...]


def _back(absorb, x, mod, hml, mo, mz, o, az, wts):
    b, s, d = x.shape
    n = b * s
    tm = BACK_TM
    if absorb:
        tb, ts = tm // s, s
        x_map = lambda i: (i, 0, 0)
        gate_map = lambda i: (i, 0, 2)
    else:
        tb, ts = 1, tm
        spb = s // tm
        x_map = lambda i: (i // spb, i % spb, 0)
        gate_map = lambda i: (i // spb, 0, 2)
    const2 = lambda i: (0, 0)
    tok = lambda w: pl.BlockSpec((tm, w), lambda i: (i, 0))
    ow = o.shape[1]
    return pl.pallas_call(
        functools.partial(_back_kernel, absorb),
        grid=(n // tm,),
        in_specs=[pl.BlockSpec((tb, ts, d), x_map),
                  pl.BlockSpec((tb, 1, d), gate_map),
                  tok(ML_W), tok(ML_W), tok(ML_W), tok(ow), tok(MLA_W),
                  pl.BlockSpec((1, ML_W), const2),
                  pl.BlockSpec((MLA_H // 2, 2 * KV_LORA, LANES), lambda i: (0, 0, 0)),
                  pl.BlockSpec((d, d), const2),
                  pl.BlockSpec((1, d), const2),
                  pl.BlockSpec((1, d), const2)],
        out_specs=pl.BlockSpec((tb, ts, d), x_map),
        out_shape=jax.ShapeDtypeStruct((b, s, d), F32),
        compiler_params=pltpu.CompilerParams(dimension_semantics=("arbitrary",),
                                             vmem_limit_bytes=VMEM_LIMIT),
        name="back_sample" if absorb else "back_prompt",
    )(x, mod, hml, mo, mz, o, az, wts["gn"], wts["wuv_pair"], wts["w_out"], wts["ln_g"], wts["ln_b"])


def _prep_weights(l, w_in, ml_b_i, ml_b_f, ml_gn, mla_q_norm, mla_kv_norm, mla_w_uq, mla_w_uk,
                  mla_w_uv, w_out, ln_g, ln_b):
    assert w_in.shape[2] == N_IN
    tail = HEAD_PAD - MLA_NOPE - MLA_ROPE
    w_in_p = _wprep(jnp.swapaxes(w_in[l], 0, 1))
    b_gate = jnp.concatenate([ml_b_i[l], ml_b_f[l], jnp.zeros((LANES - 2 * ML_H,), F32)]).reshape(1, LANES)
    uq = mla_w_uq[l].reshape(Q_LORA, MLA_H, MLA_NOPE + MLA_ROPE)
    nope, r1, r2 = uq[..., :MLA_NOPE], uq[..., MLA_NOPE:MLA_NOPE + ROPE_HALF], uq[..., MLA_NOPE + ROPE_HALF:]
    zq = lambda c: jnp.zeros((Q_LORA, MLA_H, c), F32)
    wqa = jnp.concatenate([nope, r1, r2, zq(tail)], axis=-1).reshape(Q_LORA, MLA_H * HEAD_PAD).astype(BF16)
    uk = mla_w_uk[l].reshape(KV_LORA, MLA_H, MLA_NOPE)
    wk_pad = jnp.concatenate([uk, jnp.zeros((KV_LORA, MLA_H, HEAD_PAD - MLA_NOPE), F32)],
                             axis=-1).reshape(KV_LORA, MLA_H * HEAD_PAD).astype(BF16)
    wukt = jnp.concatenate([jnp.transpose(uk, (1, 2, 0)),
                            jnp.zeros((MLA_H, HEAD_PAD - MLA_NOPE, KV_LORA), F32)], axis=1).astype(BF16)
    uv = mla_w_uv[l].reshape(KV_LORA, MLA_H // 2, 2, MLA_VD)
    zv = jnp.zeros((KV_LORA, MLA_H // 2, MLA_VD), F32)
    wuv_pair = jnp.concatenate(
        [jnp.concatenate([uv[:, :, 0], zv], axis=-1), jnp.concatenate([zv, uv[:, :, 1]], axis=-1)],
        axis=0)
    wuv_pair = jnp.transpose(wuv_pair, (1, 0, 2)).astype(BF16)
    return dict(w_in=w_in_p, b_gate=b_gate, q_norm=mla_q_norm[l].reshape(1, -1),
                kv_norm=mla_kv_norm[l].reshape(1, -1), wqa=wqa, wk_pad=wk_pad, wukt=wukt,
                w_uv_t=mla_w_uv[l].T.astype(BF16), wuv_pair=wuv_pair, gn=ml_gn[l].reshape(1, -1),
                w_out=w_out[l].astype(BF16), ln_g=ln_g[l].reshape(1, -1), ln_b=ln_b[l].reshape(1, -1))


def _rope_tables(pos):
    f32 = np.float32
    inv = ROPE_THETA ** (-np.arange(ROPE_HALF, dtype=np.float64) / ROPE_HALF)
    ang = pos.astype(np.float64)[:, None] * inv[None, :]
    n = pos.shape[0]
    tail = HEAD_PAD - MLA_NOPE - MLA_ROPE
    cos, sin = np.cos(ang).astype(f32), np.sin(ang).astype(f32)
    zeros = lambda c: np.zeros((n, c), f32)
    cm = np.concatenate([np.ones((n, MLA_NOPE), f32), cos, cos, zeros(tail)], 1)
    s_lo = np.concatenate([zeros(MLA_NOPE), -sin, zeros(ROPE_HALF), zeros(tail)], 1)
    s_hi = np.concatenate([zeros(MLA_NOPE), zeros(ROPE_HALF), sin, zeros(tail)], 1)
    return np.concatenate([cm, s_lo, s_hi], 1)


def kernel(x_prompt, x_sample, c_prompt, c_sample, cache_ckv, cache_krope, state_C, state_n, state_m,
           page_table, w_ada, b_ada, w_in, ml_b_i, ml_b_f, ml_gn, mla_q_norm, mla_kv_norm,
           mla_w_uq, mla_w_uk, mla_w_uv, w_out, ln_g, ln_b):
    bp, sp, d = x_prompt.shape
    bs, ss, _ = x_sample.shape
    past = page_table.shape[1] * PAGE_SIZE
    tab_p = jnp.asarray(_rope_tables(np.arange(sp)))
    tab_s = jnp.asarray(np.tile(_rope_tables(past + np.arange(ss)), (FRONT_TM // ss, 1)))
    pad = (-(bp + bs)) % 8
    c_all = jnp.concatenate([c_prompt, c_sample, jnp.zeros((pad, d), F32)], axis=0)

    xp, xs = x_prompt, x_sample
    outs = [[] for _ in range(10)]
    for l in range(DEPTH):
        wts = _prep_weights(l, w_in, ml_b_i, ml_b_f, ml_gn, mla_q_norm, mla_kv_norm, mla_w_uq, mla_w_uk,
                            mla_w_uv, w_out, ln_g, ln_b)
        mod = _ada(c_all, w_ada[l], b_ada[l])
        mod_p = mod[:bp].reshape(bp, 1, 3 * d)
        mod_s = mod[bp:bp + bs].reshape(bs, 1, 3 * d)

        mq, mk, mv, g, mo, mz, az, ckv, kr, qh, kh, vt = _front(False, xp, mod_p, tab_p, wts)
        seq3 = lambda t: t.reshape(bp, sp, -1)
        c0 = jnp.zeros((bp, ML_H, ML_DK, ML_DV), F32)
        n0 = jnp.zeros((bp, ML_H, 1, ML_DK), F32)
        m0 = jnp.full((bp, ML_H, 1, LANES), -jnp.inf, F32)
        hml, c_p, n_p, m_p = _mlstm(seq3(mq), seq3(mk), seq3(mv), seq3(g), c0, n0, m0, ML_CHUNK, bp)
        o_p = _attn_prompt(qh, kh, vt)
        yp = _back(False, xp, mod_p, hml.reshape(bp * sp, -1), mo, mz, o_p.reshape(bp * sp, -1), az, wts)
        outs[0].append(ckv.reshape(bp, sp, KV_LORA))
        outs[1].append(kr.reshape(bp, sp, MLA_ROPE))
        outs[2].append(c_p)
        outs[3].append(n_p[:, :, 0, :])
        outs[4].append(m_p[:, :, 0, 0])

        mq, mk, mv, g, mo, mz, az, ckv, kr, qlat, qh, krp = _front(True, xs, mod_s, tab_s, wts)
        seq3 = lambda t: t.reshape(bs, ss, -1)
        n0 = state_n[l].astype(F32).reshape(bs, ML_H, 1, ML_DK)
        m0 = jnp.broadcast_to(state_m[l].astype(F32)[:, :, None, None], (bs, ML_H, 1, LANES))
        hml, c_s, n_s, m_s = _mlstm(seq3(mq), seq3(mk), seq3(mv), seq3(g), state_C[l].astype(F32), n0, m0, ss, 8)
        o_s = _attn_sample(seq3(qlat), seq3(qh), seq3(ckv), seq3(krp), cache_ckv[l],
                           jnp.swapaxes(cache_krope[l], 1, 2), page_table)
        ys = _back(True, xs, mod_s, hml.reshape(bs * ss, -1), mo, mz,
                   o_s.reshape(bs * ss, MLA_H * KV_LORA), az, wts)
        outs[5].append(ckv.reshape(bs, ss, KV_LORA).astype(cache_ckv.dtype))
        outs[6].append(kr.reshape(bs, ss, MLA_ROPE).astype(cache_krope.dtype))
        outs[7].append(c_s.astype(state_C.dtype))
        outs[8].append(n_s[:, :, 0, :].astype(state_n.dtype))
        outs[9].append(m_s[:, :, 0, 0].astype(state_m.dtype))
        xp, xs = yp, ys
    return (xp, xs) + tuple(jnp.stack(o) for o in outs)
```

```python
import functools

import numpy as np
import jax
import jax.numpy as jnp
from jax import lax
from jax.experimental import pallas as pl
from jax.experimental.pallas import tpu as pltpu

F32 = jnp.float32
BF16 = jnp.bfloat16

D_MODEL = 1024
DEPTH = 1
PAGE_SIZE = 128
ML_W = 512
MLA_W = 512
ML_DK = 128
ML_DV = 128
ML_H = 4
MLA_VD = 64
MLA_H = 8
MLA_NOPE = 64
MLA_ROPE = 32
ROPE_HALF = MLA_ROPE // 2
MLA_SCALE = (MLA_NOPE + MLA_ROPE) ** -0.5
Q_SCALE = MLA_SCALE * float(np.log2(np.e))
Q_LORA = 384
KV_LORA = 256
ROPE_THETA = 10000.0
ML_CHUNK = 128
EPS = 1e-6
ALPHA = (2.0 * DEPTH) ** 0.25

LANES = 128
HEAD_PAD = 128
VMEM_LIMIT = 48 * 1024 * 1024

_SEG = {}
_off = 0
for _name, _w in (("mq", 512), ("mk", 512), ("mv", 512), ("gk", 256), ("mo", 512), ("mz", 512),
                  ("cq", Q_LORA), ("ckv", KV_LORA), ("az", 512)):
    _SEG[_name] = (_off, _off + _w)
    _off += _w
N_IN_PAD = _off

FRONT_TM = 512
BACK_TM = 512
ATT_TQ = 1024
ATT_TK = 2048
ATT_KC = 256
MASKED_MAX_FLOOR = -1e30
DEC_PG = 64
DEC_GROUPS = 8
DEC_SLOTS = 2


def _nt(a, b):
    return lax.dot_general(a, b, (((1,), (1,)), ((), ())), preferred_element_type=F32)


def _tn(a, b):
    return lax.dot_general(a, b, (((0,), (0,)), ((), ())), preferred_element_type=F32)


def _mm(a, b):
    return jnp.dot(a, b, preferred_element_type=F32)


def _mm_exact(a, b):
    return jnp.dot(a, b, preferred_element_type=F32, precision=lax.Precision.HIGHEST)


def _nt_exact(a, b):
    return lax.dot_general(a, b, (((1,), (1,)), ((), ())), preferred_element_type=F32,
                           precision=lax.Precision.HIGHEST)


def _ada_kernel(c_ref, w_ref, b_ref, o_ref):
    o_ref[...] = _mm(c_ref[...].astype(BF16), w_ref[...].astype(BF16)) + b_ref[...]


def _ada(c_all, w_ada, b_ada):
    m = c_all.shape[0]
    tn = 1024
    return pl.pallas_call(
        _ada_kernel,
        grid=(3 * D_MODEL // tn,),
        in_specs=[pl.BlockSpec((m, D_MODEL), lambda j: (0, 0)),
                  pl.BlockSpec((D_MODEL, tn), lambda j: (0, j)),
                  pl.BlockSpec((1, tn), lambda j: (0, j))],
        out_specs=pl.BlockSpec((m, tn), lambda j: (0, j)),
        out_shape=jax.ShapeDtypeStruct((m, 3 * D_MODEL), F32),
        name="ada",
    )(c_all, w_ada, b_ada.reshape(1, -1))


_IN_ROWS = {}
_off = 0
for _name, _w in (("mq", 512), ("mk", 512), ("mv", 512), ("gates", 2 * ML_H), ("mo", 512), ("mz", 512),
                  ("cq", Q_LORA), ("ckv", KV_LORA), ("kr", MLA_ROPE), ("az", 512)):
    _IN_ROWS[_name] = (_off, _off + _w)
    _off += _w
N_IN = _off


def _wprep_kernel(wt_ref, o_ref):
    dk = wt_ref.shape[1]
    rows = lambda name: wt_ref[_IN_ROWS[name][0]:_IN_ROWS[name][1], :]
    zeros = lambda r: jnp.zeros((r, dk), F32)
    tail = HEAD_PAD - MLA_NOPE - MLA_ROPE
    gk = jnp.concatenate([rows("gates"), zeros(LANES - 2 * ML_H), zeros(MLA_NOPE), rows("kr"), zeros(tail)], axis=0)
    for name in _SEG:
        lo, hi = _SEG[name]
        o_ref[:, lo:hi] = (gk if name == "gk" else rows(name)).T.astype(BF16)


def _wprep(w_in_t):
    n_in, d = w_in_t.shape
    dk = 256
    return pl.pallas_call(
        _wprep_kernel,
        grid=(d // dk,),
        in_specs=[pl.BlockSpec((n_in, dk), lambda i: (0, i))],
        out_specs=pl.BlockSpec((dk, N_IN_PAD), lambda i: (i, 0)),
        out_shape=jax.ShapeDtypeStruct((d, N_IN_PAD), BF16),
        compiler_params=pltpu.CompilerParams(dimension_semantics=("arbitrary",),
                                             vmem_limit_bytes=VMEM_LIMIT),
        name="wprep",
    )(w_in_t)


def _rms(x, g):
    return x * lax.rsqrt(jnp.mean(x * x, axis=-1, keepdims=True) + EPS) * g


def _log_sigmoid(x):
    return jnp.minimum(x, 0.0) - jnp.log1p(jnp.exp(-jnp.abs(x)))


def _rotary(a, tab_ref):
    cm = tab_ref[:, 0:LANES]
    s_lo = tab_ref[:, LANES:2 * LANES]
    s_hi = tab_ref[:, 2 * LANES:3 * LANES]
    return a * cm + pltpu.roll(a, LANES - ROPE_HALF, 1) * s_lo + pltpu.roll(a, ROPE_HALF, 1) * s_hi


def _front_kernel(absorb, x_ref, sh_ref, sc_ref, tab_ref, win_ref, bg_ref, qn_ref, kvn_ref,
                  wqa_ref, wk_ref, wv_ref,
                  mq_ref, mk_ref, mv_ref, g_ref, mo_ref, mz_ref, az_ref, ckv_ref, kr_ref,
                  o1_ref, o2_ref, o3_ref):
    tb, ts, d = x_ref.shape
    tm = tb * ts
    h = x_ref[...] * (1.0 + sc_ref[...]) + sh_ref[...]
    h = h.reshape(tm, d).astype(BF16)

    def seg(name):
        lo, hi = _SEG[name]
        return _mm(h, win_ref[:, lo:hi])

    mq_ref[...] = seg("mq").astype(BF16)
    mk_ref[...] = (seg("mk") * (ML_DK ** -0.5)).astype(BF16)
    mv_ref[...] = seg("mv").astype(BF16)
    gk = seg("gk")
    gz = gk[:, :LANES] + bg_ref[...]
    lane = lax.broadcasted_iota(jnp.int32, gz.shape, 1)
    g_ref[...] = jnp.where(lane < ML_H, gz, jnp.where(lane < 2 * ML_H, _log_sigmoid(gz), 0.0))
    mo_ref[...] = seg("mo").astype(BF16)
    mz_ref[...] = seg("mz").astype(BF16)
    az_ref[...] = seg("az").astype(BF16)

    ckvn = _rms(seg("ckv"), kvn_ref[...])
    ckv_ref[...] = ckvn
    ckvn_b = ckvn.astype(BF16)
    krp = _rotary(gk[:, LANES:], tab_ref)
    kr_ref[...] = krp[:, MLA_NOPE:MLA_NOPE + MLA_ROPE]

    cqn = _rms(seg("cq"), qn_ref[...]).astype(BF16)
    qa = _mm(cqn, wqa_ref[...])
    for hh in range(MLA_H):
        sl = slice(hh * HEAD_PAD, (hh + 1) * HEAD_PAD)
        qh = (_rotary(qa[:, sl], tab_ref) * Q_SCALE).astype(BF16)
        if absorb:
            o1_ref[:, hh * KV_LORA:(hh + 1) * KV_LORA] = _mm(qh, wk_ref[hh]).astype(BF16)
            o2_ref[:, sl] = qh
        else:
            o1_ref[0, hh] = qh
            o2_ref[0, hh] = (_mm(ckvn_b, wk_ref[:, sl]) + krp).astype(BF16)
    if absorb:
        o3_ref[...] = krp.astype(BF16)
    else:
        o3_ref[0] = _nt(wv_ref[...], ckvn_b).astype(BF16)


def _front(absorb, x, mod, tab, wts):
    b, s, d = x.shape
    n = b * s
    tm = FRONT_TM
    if absorb:
        tb, ts = tm // s, s
        x_map = lambda i: (i, 0, 0)
        mod_map = lambda k: (lambda i: (i, 0, k))
        tab_map = lambda i: (0, 0)
    else:
        tb, ts = 1, tm
        spb = s // tm
        x_map = lambda i: (i // spb, i % spb, 0)
        mod_map = lambda k: (lambda i: (i // spb, 0, k))
        tab_map = lambda i: (i % spb, 0)
    const2 = lambda i: (0, 0)
    tok = lambda w: pl.BlockSpec((tm, w), lambda i: (i, 0))
    in_specs = [
        pl.BlockSpec((tb, ts, d), x_map),
        pl.BlockSpec((tb, 1, d), mod_map(0)),
        pl.BlockSpec((tb, 1, d), mod_map(1)),
        pl.BlockSpec((tm, 3 * LANES), tab_map),
        pl.BlockSpec((d, N_IN_PAD), const2),
        pl.BlockSpec((1, LANES), const2),
        pl.BlockSpec((1, Q_LORA), const2),
        pl.BlockSpec((1, KV_LORA), const2),
        pl.BlockSpec((Q_LORA, MLA_H * HEAD_PAD), const2),
    ]
    out_specs = [tok(512), tok(512), tok(512), tok(LANES), tok(512), tok(512), tok(512),
                 tok(KV_LORA), tok(MLA_ROPE)]
    out_shape = [jax.ShapeDtypeStruct((n, w), dt)
                 for w, dt in ((512, BF16), (512, BF16), (512, BF16), (LANES, F32), (512, BF16), (512, BF16),
                               (512, BF16), (KV_LORA, F32), (MLA_ROPE, F32))]
    if absorb:
        in_specs += [pl.BlockSpec((MLA_H, HEAD_PAD, KV_LORA), lambda i: (0, 0, 0)),
                     pl.BlockSpec((MLA_W, KV_LORA), const2)]
        out_specs += [tok(MLA_H * KV_LORA), tok(MLA_H * HEAD_PAD), tok(HEAD_PAD)]
        out_shape += [jax.ShapeDtypeStruct((n, MLA_H * KV_LORA), BF16),
                      jax.ShapeDtypeStruct((n, MLA_H * HEAD_PAD), BF16),
                      jax.ShapeDtypeStruct((n, HEAD_PAD), BF16)]
        wk = wts["wukt"]
    else:
        head_map = lambda i: (i // spb, 0, i % spb, 0)
        in_specs += [pl.BlockSpec((KV_LORA, MLA_H * HEAD_PAD), const2),
                     pl.BlockSpec((MLA_W, KV_LORA), const2)]
        out_specs += [pl.BlockSpec((1, MLA_H, tm, HEAD_PAD), head_map),
                      pl.BlockSpec((1, MLA_H, tm, HEAD_PAD), head_map),
                      pl.BlockSpec((1, MLA_W, tm), lambda i: (i // spb, 0, i % spb))]
        out_shape += [jax.ShapeDtypeStruct((b, MLA_H, s, HEAD_PAD), BF16),
                      jax.ShapeDtypeStruct((b, MLA_H, s, HEAD_PAD), BF16),
                      jax.ShapeDtypeStruct((b, MLA_W, s), BF16)]
        wk = wts["wk_pad"]
    return pl.pallas_call(
        functools.partial(_front_kernel, absorb),
        grid=(n // tm,),
        in_specs=in_specs,
        out_specs=out_specs,
        out_shape=out_shape,
        compiler_params=pltpu.CompilerParams(dimension_semantics=("arbitrary",),
                                             vmem_limit_bytes=VMEM_LIMIT),
        name="front_sample" if absorb else "front_prompt",
    )(x, mod, mod, tab, wts["w_in"], wts["b_gate"], wts["q_norm"], wts["kv_norm"],
      wts["wqa"], wk, wts["w_uv_t"])


def _mlstm_kernel(q_ref, k_ref, v_ref, g_ref, c0_ref, n0_ref, m0_ref,
                  h_ref, c_ref, n_ref, m_ref):
    tb, L, _ = q_ref.shape
    ci = pl.program_id(1)

    @pl.when(ci == 0)
    def _():
        c_ref[...] = c0_ref[...]
        n_ref[...] = n0_ref[...]
        m_ref[...] = m0_ref[...]

    row = lax.broadcasted_iota(jnp.int32, (L, L), 0)
    col = lax.broadcasted_iota(jnp.int32, (L, L), 1)
    causal = col <= row
    tril = causal.astype(F32)
    sel = (lax.broadcasted_iota(jnp.int32, (8, LANES), 0)
           == lax.broadcasted_iota(jnp.int32, (8, LANES), 1)).astype(F32)

    chains = [(t, hh) for t in range(tb) for hh in range(ML_H)]
    sl = lambda hh: slice(hh * ML_DK, (hh + 1) * ML_DK)
    gates = {}
    for t in range(tb):
        g = g_ref[t]
        fcum = _mm_exact(tril, g)
        g_rows = _nt_exact(sel, g)
        f_rows = _nt_exact(sel, fcum)
        gates[t] = (g, fcum, g_rows, f_rows)

    qk, qc, c_prevs = {}, {}, {}
    for t, hh in chains:
        qb = q_ref[t, :, sl(hh)]
        c_prevs[t, hh] = c_ref[t, hh]
        qk[t, hh] = _nt(qb, k_ref[t, :, sl(hh)])
        qc[t, hh] = _mm(qb, c_prevs[t, hh].astype(BF16))

    stab = {}
    for t, hh in chains:
        g, fcum, g_rows, f_rows = gates[t]
        m_prev = m_ref[t, hh][:, :1]
        f_col = fcum[:, ML_H + hh:ML_H + hh + 1]
        f_row = f_rows[ML_H + hh:ML_H + hh + 1, :]
        ig_row = g_rows[hh:hh + 1, :]
        dmat = jnp.where(causal, f_col - f_row + ig_row, -jnp.inf)
        m_inter = jnp.broadcast_to(f_col, (L, LANES)) + m_prev
        m_t = jnp.maximum(m_inter, jnp.max(dmat, axis=-1, keepdims=True))
        stab[t, hh] = (jnp.exp(dmat - m_t[:, :L]), jnp.exp(m_inter - m_t), m_t)

    svs = {}
    ones_cols = jnp.ones((L, LANES), BF16)
    for t, hh in chains:
        sq = (qk[t, hh] * stab[t, hh][0]).astype(BF16)
        svs[t, hh] = _mm(sq, jnp.concatenate([v_ref[t, :, sl(hh)], ones_cols], axis=1))

    for t, hh in chains:
        _, a, m_t = stab[t, hh]
        n_rep = jnp.broadcast_to(n_ref[t, hh], (LANES, ML_DK)).astype(BF16)
        qn = _nt(q_ref[t, :, sl(hh)], n_rep)
        num = svs[t, hh][:, :ML_DV] + a * qc[t, hh]
        den = svs[t, hh][:, ML_DV:] + a * qn
        h_ref[t, :, sl(hh)] = (num / jnp.maximum(jnp.abs(den), jnp.exp(-m_t))).astype(h_ref.dtype)

    for t, hh in chains:
        w, a, m_t = stab[t, hh]
        wl_row = w[L - 1:L, :]
        al = a[L - 1:L, :]
        kb = k_ref[t, :, sl(hh)]
        kw_t = (kb.astype(F32).T * wl_row).astype(BF16)
        c_ref[t, hh] = al[:, :1] * c_prevs[t, hh] + _mm(kw_t, v_ref[t, :, sl(hh)])
        n_ref[t, hh] = al * n_ref[t, hh] + _mm(w[L - 8:L, :].astype(BF16), kb)[7:8, :]
        m_ref[t, hh] = m_t[L - 1:L, :]


def _mlstm(q, k, v, g, c0, n0, m0, chunk, tb):
    b, s, w = q.shape
    nc = s // chunk
    seq = lambda ww: pl.BlockSpec((tb, chunk, ww), lambda i, c: (i, c, 0))
    st_c = pl.BlockSpec((tb, ML_H, ML_DK, ML_DV), lambda i, c: (i, 0, 0, 0))
    st_v = pl.BlockSpec((tb, ML_H, 1, LANES), lambda i, c: (i, 0, 0, 0))
    return pl.pallas_call(
        _mlstm_kernel,
        grid=(b // tb, nc),
        in_specs=[seq(w), seq(w), seq(w), seq(LANES), st_c, st_v, st_v],
        out_specs=[seq(w), st_c, st_v, st_v],
        out_shape=[jax.ShapeDtypeStruct((b, s, w), BF16),
                   jax.ShapeDtypeStruct((b, ML_H, ML_DK, ML_DV), F32),
                   jax.ShapeDtypeStruct((b, ML_H, 1, LANES), F32),
                   jax.ShapeDtypeStruct((b, ML_H, 1, LANES), F32)],
        compiler_params=pltpu.CompilerParams(dimension_semantics=("arbitrary", "arbitrary"),
                                             vmem_limit_bytes=VMEM_LIMIT),
        name="mlstm",
    )(q, k, v, g, c0, n0, m0)


def _attn_kernel(qi_ref, ki_ref, last_ref, q_ref, k_ref, vt_ref, o_ref, m_sc, l_sc, acc_sc):
    step = pl.program_id(2)
    qi = qi_ref[step]
    ki = ki_ref[step]
    tq = q_ref.shape[2]
    tk = k_ref.shape[2]

    @pl.when(ki == 0)
    def _():
        m_sc[...] = jnp.full(m_sc.shape, -jnp.inf, F32)
        l_sc[...] = jnp.zeros(l_sc.shape, F32)
        acc_sc[...] = jnp.zeros(acc_sc.shape, F32)

    def tile(diag):
        chains = [(hh, c0) for hh in range(2) for c0 in range(0, min(tk, diag + tq), ATT_KC)]
        scores = {}
        for hh, c0 in chains:
            st = _nt(k_ref[0, hh, c0:c0 + ATT_KC, :], q_ref[0, hh])
            if c0 >= diag:
                keys = c0 - diag + lax.broadcasted_iota(jnp.int32, (ATT_KC, tq), 0)
                qrys = lax.broadcasted_iota(jnp.int32, (ATT_KC, tq), 1)
                st = jnp.where(keys <= qrys, st, -jnp.inf)
            scores[hh, c0] = st
        parts = ([], [])
        ones_rows = jnp.ones((16, ATT_KC), BF16)
        for hh, c0 in chains:
            st = scores[hh, c0]
            m_g = jnp.max(st, axis=0, keepdims=True)
            if c0 >= diag:
                m_g = jnp.maximum(m_g, MASKED_MAX_FLOOR)
            pt = jnp.exp2(st - m_g).astype(BF16)
            rows = slice(hh * MLA_VD, (hh + 1) * MLA_VD)
            v_ones = jnp.concatenate([vt_ref[0, rows, c0:c0 + ATT_KC], ones_rows], axis=0)
            o_g = _mm(v_ones, pt)
            parts[hh].append((m_g, o_g[MLA_VD:MLA_VD + 1, :], o_g[:MLA_VD, :]))
        for hh in range(2):
            rows = slice(hh * MLA_VD, (hh + 1) * MLA_VD)
            m_prev = m_sc[hh]
            m_new = functools.reduce(jnp.maximum, [m_g for m_g, _, _ in parts[hh]], m_prev)
            alpha = jnp.exp2(m_prev - m_new)
            l_new = alpha * l_sc[hh]
            acc = alpha * acc_sc[rows, :]
            for m_g, l_g, o_g in parts[hh]:
                w_g = jnp.exp2(m_g - m_new)
                l_new = l_new + w_g * l_g
                acc = acc + w_g * o_g
            m_sc[hh] = m_new
            l_sc[hh] = l_new
            acc_sc[rows, :] = acc

    first_query = qi * tq - ki * tk
    for diag in range(0, tk, tq):
        pl.when(first_query == diag)(functools.partial(tile, diag))
    pl.when(first_query >= tk)(functools.partial(tile, tk))

    @pl.when(last_ref[step] == 1)
    def _():
        out_t = jnp.concatenate([acc_sc[0:MLA_VD, :] / l_sc[0], acc_sc[MLA_VD:, :] / l_sc[1]], axis=0)
        o_ref[0] = out_t.T.astype(o_ref.dtype)


def _attn_prompt(qh, kh, vt):
    b, nh, s, _ = qh.shape
    tq, tk = ATT_TQ, ATT_TK
    assert tk % tq == 0 and tq % ATT_KC == 0 and s % tk == 0
    qi_l, ki_l, last_l = [], [], []
    for qi in range(s // tq):
        nk = ((qi + 1) * tq + tk - 1) // tk
        for ki in range(nk):
            qi_l.append(qi)
            ki_l.append(ki)
            last_l.append(int(ki == nk - 1))
    nsteps = len(qi_l)
    sched = [jnp.asarray(np.asarray(a, np.int32)) for a in (qi_l, ki_l, last_l)]
    grid_spec = pltpu.PrefetchScalarGridSpec(
        num_scalar_prefetch=3,
        grid=(b, nh // 2, nsteps),
        in_specs=[pl.BlockSpec((1, 2, tq, HEAD_PAD), lambda bi, hp, st, qi, ki, la: (bi, hp, qi[st], 0)),
                  pl.BlockSpec((1, 2, tk, HEAD_PAD), lambda bi, hp, st, qi, ki, la: (bi, hp, ki[st], 0)),
                  pl.BlockSpec((1, 2 * MLA_VD, tk), lambda bi, hp, st, qi, ki, la: (bi, hp, ki[st]))],
        out_specs=pl.BlockSpec((1, tq, LANES), lambda bi, hp, st, qi, ki, la: (bi, qi[st], hp)),
        scratch_shapes=[pltpu.VMEM((2, 1, tq), F32), pltpu.VMEM((2, 1, tq), F32),
                        pltpu.VMEM((2 * MLA_VD, tq), F32)],
    )
    return pl.pallas_call(
        _attn_kernel,
        grid_spec=grid_spec,
        out_shape=jax.ShapeDtypeStruct((b, s, nh * MLA_VD), BF16),
        compiler_params=pltpu.CompilerParams(
            dimension_semantics=("arbitrary", "arbitrary", "arbitrary"),
            vmem_limit_bytes=VMEM_LIMIT),
        name="attn_prompt",
    )(*sched, qh, kh, vt)


def _dec_page_copies(pt_ref, pool_ckv, pool_kr, cbuf, kbuf, sems, step, slot):
    copies = []
    for i in range(DEC_PG):
        page = pt_ref[step * DEC_PG + i]
        keys = pl.ds(i * PAGE_SIZE, PAGE_SIZE)
        copies.append(pltpu.make_async_copy(pool_ckv.at[page], cbuf.at[slot, keys, :], sems.at[slot, 0]))
        copies.append(pltpu.make_async_copy(pool_kr.at[page], kbuf.at[slot, :, keys], sems.at[slot, 1]))
    return copies


def _dec_wait_slot(cbuf, kbuf, sems, slot):
    pltpu.make_async_copy(cbuf.at[slot], cbuf.at[slot], sems.at[slot, 0]).wait()
    pltpu.make_async_copy(kbuf.at[slot], kbuf.at[slot], sems.at[slot, 1]).wait()


def _dec_kernel(pt_ref, qlat_ref, qh_ref, ckvn_ref, krn_ref, pool_ckv, pool_kr,
                o_ref, m_sc, l_sc, acc_sc, padc_sc, padk_sc, cbuf, kbuf, sems):
    j = pl.program_id(1)
    nj = pl.num_programs(1)
    step = pl.program_id(0) * nj + j
    last_step = pl.num_programs(0) * nj - 1
    slot = lax.rem(step, DEC_SLOTS)
    qlat = jnp.concatenate([qlat_ref[0, :, hh * KV_LORA:(hh + 1) * KV_LORA] for hh in range(MLA_H)], axis=0)
    qh = jnp.concatenate([qh_ref[0, :, hh * HEAD_PAD:(hh + 1) * HEAD_PAD] for hh in range(MLA_H)], axis=0)
    ds = ckvn_ref.shape[1]
    copies = functools.partial(_dec_page_copies, pt_ref, pool_ckv, pool_kr, cbuf, kbuf, sems)

    @pl.when(step == 0)
    def _():
        for s0 in range(DEC_SLOTS):
            for idx, c in enumerate(copies(s0, s0)):
                c.start(priority=idx % 2)

    _dec_wait_slot(cbuf, kbuf, sems, slot)
    next_copies = copies(step + DEC_SLOTS, slot)

    @pl.when(j == 0)
    def _():
        padc_sc[...] = jnp.zeros(padc_sc.shape, BF16)
        padk_sc[...] = jnp.zeros(padk_sc.shape, BF16)
        padc_sc[0:ds, :] = ckvn_ref[0].astype(BF16)
        padk_sc[0:ds, :] = krn_ref[0]
        cn = padc_sc[...]
        s = _nt(qlat, cn) + _nt(qh, padk_sc[...])
        tok = lax.broadcasted_iota(jnp.int32, (MLA_H, ds, s.shape[1]), 1).reshape(s.shape)
        key = lax.broadcasted_iota(jnp.int32, s.shape, 1)
        s = jnp.where(key <= tok, s, -jnp.inf)
        m = jnp.max(s, axis=-1, keepdims=True)
        p = jnp.exp2(s - m)
        m_sc[...] = m
        l_sc[...] = jnp.sum(p, axis=-1, keepdims=True)
        acc_sc[...] = _mm(p.astype(BF16), cn)

    qr = qh[:, MLA_NOPE:MLA_NOPE + MLA_ROPE]
    pages = cbuf[slot].astype(BF16)
    krt = kbuf[slot].astype(BF16)
    gk = pages.shape[0] // DEC_GROUPS
    groups = [pages[g * gk:(g + 1) * gk] for g in range(DEC_GROUPS)]
    scores, probs, parts = {}, {}, []

    def score(g):
        scores[g] = _nt(qlat, groups[g]) + _mm(qr, krt[:, g * gk:(g + 1) * gk])

    def soft(g):
        m_g = jnp.max(scores[g], axis=-1, keepdims=True)
        p = jnp.exp2(scores[g] - m_g)
        probs[g] = (m_g, jnp.sum(p, axis=-1, keepdims=True), p.astype(BF16))

    def value(g):
        m_g, l_g, pb = probs[g]
        parts.append((m_g, l_g, _mm(pb, groups[g])))

    per_group = len(next_copies) // DEC_GROUPS
    for g in range(DEC_GROUPS):
        score(g)
    soft(0)
    for g in range(DEC_GROUPS):
        if g + 1 < DEC_GROUPS:
            soft(g + 1)
        value(g)
        for idx, c in enumerate(next_copies[g * per_group:(g + 1) * per_group]):
            c.start(priority=idx % 2)
    m_prev = m_sc[...]
    m_new = functools.reduce(jnp.maximum, [m_g for m_g, _, _ in parts], m_prev)
    alpha = jnp.exp2(m_prev - m_new)
    l_new = alpha * l_sc[...]
    acc = alpha * acc_sc[...]
    for m_g, l_g, o_g in parts:
        w_g = jnp.exp2(m_g - m_new)
        l_new = l_new + w_g * l_g
        acc = acc + w_g * o_g
    m_sc[...] = m_new
    l_sc[...] = l_new
    acc_sc[...] = acc

    @pl.when(j == nj - 1)
    def _():
        out = acc_sc[...] / l_sc[...]
        for hh in range(MLA_H):
            o_ref[0, :, hh * KV_LORA:(hh + 1) * KV_LORA] = out[hh * ds:(hh + 1) * ds, :]

    @pl.when(step == last_step)
    def _():
        for s0 in range(DEC_SLOTS):
            _dec_wait_slot(cbuf, kbuf, sems, s0)


def _attn_sample(qlat, qh, ckvn, krn, pool_ckv, pool_kr, page_table):
    b, ds, _ = qlat.shape
    nq = ds * MLA_H
    n_pages = page_table.shape[1]
    pg = DEC_PG
    pt = page_table.reshape(-1)
    pt = jnp.concatenate([pt, pt[:DEC_SLOTS * pg]])

    fixed = lambda w, r: pl.BlockSpec((1, r, w), lambda bi, j, pt_ref: (bi, 0, 0))
    grid_spec = pltpu.PrefetchScalarGridSpec(
        num_scalar_prefetch=1,
        grid=(b, n_pages // pg),
        in_specs=[fixed(MLA_H * KV_LORA, ds), fixed(MLA_H * HEAD_PAD, ds), fixed(KV_LORA, ds), fixed(HEAD_PAD, ds),
                  pl.BlockSpec(memory_space=pl.ANY), pl.BlockSpec(memory_space=pl.ANY)],
        out_specs=fixed(MLA_H * KV_LORA, ds),
        scratch_shapes=[pltpu.VMEM((nq, 1), F32), pltpu.VMEM((nq, 1), F32),
                        pltpu.VMEM((nq, KV_LORA), F32),
                        pltpu.VMEM((PAGE_SIZE, KV_LORA), BF16), pltpu.VMEM((PAGE_SIZE, HEAD_PAD), BF16),
                        pltpu.VMEM((DEC_SLOTS, pg * PAGE_SIZE, KV_LORA), F32),
                        pltpu.VMEM((DEC_SLOTS, MLA_ROPE, pg * PAGE_SIZE), F32),
                        pltpu.SemaphoreType.DMA((DEC_SLOTS, 2))],
    )
    return pl.pallas_call(
        _dec_kernel,
        grid_spec=grid_spec,
        out_shape=jax.ShapeDtypeStruct((b, ds, MLA_H * KV_LORA), F32),
        compiler_params=pltpu.CompilerParams(dimension_semantics=("arbitrary", "arbitrary"),
                                             vmem_limit_bytes=VMEM_LIMIT),
        name="attn_sample",
    )(pt, qlat, qh, ckvn, krn, pool_ckv, pool_kr)


def _sigmoid(x):
    return 0.5 * jnp.tanh(0.5 * x) + 0.5


def _silu(x):
    h = 0.5 * x
    return h * jnp.tanh(h) + h


def _back_kernel(absorb, x_ref, gate_ref, hml_ref, mo_ref, mz_ref, o_ref, az_ref, gn_ref, wuv_ref,
                 wout_ref, lng_ref, lnb_ref, y_ref):
    tb, ts, d = x_ref.shape
    tm = tb * ts
    hm = hml_ref[...].astype(F32) * _sigmoid(mo_ref[...].astype(F32))
    parts = []
    for hh in range(ML_H):
        t = hm[:, hh * ML_DV:(hh + 1) * ML_DV]
        mu = jnp.mean(t, axis=-1, keepdims=True)
        tc = t - mu
        var = jnp.mean(tc * tc, axis=-1, keepdims=True)
        parts.append(tc * lax.rsqrt(var + EPS))
    mz = mz_ref[...].astype(F32)
    y_ml = jnp.concatenate(parts, axis=1) * gn_ref[...] * _silu(mz)
    if absorb:
        olat = o_ref[...].astype(BF16)
        o_mla = jnp.concatenate(
            [_mm(olat[:, p * 2 * KV_LORA:(p + 1) * 2 * KV_LORA], wuv_ref[p]) for p in range(MLA_H // 2)],
            axis=1)
    else:
        o_mla = o_ref[...].astype(F32)
    az = az_ref[...].astype(F32)
    y_mla = o_mla * _silu(az)
    out = _mm(y_ml.astype(BF16), wout_ref[0:ML_W, :]) + _mm(y_mla.astype(BF16), wout_ref[ML_W:, :])
    z = ALPHA * x_ref[...] + gate_ref[...] * out.reshape(tb, ts, d)
    mu = jnp.mean(z, axis=-1, keepdims=True)
    zc = z - mu
    var = jnp.mean(zc * zc, axis=-1, keepdims=True)
    y_ref[...] = zc * lax.rsqrt(var + EPS) * lng_ref[...] + lnb_ref[...]


def _back(absorb, x, mod, hml, mo, mz, o, az, wts):
    b, s, d = x.shape
    n = b * s
    tm = BACK_TM
    if absorb:
        tb, ts = tm // s, s
        x_map = lambda i: (i, 0, 0)
        gate_map = lambda i: (i, 0, 2)
    else:
        tb, ts = 1, tm
        spb = s // tm
        x_map = lambda i: (i // spb, i % spb, 0)
        gate_map = lambda i: (i // spb, 0, 2)
    const2 = lambda i: (0, 0)
    tok = lambda w: pl.BlockSpec((tm, w), lambda i: (i, 0))
    ow = o.shape[1]
    return pl.pallas_call(
        functools.partial(_back_kernel, absorb),
        grid=(n // tm,),
        in_specs=[pl.BlockSpec((tb, ts, d), x_map),
                  pl.BlockSpec((tb, 1, d), gate_map),
                  tok(ML_W), tok(ML_W), tok(ML_W), tok(ow), tok(MLA_W),
                  pl.BlockSpec((1, ML_W), const2),
                  pl.BlockSpec((MLA_H // 2, 2 * KV_LORA, LANES), lambda i: (0, 0, 0)),
                  pl.BlockSpec((d, d), const2),
                  pl.BlockSpec((1, d), const2),
                  pl.BlockSpec((1, d), const2)],
        out_specs=pl.BlockSpec((tb, ts, d), x_map),
        out_shape=jax.ShapeDtypeStruct((b, s, d), F32),
        compiler_params=pltpu.CompilerParams(dimension_semantics=("arbitrary",),
                                             vmem_limit_bytes=VMEM_LIMIT),
        name="back_sample" if absorb else "back_prompt",
    )(x, mod, hml, mo, mz, o, az, wts["gn"], wts["wuv_pair"], wts["w_out"], wts["ln_g"], wts["ln_b"])


def _prep_weights(l, w_in, ml_b_i, ml_b_f, ml_gn, mla_q_norm, mla_kv_norm, mla_w_uq, mla_w_uk,
                  mla_w_uv, w_out, ln_g, ln_b):
    assert w_in.shape[2] == N_IN
    tail = HEAD_PAD - MLA_NOPE - MLA_ROPE
    w_in_p = _wprep(jnp.swapaxes(w_in[l], 0, 1))
    b_gate = jnp.concatenate([ml_b_i[l], ml_b_f[l], jnp.zeros((LANES - 2 * ML_H,), F32)]).reshape(1, LANES)
    uq = mla_w_uq[l].reshape(Q_LORA, MLA_H, MLA_NOPE + MLA_ROPE)
    nope, r1, r2 = uq[..., :MLA_NOPE], uq[..., MLA_NOPE:MLA_NOPE + ROPE_HALF], uq[..., MLA_NOPE + ROPE_HALF:]
    zq = lambda c: jnp.zeros((Q_LORA, MLA_H, c), F32)
    wqa = jnp.concatenate([nope, r1, r2, zq(tail)], axis=-1).reshape(Q_LORA, MLA_H * HEAD_PAD).astype(BF16)
    uk = mla_w_uk[l].reshape(KV_LORA, MLA_H, MLA_NOPE)
    wk_pad = jnp.concatenate([uk, jnp.zeros((KV_LORA, MLA_H, HEAD_PAD - MLA_NOPE), F32)],
                             axis=-1).reshape(KV_LORA, MLA_H * HEAD_PAD).astype(BF16)
    wukt = jnp.concatenate([jnp.transpose(uk, (1, 2, 0)),
                            jnp.zeros((MLA_H, HEAD_PAD - MLA_NOPE, KV_LORA), F32)], axis=1).astype(BF16)
    uv = mla_w_uv[l].reshape(KV_LORA, MLA_H // 2, 2, MLA_VD)
    zv = jnp.zeros((KV_LORA, MLA_H // 2, MLA_VD), F32)
    wuv_pair = jnp.concatenate(
        [jnp.concatenate([uv[:, :, 0], zv], axis=-1), jnp.concatenate([zv, uv[:, :, 1]], axis=-1)],
        axis=0)
    wuv_pair = jnp.transpose(wuv_pair, (1, 0, 2)).astype(BF16)
    return dict(w_in=w_in_p, b_gate=b_gate, q_norm=mla_q_norm[l].reshape(1, -1),
                kv_norm=mla_kv_norm[l].reshape(1, -1), wqa=wqa, wk_pad=wk_pad, wukt=wukt,
                w_uv_t=mla_w_uv[l].T.astype(BF16), wuv_pair=wuv_pair, gn=ml_gn[l].reshape(1, -1),
                w_out=w_out[l].astype(BF16), ln_g=ln_g[l].reshape(1, -1), ln_b=ln_b[l].reshape(1, -1))


def _rope_tables(pos):
    f32 = np.float32
    inv = ROPE_THETA ** (-np.arange(ROPE_HALF, dtype=np.float64) / ROPE_HALF)
    ang = pos.astype(np.float64)[:, None] * inv[None, :]
    n = pos.shape[0]
    tail = HEAD_PAD - MLA_NOPE - MLA_ROPE
    cos, sin = np.cos(ang).astype(f32), np.sin(ang).astype(f32)
    zeros = lambda c: np.zeros((n, c), f32)
    cm = np.concatenate([np.ones((n, MLA_NOPE), f32), cos, cos, zeros(tail)], 1)
    s_lo = np.concatenate([zeros(MLA_NOPE), -sin, zeros(ROPE_HALF), zeros(tail)], 1)
    s_hi = np.concatenate([zeros(MLA_NOPE), zeros(ROPE_HALF), sin, zeros(tail)], 1)
    return np.concatenate([cm, s_lo, s_hi], 1)


def kernel(x_prompt, x_sample, c_prompt, c_sample, cache_ckv, cache_krope, state_C, state_n, state_m,
           page_table, w_ada, b_ada, w_in, ml_b_i, ml_b_f, ml_gn, mla_q_norm, mla_kv_norm,
           mla_w_uq, mla_w_uk, mla_w_uv, w_out, ln_g, ln_b):
    bp, sp, d = x_prompt.shape
    bs, ss, _ = x_sample.shape
    past = page_table.shape[1] * PAGE_SIZE
    tab_p = jnp.asarray(_rope_tables(np.arange(sp)))
    tab_s = jnp.asarray(np.tile(_rope_tables(past + np.arange(ss)), (FRONT_TM // ss, 1)))
    pad = (-(bp + bs)) % 8
    c_all = jnp.concatenate([c_prompt, c_sample, jnp.zeros((pad, d), F32)], axis=0)

    xp, xs = x_prompt, x_sample
    outs = [[] for _ in range(10)]
    for l in range(DEPTH):
        wts = _prep_weights(l, w_in, ml_b_i, ml_b_f, ml_gn, mla_q_norm, mla_kv_norm, mla_w_uq, mla_w_uk,
                            mla_w_uv, w_out, ln_g, ln_b)
        mod = _ada(c_all, w_ada[l], b_ada[l])
        mod_p = mod[:bp].reshape(bp, 1, 3 * d)
        mod_s = mod[bp:bp + bs].reshape(bs, 1, 3 * d)

        mq, mk, mv, g, mo, mz, az, ckv, kr, qh, kh, vt = _front(False, xp, mod_p, tab_p, wts)
        seq3 = lambda t: t.reshape(bp, sp, -1)
        c0 = jnp.zeros((bp, ML_H, ML_DK, ML_DV), F32)
        n0 = jnp.zeros((bp, ML_H, 1, ML_DK), F32)
        m0 = jnp.full((bp, ML_H, 1, LANES), -jnp.inf, F32)
        hml, c_p, n_p, m_p = _mlstm(seq3(mq), seq3(mk), seq3(mv), seq3(g), c0, n0, m0, ML_CHUNK, bp)
        o_p = _attn_prompt(qh, kh, vt)
        yp = _back(False, xp, mod_p, hml.reshape(bp * sp, -1), mo, mz, o_p.reshape(bp * sp, -1), az, wts)
        outs[0].append(ckv.reshape(bp, sp, KV_LORA))
        outs[1].append(kr.reshape(bp, sp, MLA_ROPE))
        outs[2].append(c_p)
        outs[3].append(n_p[:, :, 0, :])
        outs[4].append(m_p[:, :, 0, 0])

        mq, mk, mv, g, mo, mz, az, ckv, kr, qlat, qh, krp = _front(True, xs, mod_s, tab_s, wts)
        seq3 = lambda t: t.reshape(bs, ss, -1)
        n0 = state_n[l].astype(F32).reshape(bs, ML_H, 1, ML_DK)
        m0 = jnp.broadcast_to(state_m[l].astype(F32)[:, :, None, None], (bs, ML_H, 1, LANES))
        hml, c_s, n_s, m_s = _mlstm(seq3(mq), seq3(mk), seq3(mv), seq3(g), state_C[l].astype(F32), n0, m0, ss, 8)
        o_s = _attn_sample(seq3(qlat), seq3(qh), seq3(ckv), seq3(krp), cache_ckv[l],
                           jnp.swapaxes(cache_krope[l], 1, 2), page_table)
        ys = _back(True, xs, mod_s, hml.reshape(bs * ss, -1), mo, mz,
                   o_s.reshape(bs * ss, MLA_H * KV_LORA), az, wts)
        outs[5].append(ckv.reshape(bs, ss, KV_LORA).astype(cache_ckv.dtype))
        outs[6].append(kr.reshape(bs, ss, MLA_ROPE).astype(cache_krope.dtype))
        outs[7].append(c_s.astype(state_C.dtype))
        outs[8].append(n_s[:, :, 0, :].astype(state_n.dtype))
        outs[9].append(m_s[:, :, 0, 0].astype(state_m.dtype))
        xp, xs = yp, ys
    return (xp, xs) + tuple(jnp.stack(o) for o in outs)
```

```python
import functools

import numpy as np
import jax
import jax.numpy as jnp
from jax import lax
from jax.experimental import pallas as pl
from jax.experimental.pallas import tpu as pltpu

F32 = jnp.float32
BF16 = jnp.bfloat16

D_MODEL = 1024
DEPTH = 1
PAGE_SIZE = 128
ML_W = 512
MLA_W = 512
ML_DK = 128
ML_DV = 128
ML_H = 4
MLA_VD = 64
MLA_H = 8
MLA_NOPE = 64
MLA_ROPE = 32
ROPE_HALF = MLA_ROPE // 2
MLA_SCALE = (MLA_NOPE + MLA_ROPE) ** -0.5
Q_SCALE = MLA_SCALE * float(np.log2(np.e))
Q_LORA = 384
KV_LORA = 256
ROPE_THETA = 10000.0
ML_CHUNK = 128
EPS = 1e-6
ALPHA = (2.0 * DEPTH) ** 0.25

LANES = 128
HEAD_PAD = 128
VMEM_LIMIT = 48 * 1024 * 1024

_SEG = {}
_off = 0
for _name, _w in (("mq", 512), ("mk", 512), ("mv", 512), ("gk", 256), ("mo", 512), ("mz", 512),
                  ("cq", Q_LORA), ("ckv", KV_LORA), ("az", 512)):
    _SEG[_name] = (_off, _off + _w)
    _off += _w
N_IN_PAD = _off

FRONT_TM = 512
BACK_TM = 512
ATT_TQ = 1024
ATT_TK = 2048
ATT_KC = 256
MASKED_MAX_FLOOR = -1e30
DEC_PG = 64
DEC_GROUPS = 8
DEC_SLOTS = 2


def _nt(a, b):
    return lax.dot_general(a, b, (((1,), (1,)), ((), ())), preferred_element_type=F32)


def _tn(a, b):
    return lax.dot_general(a, b, (((0,), (0,)), ((), ())), preferred_element_type=F32)


def _mm(a, b):
    return jnp.dot(a, b, preferred_element_type=F32)


def _mm_exact(a, b):
    return jnp.dot(a, b, preferred_element_type=F32, precision=lax.Precision.HIGHEST)


def _nt_exact(a, b):
    return lax.dot_general(a, b, (((1,), (1,)), ((), ())), preferred_element_type=F32,
                           precision=lax.Precision.HIGHEST)


def _ada_kernel(c_ref, w_ref, b_ref, o_ref):
    o_ref[...] = _mm(c_ref[...].astype(BF16), w_ref[...].astype(BF16)) + b_ref[...]


def _ada(c_all, w_ada, b_ada):
    m = c_all.shape[0]
    tn = 1024
    return pl.pallas_call(
        _ada_kernel,
        grid=(3 * D_MODEL // tn,),
        in_specs=[pl.BlockSpec((m, D_MODEL), lambda j: (0, 0)),
                  pl.BlockSpec((D_MODEL, tn), lambda j: (0, j)),
                  pl.BlockSpec((1, tn), lambda j: (0, j))],
        out_specs=pl.BlockSpec((m, tn), lambda j: (0, j)),
        out_shape=jax.ShapeDtypeStruct((m, 3 * D_MODEL), F32),
        name="ada",
    )(c_all, w_ada, b_ada.reshape(1, -1))


_IN_ROWS = {}
_off = 0
for _name, _w in (("mq", 512), ("mk", 512), ("mv", 512), ("gates", 2 * ML_H), ("mo", 512), ("mz", 512),
                  ("cq", Q_LORA), ("ckv", KV_LORA), ("kr", MLA_ROPE), ("az", 512)):
    _IN_ROWS[_name] = (_off, _off + _w)
    _off += _w
N_IN = _off


def _wprep_kernel(wt_ref, o_ref):
    dk = wt_ref.shape[1]
    rows = lambda name: wt_ref[_IN_ROWS[name][0]:_IN_ROWS[name][1], :]
    zeros = lambda r: jnp.zeros((r, dk), F32)
    tail = HEAD_PAD - MLA_NOPE - MLA_ROPE
    gk = jnp.concatenate([rows("gates"), zeros(LANES - 2 * ML_H), zeros(MLA_NOPE), rows("kr"), zeros(tail)], axis=0)
    for name in _SEG:
        lo, hi = _SEG[name]
        o_ref[:, lo:hi] = (gk if name == "gk" else rows(name)).T.astype(BF16)


def _wprep(w_in_t):
    n_in, d = w_in_t.shape
    dk = 256
    return pl.pallas_call(
        _wprep_kernel,
        grid=(d // dk,),
        in_specs=[pl.BlockSpec((n_in, dk), lambda i: (0, i))],
        out_specs=pl.BlockSpec((dk, N_IN_PAD), lambda i: (i, 0)),
        out_shape=jax.ShapeDtypeStruct((d, N_IN_PAD), BF16),
        compiler_params=pltpu.CompilerParams(dimension_semantics=("arbitrary",),
                                             vmem_limit_bytes=VMEM_LIMIT),
        name="wprep",
    )(w_in_t)


def _rms(x, g):
    return x * lax.rsqrt(jnp.mean(x * x, axis=-1, keepdims=True) + EPS) * g


def _log_sigmoid(x):
    return jnp.minimum(x, 0.0) - jnp.log1p(jnp.exp(-jnp.abs(x)))


def _rotary(a, tab_ref):
    cm = tab_ref[:, 0:LANES]
    s_lo = tab_ref[:, LANES:2 * LANES]
    s_hi = tab_ref[:, 2 * LANES:3 * LANES]
    return a * cm + pltpu.roll(a, LANES - ROPE_HALF, 1) * s_lo + pltpu.roll(a, ROPE_HALF, 1) * s_hi


def _front_kernel(absorb, x_ref, sh_ref, sc_ref, tab_ref, win_ref, bg_ref, qn_ref, kvn_ref,
                  wqa_ref, wk_ref, wv_ref,
                  mq_ref, mk_ref, mv_ref, g_ref, mo_ref, mz_ref, az_ref, ckv_ref, kr_ref,
                  o1_ref, o2_ref, o3_ref):
    tb, ts, d = x_ref.shape
    tm = tb * ts
    h = x_ref[...] * (1.0 + sc_ref[...]) + sh_ref[...]
    h = h.reshape(tm, d).astype(BF16)

    def seg(name):
        lo, hi = _SEG[name]
        return _mm(h, win_ref[:, lo:hi])

    mq_ref[...] = seg("mq").astype(BF16)
    mk_ref[...] = (seg("mk") * (ML_DK ** -0.5)).astype(BF16)
    mv_ref[...] = seg("mv").astype(BF16)
    gk = seg("gk")
    gz = gk[:, :LANES] + bg_ref[...]
    lane = lax.broadcasted_iota(jnp.int32, gz.shape, 1)
    g_ref[...] = jnp.where(lane < ML_H, gz, jnp.where(lane < 2 * ML_H, _log_sigmoid(gz), 0.0))
    mo_ref[...] = seg("mo").astype(BF16)
    mz_ref[...] = seg("mz").astype(BF16)
    az_ref[...] = seg("az").astype(BF16)

    ckvn = _rms(seg("ckv"), kvn_ref[...])
    ckv_ref[...] = ckvn
    ckvn_b = ckvn.astype(BF16)
    krp = _rotary(gk[:, LANES:], tab_ref)
    if absorb:
        kr_ref[...] = krp[:, MLA_NOPE:MLA_NOPE + MLA_ROPE]
    else:
        kr_ref[0] = krp.T[MLA_NOPE:MLA_NOPE + MLA_ROPE, :]

    cqn = _rms(seg("cq"), qn_ref[...]).astype(BF16)
    qa = _mm(cqn, wqa_ref[...])
    for hh in range(MLA_H):
        sl = slice(hh * HEAD_PAD, (hh + 1) * HEAD_PAD)
        qh = (_rotary(qa[:, sl], tab_ref) * Q_SCALE).astype(BF16)
        if absorb:
            o1_ref[:, hh * KV_LORA:(hh + 1) * KV_LORA] = _mm(qh, wk_ref[hh]).astype(BF16)
            o2_ref[:, sl] = qh
        else:
            o1_ref[0, hh] = qh
            o2_ref[0, hh] = (_mm(ckvn_b, wk_ref[:, sl]) + krp).astype(BF16)
    if absorb:
        o3_ref[...] = krp.astype(BF16)
    else:
        o3_ref[0] = _nt(wv_ref[...], ckvn_b).astype(BF16)


def _front(absorb, x, mod, tab, wts):
    b, s, d = x.shape
    n = b * s
    tm = FRONT_TM
    if absorb:
        tb, ts = tm // s, s
        x_map = lambda i: (i, 0, 0)
        mod_map = lambda k: (lambda i: (i, 0, k))
        tab_map = lambda i: (0, 0)
    else:
        tb, ts = 1, tm
        spb = s // tm
        x_map = lambda i: (i // spb, i % spb, 0)
        mod_map = lambda k: (lambda i: (i // spb, 0, k))
        tab_map = lambda i: (i % spb, 0)
    const2 = lambda i: (0, 0)
    tok = lambda w: pl.BlockSpec((tm, w), lambda i: (i, 0))
    in_specs = [
        pl.BlockSpec((tb, ts, d), x_map),
        pl.BlockSpec((tb, 1, d), mod_map(0)),
        pl.BlockSpec((tb, 1, d), mod_map(1)),
        pl.BlockSpec((tm, 3 * LANES), tab_map),
        pl.BlockSpec((d, N_IN_PAD), const2),
        pl.BlockSpec((1, LANES), const2),
        pl.BlockSpec((1, Q_LORA), const2),
        pl.BlockSpec((1, KV_LORA), const2),
        pl.BlockSpec((Q_LORA, MLA_H * HEAD_PAD), const2),
    ]
    out_specs = [tok(512), tok(512), tok(512), tok(LANES), tok(512), tok(512), tok(512),
                 tok(KV_LORA), tok(MLA_ROPE)]
    out_shape = [jax.ShapeDtypeStruct((n, w), dt)
                 for w, dt in ((512, BF16), (512, BF16), (512, BF16), (LANES, F32), (512, BF16), (512, BF16),
                               (512, BF16), (KV_LORA, F32), (MLA_ROPE, F32))]
    if not absorb:
        out_specs[8] = pl.BlockSpec((1, MLA_ROPE, tm), lambda i: (i // spb, 0, i % spb))
        out_shape[8] = jax.ShapeDtypeStruct((b, MLA_ROPE, s), F32)
    if absorb:
        in_specs += [pl.BlockSpec((MLA_H, HEAD_PAD, KV_LORA), lambda i: (0, 0, 0)),
                     pl.BlockSpec((MLA_W, KV_LORA), const2)]
        out_specs += [tok(MLA_H * KV_LORA), tok(MLA_H * HEAD_PAD), tok(HEAD_PAD)]
        out_shape += [jax.ShapeDtypeStruct((n, MLA_H * KV_LORA), BF16),
                      jax.ShapeDtypeStruct((n, MLA_H * HEAD_PAD), BF16),
                      jax.ShapeDtypeStruct((n, HEAD_PAD), BF16)]
        wk = wts["wukt"]
    else:
        head_map = lambda i: (i // spb, 0, i % spb, 0)
        in_specs += [pl.BlockSpec((KV_LORA, MLA_H * HEAD_PAD), const2),
                     pl.BlockSpec((MLA_W, KV_LORA), const2)]
        out_specs += [pl.BlockSpec((1, MLA_H, tm, HEAD_PAD), head_map),
                      pl.BlockSpec((1, MLA_H, tm, HEAD_PAD), head_map),
                      pl.BlockSpec((1, MLA_W, tm), lambda i: (i // spb, 0, i % spb))]
        out_shape += [jax.ShapeDtypeStruct((b, MLA_H, s, HEAD_PAD), BF16),
                      jax.ShapeDtypeStruct((b, MLA_H, s, HEAD_PAD), BF16),
                      jax.ShapeDtypeStruct((b, MLA_W, s), BF16)]
        wk = wts["wk_pad"]
    return pl.pallas_call(
        functools.partial(_front_kernel, absorb),
        grid=(n // tm,),
        in_specs=in_specs,
        out_specs=out_specs,
        out_shape=out_shape,
        compiler_params=pltpu.CompilerParams(dimension_semantics=("arbitrary",),
                                             vmem_limit_bytes=VMEM_LIMIT),
        name="front_sample" if absorb else "front_prompt",
    )(x, mod, mod, tab, wts["w_in"], wts["b_gate"], wts["q_norm"], wts["kv_norm"],
      wts["wqa"], wk, wts["w_uv_t"])


def _mlstm_kernel(q_ref, k_ref, v_ref, g_ref, c0_ref, n0_ref, m0_ref,
                  h_ref, c_ref, n_ref, m_ref):
    tb, L, _ = q_ref.shape
    ci = pl.program_id(1)

    @pl.when(ci == 0)
    def _():
        c_ref[...] = c0_ref[...]
        n_ref[...] = n0_ref[...]
        m_ref[...] = m0_ref[...]

    row = lax.broadcasted_iota(jnp.int32, (L, L), 0)
    col = lax.broadcasted_iota(jnp.int32, (L, L), 1)
    causal = col <= row
    tril = causal.astype(F32)
    sel = (lax.broadcasted_iota(jnp.int32, (8, LANES), 0)
           == lax.broadcasted_iota(jnp.int32, (8, LANES), 1)).astype(F32)

    chains = [(t, hh) for t in range(tb) for hh in range(ML_H)]
    sl = lambda hh: slice(hh * ML_DK, (hh + 1) * ML_DK)
    gates = {}
    for t in range(tb):
        g = g_ref[t]
        fcum = _mm_exact(tril, g)
        g_rows = _nt_exact(sel, g)
        f_rows = _nt_exact(sel, fcum)
        gates[t] = (g, fcum, g_rows, f_rows)

    qk, qc, c_prevs = {}, {}, {}
    for t, hh in chains:
        qb = q_ref[t, :, sl(hh)]
        c_prevs[t, hh] = c_ref[t, hh]
        qk[t, hh] = _nt(qb, k_ref[t, :, sl(hh)])
        qc[t, hh] = _mm(qb, c_prevs[t, hh].astype(BF16))

    stab = {}
    for t, hh in chains:
        g, fcum, g_rows, f_rows = gates[t]
        m_prev = m_ref[t, hh][:, :1]
        f_col = fcum[:, ML_H + hh:ML_H + hh + 1]
        f_row = f_rows[ML_H + hh:ML_H + hh + 1, :]
        ig_row = g_rows[hh:hh + 1, :]
        dmat = jnp.where(causal, f_col - f_row + ig_row, -jnp.inf)
        m_inter = jnp.broadcast_to(f_col, (L, LANES)) + m_prev
        m_t = jnp.maximum(m_inter, jnp.max(dmat, axis=-1, keepdims=True))
        stab[t, hh] = (jnp.exp(dmat - m_t[:, :L]), jnp.exp(m_inter - m_t), m_t)

    svs = {}
    ones_cols = jnp.ones((L, LANES), BF16)
    for t, hh in chains:
        sq = (qk[t, hh] * stab[t, hh][0]).astype(BF16)
        svs[t, hh] = _mm(sq, jnp.concatenate([v_ref[t, :, sl(hh)], ones_cols], axis=1))

    for t, hh in chains:
        _, a, m_t = stab[t, hh]
        n_rep = jnp.broadcast_to(n_ref[t, hh], (LANES, ML_DK)).astype(BF16)
        qn = _nt(q_ref[t, :, sl(hh)], n_rep)
        num = svs[t, hh][:, :ML_DV] + a * qc[t, hh]
        den = svs[t, hh][:, ML_DV:] + a * qn
        h_ref[t, :, sl(hh)] = (num / jnp.maximum(jnp.abs(den), jnp.exp(-m_t))).astype(h_ref.dtype)

    for t, hh in chains:
        w, a, m_t = stab[t, hh]
        wl_row = w[L - 1:L, :]
        al = a[L - 1:L, :]
        kb = k_ref[t, :, sl(hh)]
        kw_t = (kb.astype(F32).T * wl_row).astype(BF16)
        c_ref[t, hh] = al[:, :1] * c_prevs[t, hh] + _mm(kw_t, v_ref[t, :, sl(hh)])
        n_ref[t, hh] = al * n_ref[t, hh] + _mm(w[L - 8:L, :].astype(BF16), kb)[7:8, :]
        m_ref[t, hh] = m_t[L - 1:L, :]


def _mlstm(q, k, v, g, c0, n0, m0, chunk, tb):
    b, s, w = q.shape
    nc = s // chunk
    seq = lambda ww: pl.BlockSpec((tb, chunk, ww), lambda i, c: (i, c, 0))
    st_c = pl.BlockSpec((tb, ML_H, ML_DK, ML_DV), lambda i, c: (i, 0, 0, 0))
    st_v = pl.BlockSpec((tb, ML_H, 1, LANES), lambda i, c: (i, 0, 0, 0))
    return pl.pallas_call(
        _mlstm_kernel,
        grid=(b // tb, nc),
        in_specs=[seq(w), seq(w), seq(w), seq(LANES), st_c, st_v, st_v],
        out_specs=[seq(w), st_c, st_v, st_v],
        out_shape=[jax.ShapeDtypeStruct((b, s, w), BF16),
                   jax.ShapeDtypeStruct((b, ML_H, ML_DK, ML_DV), F32),
                   jax.ShapeDtypeStruct((b, ML_H, 1, LANES), F32),
                   jax.ShapeDtypeStruct((b, ML_H, 1, LANES), F32)],
        compiler_params=pltpu.CompilerParams(dimension_semantics=("arbitrary", "arbitrary"),
                                             vmem_limit_bytes=VMEM_LIMIT),
        name="mlstm",
    )(q, k, v, g, c0, n0, m0)


def _attn_kernel(qi_ref, ki_ref, last_ref, q_ref, k_ref, vt_ref, o_ref, m_sc, l_sc, acc_sc):
    step = pl.program_id(2)
    qi = qi_ref[step]
    ki = ki_ref[step]
    tq = q_ref.shape[2]
    tk = k_ref.shape[2]

    @pl.when(ki == 0)
    def _():
        m_sc[...] = jnp.full(m_sc.shape, -jnp.inf, F32)
        l_sc[...] = jnp.zeros(l_sc.shape, F32)
        acc_sc[...] = jnp.zeros(acc_sc.shape, F32)

    def tile(diag):
        chains = [(hh, c0) for hh in range(2) for c0 in range(0, min(tk, diag + tq), ATT_KC)]
        scores = {}
        for hh, c0 in chains:
            st = _nt(k_ref[0, hh, c0:c0 + ATT_KC, :], q_ref[0, hh])
            if c0 >= diag:
                keys = c0 - diag + lax.broadcasted_iota(jnp.int32, (ATT_KC, tq), 0)
                qrys = lax.broadcasted_iota(jnp.int32, (ATT_KC, tq), 1)
                st = jnp.where(keys <= qrys, st, -jnp.inf)
            scores[hh, c0] = st
        parts = ([], [])
        ones_rows = jnp.ones((16, ATT_KC), BF16)
        for hh, c0 in chains:
            st = scores[hh, c0]
            m_g = jnp.max(st, axis=0, keepdims=True)
            if c0 >= diag:
                m_g = jnp.maximum(m_g, MASKED_MAX_FLOOR)
            pt = jnp.exp2(st - m_g).astype(BF16)
            rows = slice(hh * MLA_VD, (hh + 1) * MLA_VD)
            v_ones = jnp.concatenate([vt_ref[0, rows, c0:c0 + ATT_KC], ones_rows], axis=0)
            o_g = _mm(v_ones, pt)
            parts[hh].append((m_g, o_g[MLA_VD:MLA_VD + 1, :], o_g[:MLA_VD, :]))
        for hh in range(2):
            rows = slice(hh * MLA_VD, (hh + 1) * MLA_VD)
            m_prev = m_sc[hh]
            m_new = functools.reduce(jnp.maximum, [m_g for m_g, _, _ in parts[hh]], m_prev)
            alpha = jnp.exp2(m_prev - m_new)
            l_new = alpha * l_sc[hh]
            acc = alpha * acc_sc[rows, :]
            for m_g, l_g, o_g in parts[hh]:
                w_g = jnp.exp2(m_g - m_new)
                l_new = l_new + w_g * l_g
                acc = acc + w_g * o_g
            m_sc[hh] = m_new
            l_sc[hh] = l_new
            acc_sc[rows, :] = acc

    first_query = qi * tq - ki * tk
    for diag in range(0, tk, tq):
        pl.when(first_query == diag)(functools.partial(tile, diag))
    pl.when(first_query >= tk)(functools.partial(tile, tk))

    @pl.when(last_ref[step] == 1)
    def _():
        out_t = jnp.concatenate([acc_sc[0:MLA_VD, :] / l_sc[0], acc_sc[MLA_VD:, :] / l_sc[1]], axis=0)
        o_ref[0] = out_t.T.astype(o_ref.dtype)


def _attn_prompt(qh, kh, vt):
    b, nh, s, _ = qh.shape
    tq, tk = ATT_TQ, ATT_TK
    assert tk % tq == 0 and tq % ATT_KC == 0 and s % tk == 0
    qi_l, ki_l, last_l = [], [], []
    for qi in range(s // tq):
        nk = ((qi + 1) * tq + tk - 1) // tk
        for ki in range(nk):
            qi_l.append(qi)
            ki_l.append(ki)
            last_l.append(int(ki == nk - 1))
    nsteps = len(qi_l)
    sched = [jnp.asarray(np.asarray(a, np.int32)) for a in (qi_l, ki_l, last_l)]
    grid_spec = pltpu.PrefetchScalarGridSpec(
        num_scalar_prefetch=3,
        grid=(b, nh // 2, nsteps),
        in_specs=[pl.BlockSpec((1, 2, tq, HEAD_PAD), lambda bi, hp, st, qi, ki, la: (bi, hp, qi[st], 0)),
                  pl.BlockSpec((1, 2, tk, HEAD_PAD), lambda bi, hp, st, qi, ki, la: (bi, hp, ki[st], 0)),
                  pl.BlockSpec((1, 2 * MLA_VD, tk), lambda bi, hp, st, qi, ki, la: (bi, hp, ki[st]))],
        out_specs=pl.BlockSpec((1, tq, LANES), lambda bi, hp, st, qi, ki, la: (bi, qi[st], hp)),
        scratch_shapes=[pltpu.VMEM((2, 1, tq), F32), pltpu.VMEM((2, 1, tq), F32),
                        pltpu.VMEM((2 * MLA_VD, tq), F32)],
    )
    return pl.pallas_call(
        _attn_kernel,
        grid_spec=grid_spec,
        out_shape=jax.ShapeDtypeStruct((b, s, nh * MLA_VD), BF16),
        compiler_params=pltpu.CompilerParams(
            dimension_semantics=("arbitrary", "arbitrary", "arbitrary"),
            vmem_limit_bytes=VMEM_LIMIT),
        name="attn_prompt",
    )(*sched, qh, kh, vt)


def _dec_page_copies(pt_ref, pool_ckv, pool_kr, cbuf, kbuf, sems, step, slot):
    copies = []
    for i in range(DEC_PG):
        page = pt_ref[step * DEC_PG + i]
        keys = pl.ds(i * PAGE_SIZE, PAGE_SIZE)
        copies.append(pltpu.make_async_copy(pool_ckv.at[page], cbuf.at[slot, keys, :], sems.at[slot, 0]))
        copies.append(pltpu.make_async_copy(pool_kr.at[page], kbuf.at[slot, :, keys], sems.at[slot, 1]))
    return copies


def _dec_wait_slot(cbuf, kbuf, sems, slot):
    pltpu.make_async_copy(cbuf.at[slot], cbuf.at[slot], sems.at[slot, 0]).wait()
    pltpu.make_async_copy(kbuf.at[slot], kbuf.at[slot], sems.at[slot, 1]).wait()


def _dec_kernel(pt_ref, qlat_ref, qh_ref, ckvn_ref, krn_ref, pool_ckv, pool_kr,
                o_ref, m_sc, l_sc, acc_sc, padc_sc, padk_sc, cbuf, kbuf, sems):
    j = pl.program_id(1)
    nj = pl.num_programs(1)
    step = pl.program_id(0) * nj + j
    last_step = pl.num_programs(0) * nj - 1
    slot = lax.rem(step, DEC_SLOTS)
    qlat = jnp.concatenate([qlat_ref[0, :, hh * KV_LORA:(hh + 1) * KV_LORA] for hh in range(MLA_H)], axis=0)
    qh = jnp.concatenate([qh_ref[0, :, hh * HEAD_PAD:(hh + 1) * HEAD_PAD] for hh in range(MLA_H)], axis=0)
    ds = ckvn_ref.shape[1]
    copies = functools.partial(_dec_page_copies, pt_ref, pool_ckv, pool_kr, cbuf, kbuf, sems)

    @pl.when(step == 0)
    def _():
        for s0 in range(DEC_SLOTS):
            for c in copies(s0, s0):
                c.start()

    _dec_wait_slot(cbuf, kbuf, sems, slot)
    next_copies = copies(step + DEC_SLOTS, slot)

    @pl.when(j == 0)
    def _():
        padc_sc[...] = jnp.zeros(padc_sc.shape, BF16)
        padk_sc[...] = jnp.zeros(padk_sc.shape, BF16)
        padc_sc[0:ds, :] = ckvn_ref[0].astype(BF16)
        padk_sc[0:ds, :] = krn_ref[0]
        cn = padc_sc[...]
        s = _nt(qlat, cn) + _nt(qh, padk_sc[...])
        tok = lax.broadcasted_iota(jnp.int32, (MLA_H, ds, s.shape[1]), 1).reshape(s.shape)
        key = lax.broadcasted_iota(jnp.int32, s.shape, 1)
        s = jnp.where(key <= tok, s, -jnp.inf)
        m = jnp.max(s, axis=-1, keepdims=True)
        p = jnp.exp2(s - m)
        m_sc[...] = m
        l_sc[...] = jnp.sum(p, axis=-1, keepdims=True)
        acc_sc[...] = _mm(p.astype(BF16), cn)

    qr = qh[:, MLA_NOPE:MLA_NOPE + MLA_ROPE]
    pages = cbuf[slot].astype(BF16)
    krt = kbuf[slot].astype(BF16)
    gk = pages.shape[0] // DEC_GROUPS
    groups = [pages[g * gk:(g + 1) * gk] for g in range(DEC_GROUPS)]
    scores, probs, parts = {}, {}, []

    def score(g):
        scores[g] = _nt(qlat, groups[g]) + _mm(qr, krt[:, g * gk:(g + 1) * gk])

    def soft(g):
        m_g = jnp.max(scores[g], axis=-1, keepdims=True)
        p = jnp.exp2(scores[g] - m_g)
        probs[g] = (m_g, jnp.sum(p, axis=-1, keepdims=True), p.astype(BF16))

    def value(g):
        m_g, l_g, pb = probs[g]
        parts.append((m_g, l_g, _mm(pb, groups[g])))

    per_group = len(next_copies) // DEC_GROUPS
    for g in range(DEC_GROUPS):
        score(g)
    soft(0)
    for g in range(DEC_GROUPS):
        if g + 1 < DEC_GROUPS:
            soft(g + 1)
        value(g)
        for c in next_copies[g * per_group:(g + 1) * per_group]:
            c.start()
    m_prev = m_sc[...]
    m_new = functools.reduce(jnp.maximum, [m_g for m_g, _, _ in parts], m_prev)
    alpha = jnp.exp2(m_prev - m_new)
    l_new = alpha * l_sc[...]
    acc = alpha * acc_sc[...]
    for m_g, l_g, o_g in parts:
        w_g = jnp.exp2(m_g - m_new)
        l_new = l_new + w_g * l_g
        acc = acc + w_g * o_g
    m_sc[...] = m_new
    l_sc[...] = l_new
    acc_sc[...] = acc

    @pl.when(j == nj - 1)
    def _():
        out = acc_sc[...] / l_sc[...]
        for hh in range(MLA_H):
            o_ref[0, :, hh * KV_LORA:(hh + 1) * KV_LORA] = out[hh * ds:(hh + 1) * ds, :]

    @pl.when(step == last_step)
    def _():
        for s0 in range(DEC_SLOTS):
            _dec_wait_slot(cbuf, kbuf, sems, s0)


def _attn_sample(qlat, qh, ckvn, krn, pool_ckv, pool_kr, page_table):
    b, ds, _ = qlat.shape
    nq = ds * MLA_H
    n_pages = page_table.shape[1]
    pg = DEC_PG
    pt = page_table.reshape(-1)
    pt = jnp.concatenate([pt, pt[:DEC_SLOTS * pg]])

    fixed = lambda w, r: pl.BlockSpec((1, r, w), lambda bi, j, pt_ref: (bi, 0, 0))
    grid_spec = pltpu.PrefetchScalarGridSpec(
        num_scalar_prefetch=1,
        grid=(b, n_pages // pg),
        in_specs=[fixed(MLA_H * KV_LORA, ds), fixed(MLA_H * HEAD_PAD, ds), fixed(KV_LORA, ds), fixed(HEAD_PAD, ds),
                  pl.BlockSpec(memory_space=pl.ANY), pl.BlockSpec(memory_space=pl.ANY)],
        out_specs=fixed(MLA_H * KV_LORA, ds),
        scratch_shapes=[pltpu.VMEM((nq, 1), F32), pltpu.VMEM((nq, 1), F32),
                        pltpu.VMEM((nq, KV_LORA), F32),
                        pltpu.VMEM((PAGE_SIZE, KV_LORA), BF16), pltpu.VMEM((PAGE_SIZE, HEAD_PAD), BF16),
                        pltpu.VMEM((DEC_SLOTS, pg * PAGE_SIZE, KV_LORA), F32),
                        pltpu.VMEM((DEC_SLOTS, MLA_ROPE, pg * PAGE_SIZE), F32),
                        pltpu.SemaphoreType.DMA((DEC_SLOTS, 2))],
    )
    return pl.pallas_call(
        _dec_kernel,
        grid_spec=grid_spec,
        out_shape=jax.ShapeDtypeStruct((b, ds, MLA_H * KV_LORA), F32),
        compiler_params=pltpu.CompilerParams(dimension_semantics=("arbitrary", "arbitrary"),
                                             vmem_limit_bytes=VMEM_LIMIT),
        name="attn_sample",
    )(pt, qlat, qh, ckvn, krn, pool_ckv, pool_kr)


def _sigmoid(x):
    return 0.5 * jnp.tanh(0.5 * x) + 0.5


def _silu(x):
    h = 0.5 * x
    return h * jnp.tanh(h) + h


def _back_kernel(absorb, x_ref, gate_ref, hml_ref, mo_ref, mz_ref, o_ref, az_ref, gn_ref, wuv_ref,
                 wout_ref, lng_ref, lnb_ref, y_ref):
    tb, ts, d = x_ref.shape
    tm = tb * ts
    hm = hml_ref[...].astype(F32) * _sigmoid(mo_ref[...].astype(F32))
    parts = []
    for hh in range(ML_H):
        t = hm[:, hh * ML_DV:(hh + 1) * ML_DV]
        mu = jnp.mean(t, axis=-1, keepdims=True)
        tc = t - mu
        var = jnp.mean(tc * tc, axis=-1, keepdims=True)
        parts.append(tc * lax.rsqrt(var + EPS))
    mz = mz_ref[...].astype(F32)
    y_ml = jnp.concatenate(parts, axis=1) * gn_ref[...] * _silu(mz)
    if absorb:
        olat = o_ref[...].astype(BF16)
        o_mla = jnp.concatenate(
            [_mm(olat[:, p * 2 * KV_LORA:(p + 1) * 2 * KV_LORA], wuv_ref[p]) for p in range(MLA_H // 2)],
            axis=1)
    else:
        o_mla = o_ref[...].astype(F32)
    az = az_ref[...].astype(F32)
    y_mla = o_mla * _silu(az)
    out = _mm(y_ml.astype(BF16), wout_ref[0:ML_W, :]) + _mm(y_mla.astype(BF16), wout_ref[ML_W:, :])
    z = ALPHA * x_ref[...] + gate_ref[...] * out.reshape(tb, ts, d)
    mu = jnp.mean(z, axis=-1, keepdims=True)
    zc = z - mu
    var = jnp.mean(zc * zc, axis=-1, keepdims=True)
    y_ref[...] = zc * lax.rsqrt(var + EPS) * lng_ref[...] + lnb_ref[...]


def _back(absorb, x, mod, hml, mo, mz, o, az, wts):
    b, s, d = x.shape
    n = b * s
    tm = BACK_TM
    if absorb:
        tb, ts = tm // s, s
        x_map = lambda i: (i, 0, 0)
        gate_map = lambda i: (i, 0, 2)
    else:
        tb, ts = 1, tm
        spb = s // tm
        x_map = lambda i: (i // spb, i % spb, 0)
        gate_map = lambda i: (i // spb, 0, 2)
    const2 = lambda i: (0, 0)
    tok = lambda w: pl.BlockSpec((tm, w), lambda i: (i, 0))
    ow = o.shape[1]
    return pl.pallas_call(
        functools.partial(_back_kernel, absorb),
        grid=(n // tm,),
        in_specs=[pl.BlockSpec((tb, ts, d), x_map),
                  pl.BlockSpec((tb, 1, d), gate_map),
                  tok(ML_W), tok(ML_W), tok(ML_W), tok(ow), tok(MLA_W),
                  pl.BlockSpec((1, ML_W), const2),
                  pl.BlockSpec((MLA_H // 2, 2 * KV_LORA, LANES), lambda i: (0, 0, 0)),
                  pl.BlockSpec((d, d), const2),
                  pl.BlockSpec((1, d), const2),
                  pl.BlockSpec((1, d), const2)],
        out_specs=pl.BlockSpec((tb, ts, d), x_map),
        out_shape=jax.ShapeDtypeStruct((b, s, d), F32),
        compiler_params=pltpu.CompilerParams(dimension_semantics=("arbitrary",),
                                             vmem_limit_bytes=VMEM_LIMIT),
        name="back_sample" if absorb else "back_prompt",
    )(x, mod, hml, mo, mz, o, az, wts["gn"], wts["wuv_pair"], wts["w_out"], wts["ln_g"], wts["ln_b"])


def _prep_weights(l, w_in, ml_b_i, ml_b_f, ml_gn, mla_q_norm, mla_kv_norm, mla_w_uq, mla_w_uk,
                  mla_w_uv, w_out, ln_g, ln_b):
    assert w_in.shape[2] == N_IN
    tail = HEAD_PAD - MLA_NOPE - MLA_ROPE
    w_in_p = _wprep(jnp.swapaxes(w_in[l], 0, 1))
    b_gate = jnp.concatenate([ml_b_i[l], ml_b_f[l], jnp.zeros((LANES - 2 * ML_H,), F32)]).reshape(1, LANES)
    uq = mla_w_uq[l].reshape(Q_LORA, MLA_H, MLA_NOPE + MLA_ROPE)
    nope, r1, r2 = uq[..., :MLA_NOPE], uq[..., MLA_NOPE:MLA_NOPE + ROPE_HALF], uq[..., MLA_NOPE + ROPE_HALF:]
    zq = lambda c: jnp.zeros((Q_LORA, MLA_H, c), F32)
    wqa = jnp.concatenate([nope, r1, r2, zq(tail)], axis=-1).reshape(Q_LORA, MLA_H * HEAD_PAD).astype(BF16)
    uk = mla_w_uk[l].reshape(KV_LORA, MLA_H, MLA_NOPE)
    wk_pad = jnp.concatenate([uk, jnp.zeros((KV_LORA, MLA_H, HEAD_PAD - MLA_NOPE), F32)],
                             axis=-1).reshape(KV_LORA, MLA_H * HEAD_PAD).astype(BF16)
    wukt = jnp.concatenate([jnp.transpose(uk, (1, 2, 0)),
                            jnp.zeros((MLA_H, HEAD_PAD - MLA_NOPE, KV_LORA), F32)], axis=1).astype(BF16)
    uv = mla_w_uv[l].reshape(KV_LORA, MLA_H // 2, 2, MLA_VD)
    zv = jnp.zeros((KV_LORA, MLA_H // 2, MLA_VD), F32)
    wuv_pair = jnp.concatenate(
        [jnp.concatenate([uv[:, :, 0], zv], axis=-1), jnp.concatenate([zv, uv[:, :, 1]], axis=-1)],
        axis=0)
    wuv_pair = jnp.transpose(wuv_pair, (1, 0, 2)).astype(BF16)
    return dict(w_in=w_in_p, b_gate=b_gate, q_norm=mla_q_norm[l].reshape(1, -1),
                kv_norm=mla_kv_norm[l].reshape(1, -1), wqa=wqa, wk_pad=wk_pad, wukt=wukt,
                w_uv_t=mla_w_uv[l].T.astype(BF16), wuv_pair=wuv_pair, gn=ml_gn[l].reshape(1, -1),
                w_out=w_out[l].astype(BF16), ln_g=ln_g[l].reshape(1, -1), ln_b=ln_b[l].reshape(1, -1))


def _rope_tables(pos):
    f32 = np.float32
    inv = ROPE_THETA ** (-np.arange(ROPE_HALF, dtype=np.float64) / ROPE_HALF)
    ang = pos.astype(np.float64)[:, None] * inv[None, :]
    n = pos.shape[0]
    tail = HEAD_PAD - MLA_NOPE - MLA_ROPE
    cos, sin = np.cos(ang).astype(f32), np.sin(ang).astype(f32)
    zeros = lambda c: np.zeros((n, c), f32)
    cm = np.concatenate([np.ones((n, MLA_NOPE), f32), cos, cos, zeros(tail)], 1)
    s_lo = np.concatenate([zeros(MLA_NOPE), -sin, zeros(ROPE_HALF), zeros(tail)], 1)
    s_hi = np.concatenate([zeros(MLA_NOPE), zeros(ROPE_HALF), sin, zeros(tail)], 1)
    return np.concatenate([cm, s_lo, s_hi], 1)


def kernel(x_prompt, x_sample, c_prompt, c_sample, cache_ckv, cache_krope, state_C, state_n, state_m,
           page_table, w_ada, b_ada, w_in, ml_b_i, ml_b_f, ml_gn, mla_q_norm, mla_kv_norm,
           mla_w_uq, mla_w_uk, mla_w_uv, w_out, ln_g, ln_b):
    bp, sp, d = x_prompt.shape
    bs, ss, _ = x_sample.shape
    past = page_table.shape[1] * PAGE_SIZE
    tab_p = jnp.asarray(_rope_tables(np.arange(sp)))
    tab_s = jnp.asarray(np.tile(_rope_tables(past + np.arange(ss)), (FRONT_TM // ss, 1)))
    pad = (-(bp + bs)) % 8
    c_all = jnp.concatenate([c_prompt, c_sample, jnp.zeros((pad, d), F32)], axis=0)

    xp, xs = x_prompt, x_sample
    outs = [[] for _ in range(10)]
    for l in range(DEPTH):
        wts = _prep_weights(l, w_in, ml_b_i, ml_b_f, ml_gn, mla_q_norm, mla_kv_norm, mla_w_uq, mla_w_uk,
                            mla_w_uv, w_out, ln_g, ln_b)
        mod = _ada(c_all, w_ada[l], b_ada[l])
        mod_p = mod[:bp].reshape(bp, 1, 3 * d)
        mod_s = mod[bp:bp + bs].reshape(bs, 1, 3 * d)

        mq, mk, mv, g, mo, mz, az, ckv, kr, qh, kh, vt = _front(False, xp, mod_p, tab_p, wts)
        seq3 = lambda t: t.reshape(bp, sp, -1)
        c0 = jnp.zeros((bp, ML_H, ML_DK, ML_DV), F32)
        n0 = jnp.zeros((bp, ML_H, 1, ML_DK), F32)
        m0 = jnp.full((bp, ML_H, 1, LANES), -jnp.inf, F32)
        hml, c_p, n_p, m_p = _mlstm(seq3(mq), seq3(mk), seq3(mv), seq3(g), c0, n0, m0, ML_CHUNK, bp)
        o_p = _attn_prompt(qh, kh, vt)
        yp = _back(False, xp, mod_p, hml.reshape(bp * sp, -1), mo, mz, o_p.reshape(bp * sp, -1), az, wts)
        outs[0].append(ckv.reshape(bp, sp, KV_LORA))
        outs[1].append(jnp.swapaxes(kr, 1, 2))
        outs[2].append(c_p)
        outs[3].append(n_p[:, :, 0, :])
        outs[4].append(m_p[:, :, 0, 0])

        mq, mk, mv, g, mo, mz, az, ckv, kr, qlat, qh, krp = _front(True, xs, mod_s, tab_s, wts)
        seq3 = lambda t: t.reshape(bs, ss, -1)
        n0 = state_n[l].astype(F32).reshape(bs, ML_H, 1, ML_DK)
        m0 = jnp.broadcast_to(state_m[l].astype(F32)[:, :, None, None], (bs, ML_H, 1, LANES))
        hml, c_s, n_s, m_s = _mlstm(seq3(mq), seq3(mk), seq3(mv), seq3(g), state_C[l].astype(F32), n0, m0, ss, 8)
        o_s = _attn_sample(seq3(qlat), seq3(qh), seq3(ckv), seq3(krp), cache_ckv[l],
                           jnp.swapaxes(cache_krope[l], 1, 2), page_table)
        ys = _back(True, xs, mod_s, hml.reshape(bs * ss, -1), mo, mz,
                   o_s.reshape(bs * ss, MLA_H * KV_LORA), az, wts)
        outs[5].append(ckv.reshape(bs, ss, KV_LORA).astype(cache_ckv.dtype))
        outs[6].append(kr.reshape(bs, ss, MLA_ROPE).astype(cache_krope.dtype))
        outs[7].append(c_s.astype(state_C.dtype))
        outs[8].append(n_s[:, :, 0, :].astype(state_n.dtype))
        outs[9].append(m_s[:, :, 0, 0].astype(state_m.dtype))
        xp, xs = yp, ys
    return (xp, xs) + tuple(jnp.stack(o) for o in outs)
```

```python
import functools

import numpy as np
import jax
import jax.numpy as jnp
from jax import lax
from jax.experimental import pallas as pl
from jax.experimental.pallas import tpu as pltpu

F32 = jnp.float32
BF16 = jnp.bfloat16

D_MODEL = 1024
DEPTH = 1
PAGE_SIZE = 128
ML_W = 512
MLA_W = 512
ML_DK = 128
ML_DV = 128
ML_H = 4
MLA_VD = 64
MLA_H = 8
MLA_NOPE = 64
MLA_ROPE = 32
ROPE_HALF = MLA_ROPE // 2
MLA_SCALE = (MLA_NOPE + MLA_ROPE) ** -0.5
Q_SCALE = MLA_SCALE * float(np.log2(np.e))
Q_LORA = 384
KV_LORA = 256
ROPE_THETA = 10000.0
ML_CHUNK = 128
EPS = 1e-6
ALPHA = (2.0 * DEPTH) ** 0.25

LANES = 128
HEAD_PAD = 128
VMEM_LIMIT = 48 * 1024 * 1024

_SEG = {}
_off = 0
for _name, _w in (("mq", 512), ("mk", 512), ("mv", 512), ("gk", 256), ("mo", 512), ("mz", 512),
                  ("cq", Q_LORA), ("ckv", KV_LORA), ("az", 512)):
    _SEG[_name] = (_off, _off + _w)
    _off += _w
N_IN_PAD = _off

FRONT_TM = 512
BACK_TM = 512
ATT_TQ = 1024
ATT_TK = 2048
ATT_KC = 256
MASKED_MAX_FLOOR = -1e30
DEC_PG = 64
DEC_GROUPS = 8
DEC_SLOTS = 2


def _nt(a, b):
    return lax.dot_general(a, b, (((1,), (1,)), ((), ())), preferred_element_type=F32)


def _tn(a, b):
    return lax.dot_general(a, b, (((0,), (0,)), ((), ())), preferred_element_type=F32)


def _mm(a, b):
    return jnp.dot(a, b, preferred_element_type=F32)


def _mm_exact(a, b):
    return jnp.dot(a, b, preferred_element_type=F32, precision=lax.Precision.HIGHEST)


def _nt_exact(a, b):
    return lax.dot_general(a, b, (((1,), (1,)), ((), ())), preferred_element_type=F32,
                           precision=lax.Precision.HIGHEST)


def _ada_kernel(c_ref, w_ref, b_ref, o_ref):
    o_ref[...] = _mm(c_ref[...].astype(BF16), w_ref[...].astype(BF16)) + b_ref[...]


def _ada(c_all, w_ada, b_ada):
    m = c_all.shape[0]
    tn = 1024
    return pl.pallas_call(
        _ada_kernel,
        grid=(3 * D_MODEL // tn,),
        in_specs=[pl.BlockSpec((m, D_MODEL), lambda j: (0, 0)),
                  pl.BlockSpec((D_MODEL, tn), lambda j: (0, j)),
                  pl.BlockSpec((1, tn), lambda j: (0, j))],
        out_specs=pl.BlockSpec((m, tn), lambda j: (0, j)),
        out_shape=jax.ShapeDtypeStruct((m, 3 * D_MODEL), F32),
        name="ada",
    )(c_all, w_ada, b_ada.reshape(1, -1))


_IN_ROWS = {}
_off = 0
for _name, _w in (("mq", 512), ("mk", 512), ("mv", 512), ("gates", 2 * ML_H), ("mo", 512), ("mz", 512),
                  ("cq", Q_LORA), ("ckv", KV_LORA), ("kr", MLA_ROPE), ("az", 512)):
    _IN_ROWS[_name] = (_off, _off + _w)
    _off += _w
N_IN = _off


def _wprep_kernel(wt_ref, o_ref):
    dk = wt_ref.shape[1]
    rows = lambda name: wt_ref[_IN_ROWS[name][0]:_IN_ROWS[name][1], :]
    zeros = lambda r: jnp.zeros((r, dk), F32)
    tail = HEAD_PAD - MLA_NOPE - MLA_ROPE
    gk = jnp.concatenate([rows("gates"), zeros(LANES - 2 * ML_H), zeros(MLA_NOPE), rows("kr"), zeros(tail)], axis=0)
    for name in _SEG:
        lo, hi = _SEG[name]
        o_ref[:, lo:hi] = (gk if name == "gk" else rows(name)).T.astype(BF16)


def _wprep(w_in_t):
    n_in, d = w_in_t.shape
    dk = 256
    return pl.pallas_call(
        _wprep_kernel,
        grid=(d // dk,),
        in_specs=[pl.BlockSpec((n_in, dk), lambda i: (0, i))],
        out_specs=pl.BlockSpec((dk, N_IN_PAD), lambda i: (i, 0)),
        out_shape=jax.ShapeDtypeStruct((d, N_IN_PAD), BF16),
        compiler_params=pltpu.CompilerParams(dimension_semantics=("arbitrary",),
                                             vmem_limit_bytes=VMEM_LIMIT),
        name="wprep",
    )(w_in_t)


def _rms(x, g):
    return x * lax.rsqrt(jnp.mean(x * x, axis=-1, keepdims=True) + EPS) * g


def _log_sigmoid(x):
    return jnp.minimum(x, 0.0) - jnp.log1p(jnp.exp(-jnp.abs(x)))


def _rotary(a, tab_ref):
    cm = tab_ref[:, 0:LANES]
    s_lo = tab_ref[:, LANES:2 * LANES]
    s_hi = tab_ref[:, 2 * LANES:3 * LANES]
    return a * cm + pltpu.roll(a, LANES - ROPE_HALF, 1) * s_lo + pltpu.roll(a, ROPE_HALF, 1) * s_hi


def _front_kernel(absorb, x_ref, sh_ref, sc_ref, tab_ref, win_ref, bg_ref, qn_ref, kvn_ref,
                  wqa_ref, wk_ref, wv_ref,
                  mq_ref, mk_ref, mv_ref, g_ref, mo_ref, mz_ref, az_ref, ckv_ref, kr_ref,
                  o1_ref, o2_ref, o3_ref):
    tb, ts, d = x_ref.shape
    tm = tb * ts
    h = x_ref[...] * (1.0 + sc_ref[...]) + sh_ref[...]
    h = h.reshape(tm, d).astype(BF16)

    def seg(name):
        lo, hi = _SEG[name]
        return _mm(h, win_ref[:, lo:hi])

    mq_ref[...] = seg("mq").astype(BF16)
    mk_ref[...] = (seg("mk") * (ML_DK ** -0.5)).astype(BF16)
    mv_ref[...] = seg("mv").astype(BF16)
    gk = seg("gk")
    gz = gk[:, :LANES] + bg_ref[...]
    lane = lax.broadcasted_iota(jnp.int32, gz.shape, 1)
    g_ref[...] = jnp.where(lane < ML_H, gz, jnp.where(lane < 2 * ML_H, _log_sigmoid(gz), 0.0))
    mo_ref[...] = seg("mo").astype(BF16)
    mz_ref[...] = seg("mz").astype(BF16)
    az_ref[...] = seg("az").astype(BF16)

    ckvn = _rms(seg("ckv"), kvn_ref[...])
    ckv_ref[...] = ckvn
    ckvn_b = ckvn.astype(BF16)
    krp = _rotary(gk[:, LANES:], tab_ref)
    if absorb:
        kr_ref[...] = krp[:, MLA_NOPE:MLA_NOPE + MLA_ROPE]
    else:
        kr_ref[0] = krp.T[MLA_NOPE:MLA_NOPE + MLA_ROPE, :]

    cqn = _rms(seg("cq"), qn_ref[...]).astype(BF16)
    qa = _mm(cqn, wqa_ref[...])
    for hh in range(MLA_H):
        sl = slice(hh * HEAD_PAD, (hh + 1) * HEAD_PAD)
        qh = (_rotary(qa[:, sl], tab_ref) * Q_SCALE).astype(BF16)
        if absorb:
            o1_ref[:, hh * KV_LORA:(hh + 1) * KV_LORA] = _mm(qh, wk_ref[hh]).astype(BF16)
            o2_ref[:, sl] = qh
        else:
            o1_ref[0, hh] = qh
            o2_ref[0, hh] = (_mm(ckvn_b, wk_ref[:, sl]) + krp).astype(BF16)
    if absorb:
        o3_ref[...] = krp.astype(BF16)
    else:
        o3_ref[0] = _nt(wv_ref[...], ckvn_b).astype(BF16)


def _front(absorb, x, mod, tab, wts):
    b, s, d = x.shape
    n = b * s
    tm = FRONT_TM
    if absorb:
        tb, ts = tm // s, s
        x_map = lambda i: (i, 0, 0)
        mod_map = lambda k: (lambda i: (i, 0, k))
        tab_map = lambda i: (0, 0)
    else:
        tb, ts = 1, tm
        spb = s // tm
        x_map = lambda i: (i // spb, i % spb, 0)
        mod_map = lambda k: (lambda i: (i // spb, 0, k))
        tab_map = lambda i: (i % spb, 0)
    const2 = lambda i: (0, 0)
    tok = lambda w: pl.BlockSpec((tm, w), lambda i: (i, 0))
    in_specs = [
        pl.BlockSpec((tb, ts, d), x_map),
        pl.BlockSpec((tb, 1, d), mod_map(0)),
        pl.BlockSpec((tb, 1, d), mod_map(1)),
        pl.BlockSpec((tm, 3 * LANES), tab_map),
        pl.BlockSpec((d, N_IN_PAD), const2),
        pl.BlockSpec((1, LANES), const2),
        pl.BlockSpec((1, Q_LORA), const2),
        pl.BlockSpec((1, KV_LORA), const2),
        pl.BlockSpec((Q_LORA, MLA_H * HEAD_PAD), const2),
    ]
    out_specs = [tok(512), tok(512), tok(512), tok(LANES), tok(512), tok(512), tok(512),
                 tok(KV_LORA), tok(MLA_ROPE)]
    out_shape = [jax.ShapeDtypeStruct((n, w), dt)
                 for w, dt in ((512, BF16), (512, BF16), (512, BF16), (LANES, F32), (512, BF16), (512, BF16),
                               (512, BF16), (KV_LORA, F32), (MLA_ROPE, F32))]
    if not absorb:
        out_specs[8] = pl.BlockSpec((1, MLA_ROPE, tm), lambda i: (i // spb, 0, i % spb))
        out_shape[8] = jax.ShapeDtypeStruct((b, MLA_ROPE, s), F32)
    if absorb:
        in_specs += [pl.BlockSpec((MLA_H, HEAD_PAD, KV_LORA), lambda i: (0, 0, 0)),
                     pl.BlockSpec((MLA_W, KV_LORA), const2)]
        out_specs += [tok(MLA_H * KV_LORA), tok(MLA_H * HEAD_PAD), tok(HEAD_PAD)]
        out_shape += [jax.ShapeDtypeStruct((n, MLA_H * KV_LORA), BF16),
                      jax.ShapeDtypeStruct((n, MLA_H * HEAD_PAD), BF16),
                      jax.ShapeDtypeStruct((n, HEAD_PAD), BF16)]
        wk = wts["wukt"]
    else:
        head_map = lambda i: (i // spb, 0, i % spb, 0)
        in_specs += [pl.BlockSpec((KV_LORA, MLA_H * HEAD_PAD), const2),
                     pl.BlockSpec((MLA_W, KV_LORA), const2)]
        out_specs += [pl.BlockSpec((1, MLA_H, tm, HEAD_PAD), head_map),
                      pl.BlockSpec((1, MLA_H, tm, HEAD_PAD), head_map),
                      pl.BlockSpec((1, MLA_W, tm), lambda i: (i // spb, 0, i % spb))]
        out_shape += [jax.ShapeDtypeStruct((b, MLA_H, s, HEAD_PAD), BF16),
                      jax.ShapeDtypeStruct((b, MLA_H, s, HEAD_PAD), BF16),
                      jax.ShapeDtypeStruct((b, MLA_W, s), BF16)]
        wk = wts["wk_pad"]
    return pl.pallas_call(
        functools.partial(_front_kernel, absorb),
        grid=(n // tm,),
        in_specs=in_specs,
        out_specs=out_specs,
        out_shape=out_shape,
        compiler_params=pltpu.CompilerParams(dimension_semantics=("arbitrary",),
                                             vmem_limit_bytes=VMEM_LIMIT),
        name="front_sample" if absorb else "front_prompt",
    )(x, mod, mod, tab, wts["w_in"], wts["b_gate"], wts["q_norm"], wts["kv_norm"],
      wts["wqa"], wk, wts["w_uv_t"])


def _mlstm_kernel(q_ref, k_ref, v_ref, g_ref, c0_ref, n0_ref, m0_ref,
                  h_ref, c_ref, n_ref, m_ref):
    tb, L, _ = q_ref.shape
    ci = pl.program_id(1)

    @pl.when(ci == 0)
    def _():
        c_ref[...] = c0_ref[...]
        n_ref[...] = n0_ref[...]
        m_ref[...] = m0_ref[...]

    row = lax.broadcasted_iota(jnp.int32, (L, L), 0)
    col = lax.broadcasted_iota(jnp.int32, (L, L), 1)
    causal = col <= row
    tril = causal.astype(F32)
    sel = (lax.broadcasted_iota(jnp.int32, (8, LANES), 0)
           == lax.broadcasted_iota(jnp.int32, (8, LANES), 1)).astype(F32)

    chains = [(t, hh) for t in range(tb) for hh in range(ML_H)]
    sl = lambda hh: slice(hh * ML_DK, (hh + 1) * ML_DK)
    gates = {}
    for t in range(tb):
        g = g_ref[t]
        fcum = _mm_exact(tril, g)
        g_rows = _nt_exact(sel, g)
        f_rows = _nt_exact(sel, fcum)
        gates[t] = (g, fcum, g_rows, f_rows)

    qk, qc, c_prevs = {}, {}, {}
    for t, hh in chains:
        qb = q_ref[t, :, sl(hh)]
        c_prevs[t, hh] = c_ref[t, hh]
        qk[t, hh] = _nt(qb, k_ref[t, :, sl(hh)])
        qc[t, hh] = _mm(qb, c_prevs[t, hh].astype(BF16))

    stab = {}
    for t, hh in chains:
        g, fcum, g_rows, f_rows = gates[t]
        m_prev = m_ref[t, hh][:, :1]
        f_col = fcum[:, ML_H + hh:ML_H + hh + 1]
        f_row = f_rows[ML_H + hh:ML_H + hh + 1, :]
        ig_row = g_rows[hh:hh + 1, :]
        dmat = jnp.where(causal, f_col - f_row + ig_row, -jnp.inf)
        m_inter = jnp.broadcast_to(f_col, (L, LANES)) + m_prev
        m_t = jnp.maximum(m_inter, jnp.max(dmat, axis=-1, keepdims=True))
        stab[t, hh] = (jnp.exp(dmat - m_t[:, :L]), jnp.exp(m_inter - m_t), m_t)

    svs = {}
    ones_cols = jnp.ones((L, LANES), BF16)
    for t, hh in chains:
        sq = (qk[t, hh] * stab[t, hh][0]).astype(BF16)
        svs[t, hh] = _mm(sq, jnp.concatenate([v_ref[t, :, sl(hh)], ones_cols], axis=1))

    for t, hh in chains:
        _, a, m_t = stab[t, hh]
        n_rep = jnp.broadcast_to(n_ref[t, hh], (LANES, ML_DK)).astype(BF16)
        qn = _nt(q_ref[t, :, sl(hh)], n_rep)
        num = svs[t, hh][:, :ML_DV] + a * qc[t, hh]
        den = svs[t, hh][:, ML_DV:] + a * qn
        h_ref[t, :, sl(hh)] = (num / jnp.maximum(jnp.abs(den), jnp.exp(-m_t))).astype(h_ref.dtype)

    for t, hh in chains:
        w, a, m_t = stab[t, hh]
        wl_row = w[L - 1:L, :]
        al = a[L - 1:L, :]
        kb = k_ref[t, :, sl(hh)]
        kw_t = (kb.astype(F32).T * wl_row).astype(BF16)
        c_ref[t, hh] = al[:, :1] * c_prevs[t, hh] + _mm(kw_t, v_ref[t, :, sl(hh)])
        n_ref[t, hh] = al * n_ref[t, hh] + _mm(w[L - 8:L, :].astype(BF16), kb)[7:8, :]
        m_ref[t, hh] = m_t[L - 1:L, :]


def _mlstm(q, k, v, g, c0, n0, m0, chunk, tb):
    b, s, w = q.shape
    nc = s // chunk
    seq = lambda ww: pl.BlockSpec((tb, chunk, ww), lambda i, c: (i, c, 0))
    st_c = pl.BlockSpec((tb, ML_H, ML_DK, ML_DV), lambda i, c: (i, 0, 0, 0))
    st_v = pl.BlockSpec((tb, ML_H, 1, LANES), lambda i, c: (i, 0, 0, 0))
    return pl.pallas_call(
        _mlstm_kernel,
        grid=(b // tb, nc),
        in_specs=[seq(w), seq(w), seq(w), seq(LANES), st_c, st_v, st_v],
        out_specs=[seq(w), st_c, st_v, st_v],
        out_shape=[jax.ShapeDtypeStruct((b, s, w), BF16),
                   jax.ShapeDtypeStruct((b, ML_H, ML_DK, ML_DV), F32),
                   jax.ShapeDtypeStruct((b, ML_H, 1, LANES), F32),
                   jax.ShapeDtypeStruct((b, ML_H, 1, LANES), F32)],
        compiler_params=pltpu.CompilerParams(dimension_semantics=("arbitrary", "arbitrary"),
                                             vmem_limit_bytes=VMEM_LIMIT),
        name="mlstm",
    )(q, k, v, g, c0, n0, m0)


def _attn_kernel(qi_ref, ki_ref, last_ref, q_ref, k_ref, vt_ref, o_ref, m_sc, l_sc, acc_sc):
    step = pl.program_id(2)
    qi = qi_ref[step]
    ki = ki_ref[step]
    tq = q_ref.shape[2]
    tk = k_ref.shape[2]

    @pl.when(ki == 0)
    def _():
        m_sc[...] = jnp.full(m_sc.shape, -jnp.inf, F32)
        l_sc[...] = jnp.zeros(l_sc.shape, F32)
        acc_sc[...] = jnp.zeros(acc_sc.shape, F32)

    def tile(diag):
        chains = [(hh, c0) for c0 in range(0, min(tk, diag + tq), ATT_KC) for hh in range(2)]
        scores = {}
        for hh, c0 in chains:
            st = _nt(k_ref[0, hh, c0:c0 + ATT_KC, :], q_ref[0, hh])
            if c0 >= diag:
                keys = c0 - diag + lax.broadcasted_iota(jnp.int32, (ATT_KC, tq), 0)
                qrys = lax.broadcasted_iota(jnp.int32, (ATT_KC, tq), 1)
                st = jnp.where(keys <= qrys, st, -jnp.inf)
            scores[hh, c0] = st
        parts = ([], [])
        ones_rows = jnp.ones((16, ATT_KC), BF16)
        for hh, c0 in chains:
            st = scores[hh, c0]
            m_g = jnp.max(st, axis=0, keepdims=True)
            if c0 >= diag:
                m_g = jnp.maximum(m_g, MASKED_MAX_FLOOR)
            pt = jnp.exp2(st - m_g).astype(BF16)
            rows = slice(hh * MLA_VD, (hh + 1) * MLA_VD)
            v_ones = jnp.concatenate([vt_ref[0, rows, c0:c0 + ATT_KC], ones_rows], axis=0)
            o_g = _mm(v_ones, pt)
            parts[hh].append((m_g, o_g[MLA_VD:MLA_VD + 1, :], o_g[:MLA_VD, :]))
        for hh in range(2):
            rows = slice(hh * MLA_VD, (hh + 1) * MLA_VD)
            m_prev = m_sc[hh]
            m_new = functools.reduce(jnp.maximum, [m_g for m_g, _, _ in parts[hh]], m_prev)
            alpha = jnp.exp2(m_prev - m_new)
            l_new = alpha * l_sc[hh]
            acc = alpha * acc_sc[rows, :]
            for m_g, l_g, o_g in parts[hh]:
                w_g = jnp.exp2(m_g - m_new)
                l_new = l_new + w_g * l_g
                acc = acc + w_g * o_g
            m_sc[hh] = m_new
            l_sc[hh] = l_new
            acc_sc[rows, :] = acc

    first_query = qi * tq - ki * tk
    for diag in range(0, tk, tq):
        pl.when(first_query == diag)(functools.partial(tile, diag))
    pl.when(first_query >= tk)(functools.partial(tile, tk))

    @pl.when(last_ref[step] == 1)
    def _():
        out_t = jnp.concatenate([acc_sc[0:MLA_VD, :] / l_sc[0], acc_sc[MLA_VD:, :] / l_sc[1]], axis=0)
        o_ref[0] = out_t.T.astype(o_ref.dtype)


def _attn_prompt(qh, kh, vt):
    b, nh, s, _ = qh.shape
    tq, tk = ATT_TQ, ATT_TK
    assert tk % tq == 0 and tq % ATT_KC == 0 and s % tk == 0
    qi_l, ki_l, last_l = [], [], []
    for qi in range(s // tq):
        nk = ((qi + 1) * tq + tk - 1) // tk
        for ki in range(nk):
            qi_l.append(qi)
            ki_l.append(ki)
            last_l.append(int(ki == nk - 1))
    nsteps = len(qi_l)
    sched = [jnp.asarray(np.asarray(a, np.int32)) for a in (qi_l, ki_l, last_l)]
    grid_spec = pltpu.PrefetchScalarGridSpec(
        num_scalar_prefetch=3,
        grid=(b, nh // 2, nsteps),
        in_specs=[pl.BlockSpec((1, 2, tq, HEAD_PAD), lambda bi, hp, st, qi, ki, la: (bi, hp, qi[st], 0)),
                  pl.BlockSpec((1, 2, tk, HEAD_PAD), lambda bi, hp, st, qi, ki, la: (bi, hp, ki[st], 0)),
                  pl.BlockSpec((1, 2 * MLA_VD, tk), lambda bi, hp, st, qi, ki, la: (bi, hp, ki[st]))],
        out_specs=pl.BlockSpec((1, tq, LANES), lambda bi, hp, st, qi, ki, la: (bi, qi[st], hp)),
        scratch_shapes=[pltpu.VMEM((2, 1, tq), F32), pltpu.VMEM((2, 1, tq), F32),
                        pltpu.VMEM((2 * MLA_VD, tq), F32)],
    )
    return pl.pallas_call(
        _attn_kernel,
        grid_spec=grid_spec,
        out_shape=jax.ShapeDtypeStruct((b, s, nh * MLA_VD), BF16),
        compiler_params=pltpu.CompilerParams(
            dimension_semantics=("arbitrary", "arbitrary", "arbitrary"),
            vmem_limit_bytes=VMEM_LIMIT),
        name="attn_prompt",
    )(*sched, qh, kh, vt)


def _dec_page_copies(pt_ref, pool_ckv, pool_kr, cbuf, kbuf, sems, step, slot):
    copies = []
    for i in range(DEC_PG):
        page = pt_ref[step * DEC_PG + i]
        keys = pl.ds(i * PAGE_SIZE, PAGE_SIZE)
        copies.append(pltpu.make_async_copy(pool_ckv.at[page], cbuf.at[slot, keys, :], sems.at[slot, 0]))
        copies.append(pltpu.make_async_copy(pool_kr.at[page], kbuf.at[slot, :, keys], sems.at[slot, 1]))
    return copies


def _dec_wait_slot(cbuf, kbuf, sems, slot):
    pltpu.make_async_copy(cbuf.at[slot], cbuf.at[slot], sems.at[slot, 0]).wait()
    pltpu.make_async_copy(kbuf.at[slot], kbuf.at[slot], sems.at[slot, 1]).wait()


def _dec_kernel(pt_ref, qlat_ref, qh_ref, ckvn_ref, krn_ref, pool_ckv, pool_kr,
                o_ref, m_sc, l_sc, acc_sc, padc_sc, padk_sc, cbuf, kbuf, sems):
    j = pl.program_id(1)
    nj = pl.num_programs(1)
    step = pl.program_id(0) * nj + j
    last_step = pl.num_programs(0) * nj - 1
    slot = lax.rem(step, DEC_SLOTS)
    qlat = jnp.concatenate([qlat_ref[0, :, hh * KV_LORA:(hh + 1) * KV_LORA] for hh in range(MLA_H)], axis=0)
    qh = jnp.concatenate([qh_ref[0, :, hh * HEAD_PAD:(hh + 1) * HEAD_PAD] for hh in range(MLA_H)], axis=0)
    ds = ckvn_ref.shape[1]
    copies = functools.partial(_dec_page_copies, pt_ref, pool_ckv, pool_kr, cbuf, kbuf, sems)

    @pl.when(step == 0)
    def _():
        for s0 in range(DEC_SLOTS):
            for c in copies(s0, s0):
                c.start()

    _dec_wait_slot(cbuf, kbuf, sems, slot)
    next_copies = copies(step + DEC_SLOTS, slot)

    @pl.when(j == 0)
    def _():
        padc_sc[...] = jnp.zeros(padc_sc.shape, BF16)
        padk_sc[...] = jnp.zeros(padk_sc.shape, BF16)
        padc_sc[0:ds, :] = ckvn_ref[0].astype(BF16)
        padk_sc[0:ds, :] = krn_ref[0]
        cn = padc_sc[...]
        s = _nt(qlat, cn) + _nt(qh, padk_sc[...])
        tok = lax.broadcasted_iota(jnp.int32, (MLA_H, ds, s.shape[1]), 1).reshape(s.shape)
        key = lax.broadcasted_iota(jnp.int32, s.shape, 1)
        s = jnp.where(key <= tok, s, -jnp.inf)
        m = jnp.max(s, axis=-1, keepdims=True)
        p = jnp.exp2(s - m)
        m_sc[...] = m
        l_sc[...] = jnp.sum(p, axis=-1, keepdims=True)
        acc_sc[...] = _mm(p.astype(BF16), cn)

    qr = qh[:, MLA_NOPE:MLA_NOPE + MLA_ROPE]
    pages = cbuf[slot].astype(BF16)
    krt = kbuf[slot].astype(BF16)
    gk = pages.shape[0] // DEC_GROUPS
    groups = [pages[g * gk:(g + 1) * gk] for g in range(DEC_GROUPS)]
    scores, probs, parts = {}, {}, []

    def score(g):
        scores[g] = _nt(qlat, groups[g]) + _mm(qr, krt[:, g * gk:(g + 1) * gk])

    def soft(g):
        m_g = jnp.max(scores[g], axis=-1, keepdims=True)
        p = jnp.exp2(scores[g] - m_g)
        probs[g] = (m_g, jnp.sum(p, axis=-1, keepdims=True), p.astype(BF16))

    def value(g):
        m_g, l_g, pb = probs[g]
        parts.append((m_g, l_g, _mm(pb, groups[g])))

    per_group = len(next_copies) // DEC_GROUPS
    for g in range(DEC_GROUPS):
        score(g)
    soft(0)
    for g in range(DEC_GROUPS):
        if g + 1 < DEC_GROUPS:
            soft(g + 1)
        value(g)
        for c in next_copies[g * per_group:(g + 1) * per_group]:
            c.start()
    m_prev = m_sc[...]
    m_new = functools.reduce(jnp.maximum, [m_g for m_g, _, _ in parts], m_prev)
    alpha = jnp.exp2(m_prev - m_new)
    l_new = alpha * l_sc[...]
    acc = alpha * acc_sc[...]
    for m_g, l_g, o_g in parts:
        w_g = jnp.exp2(m_g - m_new)
        l_new = l_new + w_g * l_g
        acc = acc + w_g * o_g
    m_sc[...] = m_new
    l_sc[...] = l_new
    acc_sc[...] = acc

    @pl.when(j == nj - 1)
    def _():
        out = acc_sc[...] / l_sc[...]
        for hh in range(MLA_H):
            o_ref[0, :, hh * KV_LORA:(hh + 1) * KV_LORA] = out[hh * ds:(hh + 1) * ds, :]

    @pl.when(step == last_step)
    def _():
        for s0 in range(DEC_SLOTS):
            _dec_wait_slot(cbuf, kbuf, sems, s0)


def _attn_sample(qlat, qh, ckvn, krn, pool_ckv, pool_kr, page_table):
    b, ds, _ = qlat.shape
    nq = ds * MLA_H
    n_pages = page_table.shape[1]
    pg = DEC_PG
    pt = page_table.reshape(-1)
    pt = jnp.concatenate([pt, pt[:DEC_SLOTS * pg]])

    fixed = lambda w, r: pl.BlockSpec((1, r, w), lambda bi, j, pt_ref: (bi, 0, 0))
    grid_spec = pltpu.PrefetchScalarGridSpec(
        num_scalar_prefetch=1,
        grid=(b, n_pages // pg),
        in_specs=[fixed(MLA_H * KV_LORA, ds), fixed(MLA_H * HEAD_PAD, ds), fixed(KV_LORA, ds), fixed(HEAD_PAD, ds),
                  pl.BlockSpec(memory_space=pl.ANY), pl.BlockSpec(memory_space=pl.ANY)],
        out_specs=fixed(MLA_H * KV_LORA, ds),
        scratch_shapes=[pltpu.VMEM((nq, 1), F32), pltpu.VMEM((nq, 1), F32),
                        pltpu.VMEM((nq, KV_LORA), F32),
                        pltpu.VMEM((PAGE_SIZE, KV_LORA), BF16), pltpu.VMEM((PAGE_SIZE, HEAD_PAD), BF16),
                        pltpu.VMEM((DEC_SLOTS, pg * PAGE_SIZE, KV_LORA), F32),
                        pltpu.VMEM((DEC_SLOTS, MLA_ROPE, pg * PAGE_SIZE), F32),
                        pltpu.SemaphoreType.DMA((DEC_SLOTS, 2))],
    )
    return pl.pallas_call(
        _dec_kernel,
        grid_spec=grid_spec,
        out_shape=jax.ShapeDtypeStruct((b, ds, MLA_H * KV_LORA), F32),
        compiler_params=pltpu.CompilerParams(dimension_semantics=("arbitrary", "arbitrary"),
                                             vmem_limit_bytes=VMEM_LIMIT),
        name="attn_sample",
    )(pt, qlat, qh, ckvn, krn, pool_ckv, pool_kr)


def _sigmoid(x):
    return 0.5 * jnp.tanh(0.5 * x) + 0.5


def _silu(x):
    h = 0.5 * x
    return h * jnp.tanh(h) + h


def _back_kernel(absorb, x_ref, gate_ref, hml_ref, mo_ref, mz_ref, o_ref, az_ref, gn_ref, wuv_ref,
                 wout_ref, lng_ref, lnb_ref, y_ref):
    tb, ts, d = x_ref.shape
    tm = tb * ts
    hm = hml_ref[...].astype(F32) * _sigmoid(mo_ref[...].astype(F32))
    parts = []
    for hh in range(ML_H):
        t = hm[:, hh * ML_DV:(hh + 1) * ML_DV]
        mu = jnp.mean(t, axis=-1, keepdims=True)
        tc = t - mu
        var = jnp.mean(tc * tc, axis=-1, keepdims=True)
        parts.append(tc * lax.rsqrt(var + EPS))
    mz = mz_ref[...].astype(F32)
    y_ml = jnp.concatenate(parts, axis=1) * gn_ref[...] * _silu(mz)
    if absorb:
        olat = o_ref[...].astype(BF16)
        o_mla = jnp.concatenate(
            [_mm(olat[:, p * 2 * KV_LORA:(p + 1) * 2 * KV_LORA], wuv_ref[p]) for p in range(MLA_H // 2)],
            axis=1)
    else:
        o_mla = o_ref[...].astype(F32)
    az = az_ref[...].astype(F32)
    y_mla = o_mla * _silu(az)
    out = _mm(y_ml.astype(BF16), wout_ref[0:ML_W, :]) + _mm(y_mla.astype(BF16), wout_ref[ML_W:, :])
    z = ALPHA * x_ref[...] + gate_ref[...] * out.reshape(tb, ts, d)
    mu = jnp.mean(z, axis=-1, keepdims=True)
    zc = z - mu
    var = jnp.mean(zc * zc, axis=-1, keepdims=True)
    y_ref[...] = zc * lax.rsqrt(var + EPS) * lng_ref[...] + lnb_ref[...]


def _back(absorb, x, mod, hml, mo, mz, o, az, wts):
    b, s, d = x.shape
    n = b * s
    tm = BACK_TM
    if absorb:
        tb, ts = tm // s, s
        x_map = lambda i: (i, 0, 0)
        gate_map = lambda i: (i, 0, 2)
    else:
        tb, ts = 1, tm
        spb = s // tm
        x_map = lambda i: (i // spb, i % spb, 0)
        gate_map = lambda i: (i // spb, 0, 2)
    const2 = lambda i: (0, 0)
    tok = lambda w: pl.BlockSpec((tm, w), lambda i: (i, 0))
    ow = o.shape[1]
    return pl.pallas_call(
        functools.partial(_back_kernel, absorb),
        grid=(n // tm,),
        in_specs=[pl.BlockSpec((tb, ts, d), x_map),
                  pl.BlockSpec((tb, 1, d), gate_map),
                  tok(ML_W), tok(ML_W), tok(ML_W), tok(ow), tok(MLA_W),
                  pl.BlockSpec((1, ML_W), const2),
                  pl.BlockSpec((MLA_H // 2, 2 * KV_LORA, LANES), lambda i: (0, 0, 0)),
                  pl.BlockSpec((d, d), const2),
                  pl.BlockSpec((1, d), const2),
                  pl.BlockSpec((1, d), const2)],
        out_specs=pl.BlockSpec((tb, ts, d), x_map),
        out_shape=jax.ShapeDtypeStruct((b, s, d), F32),
        compiler_params=pltpu.CompilerParams(dimension_semantics=("arbitrary",),
                                             vmem_limit_bytes=VMEM_LIMIT),
        name="back_sample" if absorb else "back_prompt",
    )(x, mod, hml, mo, mz, o, az, wts["gn"], wts["wuv_pair"], wts["w_out"], wts["ln_g"], wts["ln_b"])


def _prep_weights(l, w_in, ml_b_i, ml_b_f, ml_gn, mla_q_norm, mla_kv_norm, mla_w_uq, mla_w_uk,
                  mla_w_uv, w_out, ln_g, ln_b):
    assert w_in.shape[2] == N_IN
    tail = HEAD_PAD - MLA_NOPE - MLA_ROPE
    w_in_p = _wprep(jnp.swapaxes(w_in[l], 0, 1))
    b_gate = jnp.concatenate([ml_b_i[l], ml_b_f[l], jnp.zeros((LANES - 2 * ML_H,), F32)]).reshape(1, LANES)
    uq = mla_w_uq[l].reshape(Q_LORA, MLA_H, MLA_NOPE + MLA_ROPE)
    nope, r1, r2 = uq[..., :MLA_NOPE], uq[..., MLA_NOPE:MLA_NOPE + ROPE_HALF], uq[..., MLA_NOPE + ROPE_HALF:]
    zq = lambda c: jnp.zeros((Q_LORA, MLA_H, c), F32)
    wqa = jnp.concatenate([nope, r1, r2, zq(tail)], axis=-1).reshape(Q_LORA, MLA_H * HEAD_PAD).astype(BF16)
    uk = mla_w_uk[l].reshape(KV_LORA, MLA_H, MLA_NOPE)
    wk_pad = jnp.concatenate([uk, jnp.zeros((KV_LORA, MLA_H, HEAD_PAD - MLA_NOPE), F32)],
                             axis=-1).reshape(KV_LORA, MLA_H * HEAD_PAD).astype(BF16)
    wukt = jnp.concatenate([jnp.transpose(uk, (1, 2, 0)),
                            jnp.zeros((MLA_H, HEAD_PAD - MLA_NOPE, KV_LORA), F32)], axis=1).astype(BF16)
    uv = mla_w_uv[l].reshape(KV_LORA, MLA_H // 2, 2, MLA_VD)
    zv = jnp.zeros((KV_LORA, MLA_H // 2, MLA_VD), F32)
    wuv_pair = jnp.concatenate(
        [jnp.concatenate([uv[:, :, 0], zv], axis=-1), jnp.concatenate([zv, uv[:, :, 1]], axis=-1)],
        axis=0)
    wuv_pair = jnp.transpose(wuv_pair, (1, 0, 2)).astype(BF16)
    return dict(w_in=w_in_p, b_gate=b_gate, q_norm=mla_q_norm[l].reshape(1, -1),
                kv_norm=mla_kv_norm[l].reshape(1, -1), wqa=wqa, wk_pad=wk_pad, wukt=wukt,
                w_uv_t=mla_w_uv[l].T.astype(BF16), wuv_pair=wuv_pair, gn=ml_gn[l].reshape(1, -1),
                w_out=w_out[l].astype(BF16), ln_g=ln_g[l].reshape(1, -1), ln_b=ln_b[l].reshape(1, -1))


def _rope_tables(pos):
    f32 = np.float32
    inv = ROPE_THETA ** (-np.arange(ROPE_HALF, dtype=np.float64) / ROPE_HALF)
    ang = pos.astype(np.float64)[:, None] * inv[None, :]
    n = pos.shape[0]
    tail = HEAD_PAD - MLA_NOPE - MLA_ROPE
    cos, sin = np.cos(ang).astype(f32), np.sin(ang).astype(f32)
    zeros = lambda c: np.zeros((n, c), f32)
    cm = np.concatenate([np.ones((n, MLA_NOPE), f32), cos, cos, zeros(tail)], 1)
    s_lo = np.concatenate([zeros(MLA_NOPE), -sin, zeros(ROPE_HALF), zeros(tail)], 1)
    s_hi = np.concatenate([zeros(MLA_NOPE), zeros(ROPE_HALF), sin, zeros(tail)], 1)
    return np.concatenate([cm, s_lo, s_hi], 1)


def kernel(x_prompt, x_sample, c_prompt, c_sample, cache_ckv, cache_krope, state_C, state_n, state_m,
           page_table, w_ada, b_ada, w_in, ml_b_i, ml_b_f, ml_gn, mla_q_norm, mla_kv_norm,
           mla_w_uq, mla_w_uk, mla_w_uv, w_out, ln_g, ln_b):
    bp, sp, d = x_prompt.shape
    bs, ss, _ = x_sample.shape
    past = page_table.shape[1] * PAGE_SIZE
    tab_p = jnp.asarray(_rope_tables(np.arange(sp)))
    tab_s = jnp.asarray(np.tile(_rope_tables(past + np.arange(ss)), (FRONT_TM // ss, 1)))
    pad = (-(bp + bs)) % 8
    c_all = jnp.concatenate([c_prompt, c_sample, jnp.zeros((pad, d), F32)], axis=0)

    xp, xs = x_prompt, x_sample
    outs = [[] for _ in range(10)]
    for l in range(DEPTH):
        wts = _prep_weights(l, w_in, ml_b_i, ml_b_f, ml_gn, mla_q_norm, mla_kv_norm, mla_w_uq, mla_w_uk,
                            mla_w_uv, w_out, ln_g, ln_b)
        mod = _ada(c_all, w_ada[l], b_ada[l])
        mod_p = mod[:bp].reshape(bp, 1, 3 * d)
        mod_s = mod[bp:bp + bs].reshape(bs, 1, 3 * d)

        mq, mk, mv, g, mo, mz, az, ckv, kr, qh, kh, vt = _front(False, xp, mod_p, tab_p, wts)
        seq3 = lambda t: t.reshape(bp, sp, -1)
        c0 = jnp.zeros((bp, ML_H, ML_DK, ML_DV), F32)
        n0 = jnp.zeros((bp, ML_H, 1, ML_DK), F32)
        m0 = jnp.full((bp, ML_H, 1, LANES), -jnp.inf, F32)
        hml, c_p, n_p, m_p = _mlstm(seq3(mq), seq3(mk), seq3(mv), seq3(g), c0, n0, m0, ML_CHUNK, bp)
        o_p = _attn_prompt(qh, kh, vt)
        yp = _back(False, xp, mod_p, hml.reshape(bp * sp, -1), mo, mz, o_p.reshape(bp * sp, -1), az, wts)
        outs[0].append(ckv.reshape(bp, sp, KV_LORA))
        outs[1].append(jnp.swapaxes(kr, 1, 2))
        outs[2].append(c_p)
        outs[3].append(n_p[:, :, 0, :])
        outs[4].append(m_p[:, :, 0, 0])

        mq, mk, mv, g, mo, mz, az, ckv, kr, qlat, qh, krp = _front(True, xs, mod_s, tab_s, wts)
        seq3 = lambda t: t.reshape(bs, ss, -1)
        n0 = state_n[l].astype(F32).reshape(bs, ML_H, 1, ML_DK)
        m0 = jnp.broadcast_to(state_m[l].astype(F32)[:, :, None, None], (bs, ML_H, 1, LANES))
        hml, c_s, n_s, m_s = _mlstm(seq3(mq), seq3(mk), seq3(mv), seq3(g), state_C[l].astype(F32), n0, m0, ss, 8)
        o_s = _attn_sample(seq3(qlat), seq3(qh), seq3(ckv), seq3(krp), cache_ckv[l],
                           jnp.swapaxes(cache_krope[l], 1, 2), page_table)
        ys = _back(True, xs, mod_s, hml.reshape(bs * ss, -1), mo, mz,
                   o_s.reshape(bs * ss, MLA_H * KV_LORA), az, wts)
        outs[5].append(ckv.reshape(bs, ss, KV_LORA).astype(cache_ckv.dtype))
        outs[6].append(kr.reshape(bs, ss, MLA_ROPE).astype(cache_krope.dtype))
        outs[7].append(c_s.astype(state_C.dtype))
        outs[8].append(n_s[:, :, 0, :].astype(state_n.dtype))
        outs[9].append(m_s[:, :, 0, 0].astype(state_m.dtype))
        xp, xs = yp, ys
    return (xp, xs) + tuple(jnp.stack(o) for o in outs)
```

```python
import functools

import numpy as np
import jax
import jax.numpy as jnp
from jax import lax
from jax.experimental import pallas as pl
from jax.experimental.pallas import tpu as pltpu

F32 = jnp.float32
BF16 = jnp.bfloat16

D_MODEL = 1024
DEPTH = 1
PAGE_SIZE = 128
ML_W = 512
MLA_W = 512
ML_DK = 128
ML_DV = 128
ML_H = 4
MLA_VD = 64
MLA_H = 8
MLA_NOPE = 64
MLA_ROPE = 32
ROPE_HALF = MLA_ROPE // 2
MLA_SCALE = (MLA_NOPE + MLA_ROPE) ** -0.5
Q_SCALE = MLA_SCALE * float(np.log2(np.e))
Q_LORA = 384
KV_LORA = 256
ROPE_THETA = 10000.0
ML_CHUNK = 128
EPS = 1e-6
ALPHA = (2.0 * DEPTH) ** 0.25

LANES = 128
HEAD_PAD = 128
VMEM_LIMIT = 48 * 1024 * 1024

_SEG = {}
_off = 0
for _name, _w in (("mq", 512), ("mk", 512), ("mv", 512), ("gk", 256), ("mo", 512), ("mz", 512),
                  ("cq", Q_LORA), ("ckv", KV_LORA), ("az", 512)):
    _SEG[_name] = (_off, _off + _w)
    _off += _w
N_IN_PAD = _off

FRONT_TM = 512
BACK_TM = 512
ATT_TQ = 1024
ATT_TK = 2048
ATT_KC = 256
MASKED_MAX_FLOOR = -1e30
DEC_PG = 64
DEC_GROUPS = 8
DEC_SLOTS = 2


def _nt(a, b):
    return lax.dot_general(a, b, (((1,), (1,)), ((), ())), preferred_element_type=F32)


def _tn(a, b):
    return lax.dot_general(a, b, (((0,), (0,)), ((), ())), preferred_element_type=F32)


def _mm(a, b):
    return jnp.dot(a, b, preferred_element_type=F32)


def _mm_exact(a, b):
    return jnp.dot(a, b, preferred_element_type=F32, precision=lax.Precision.HIGHEST)


def _nt_exact(a, b):
    return lax.dot_general(a, b, (((1,), (1,)), ((), ())), preferred_element_type=F32,
                           precision=lax.Precision.HIGHEST)


def _ada_kernel(c_ref, w_ref, b_ref, o_ref):
    o_ref[...] = _mm(c_ref[...].astype(BF16), w_ref[...].astype(BF16)) + b_ref[...]


def _ada(c_all, w_ada, b_ada):
    m = c_all.shape[0]
    tn = 1024
    return pl.pallas_call(
        _ada_kernel,
        grid=(3 * D_MODEL // tn,),
        in_specs=[pl.BlockSpec((m, D_MODEL), lambda j: (0, 0)),
                  pl.BlockSpec((D_MODEL, tn), lambda j: (0, j)),
                  pl.BlockSpec((1, tn), lambda j: (0, j))],
        out_specs=pl.BlockSpec((m, tn), lambda j: (0, j)),
        out_shape=jax.ShapeDtypeStruct((m, 3 * D_MODEL), F32),
        name="ada",
    )(c_all, w_ada, b_ada.reshape(1, -1))


_IN_ROWS = {}
_off = 0
for _name, _w in (("mq", 512), ("mk", 512), ("mv", 512), ("gates", 2 * ML_H), ("mo", 512), ("mz", 512),
                  ("cq", Q_LORA), ("ckv", KV_LORA), ("kr", MLA_ROPE), ("az", 512)):
    _IN_ROWS[_name] = (_off, _off + _w)
    _off += _w
N_IN = _off


def _wprep_kernel(wt_ref, o_ref):
    dk = wt_ref.shape[1]
    rows = lambda name: wt_ref[_IN_ROWS[name][0]:_IN_ROWS[name][1], :]
    zeros = lambda r: jnp.zeros((r, dk), F32)
    tail = HEAD_PAD - MLA_NOPE - MLA_ROPE
    gk = jnp.concatenate([rows("gates"), zeros(LANES - 2 * ML_H), zeros(MLA_NOPE), rows("kr"), zeros(tail)], axis=0)
    for name in _SEG:
        lo, hi = _SEG[name]
        o_ref[:, lo:hi] = (gk if name == "gk" else rows(name)).T.astype(BF16)


def _wprep(w_in_t):
    n_in, d = w_in_t.shape
    dk = 256
    return pl.pallas_call(
        _wprep_kernel,
        grid=(d // dk,),
        in_specs=[pl.BlockSpec((n_in, dk), lambda i: (0, i))],
        out_specs=pl.BlockSpec((dk, N_IN_PAD), lambda i: (i, 0)),
        out_shape=jax.ShapeDtypeStruct((d, N_IN_PAD), BF16),
        compiler_params=pltpu.CompilerParams(dimension_semantics=("arbitrary",),
                                             vmem_limit_bytes=VMEM_LIMIT),
        name="wprep",
    )(w_in_t)


def _rms(x, g):
    return x * lax.rsqrt(jnp.mean(x * x, axis=-1, keepdims=True) + EPS) * g


def _log_sigmoid(x):
    return jnp.minimum(x, 0.0) - jnp.log1p(jnp.exp(-jnp.abs(x)))


def _rotary(a, tab_ref):
    cm = tab_ref[:, 0:LANES]
    s_lo = tab_ref[:, LANES:2 * LANES]
    s_hi = tab_ref[:, 2 * LANES:3 * LANES]
    return a * cm + pltpu.roll(a, LANES - ROPE_HALF, 1) * s_lo + pltpu.roll(a, ROPE_HALF, 1) * s_hi


def _front_kernel(absorb, x_ref, sh_ref, sc_ref, tab_ref, win_ref, bg_ref, qn_ref, kvn_ref,
                  wqa_ref, wk_ref, wv_ref,
                  mq_ref, mk_ref, mv_ref, g_ref, mo_ref, mz_ref, az_ref, ckv_ref, kr_ref,
                  o1_ref, o2_ref, o3_ref):
    tb, ts, d = x_ref.shape
    tm = tb * ts
    h = x_ref[...] * (1.0 + sc_ref[...]) + sh_ref[...]
    h = h.reshape(tm, d).astype(BF16)

    def seg(name):
        lo, hi = _SEG[name]
        return _mm(h, win_ref[:, lo:hi])

    mq_ref[...] = seg("mq").astype(BF16)
    mk_ref[...] = (seg("mk") * (ML_DK ** -0.5)).astype(BF16)
    mv_ref[...] = seg("mv").astype(BF16)
    gk = seg("gk")
    gz = gk[:, :LANES] + bg_ref[...]
    lane = lax.broadcasted_iota(jnp.int32, gz.shape, 1)
    g_ref[...] = jnp.where(lane < ML_H, gz, jnp.where(lane < 2 * ML_H, _log_sigmoid(gz), 0.0))
    mo_ref[...] = seg("mo").astype(BF16)
    mz_ref[...] = seg("mz").astype(BF16)
    az_ref[...] = seg("az").astype(BF16)

    ckvn = _rms(seg("ckv"), kvn_ref[...])
    ckv_ref[...] = ckvn
    ckvn_b = ckvn.astype(BF16)
    krp = _rotary(gk[:, LANES:], tab_ref)
    if absorb:
        kr_ref[...] = krp[:, MLA_NOPE:MLA_NOPE + MLA_ROPE]
    else:
        kr_ref[0] = krp.T[MLA_NOPE:MLA_NOPE + MLA_ROPE, :]

    cqn = _rms(seg("cq"), qn_ref[...]).astype(BF16)
    qa = _mm(cqn, wqa_ref[...])
    for hh in range(MLA_H):
        sl = slice(hh * HEAD_PAD, (hh + 1) * HEAD_PAD)
        qh = (_rotary(qa[:, sl], tab_ref) * Q_SCALE).astype(BF16)
        if absorb:
            o1_ref[:, hh * KV_LORA:(hh + 1) * KV_LORA] = _mm(qh, wk_ref[hh]).astype(BF16)
            o2_ref[:, sl] = qh
        else:
            o1_ref[0, hh] = qh
            o2_ref[0, hh] = (_mm(ckvn_b, wk_ref[:, sl]) + krp).astype(BF16)
    if absorb:
        o3_ref[...] = krp.astype(BF16)
    else:
        o3_ref[0] = _nt(wv_ref[...], ckvn_b).astype(BF16)


def _front(absorb, x, mod, tab, wts):
    b, s, d = x.shape
    n = b * s
    tm = FRONT_TM
    if absorb:
        tb, ts = tm // s, s
        x_map = lambda i: (i, 0, 0)
        mod_map = lambda k: (lambda i: (i, 0, k))
        tab_map = lambda i: (0, 0)
    else:
        tb, ts = 1, tm
        spb = s // tm
        x_map = lambda i: (i // spb, i % spb, 0)
        mod_map = lambda k: (lambda i: (i // spb, 0, k))
        tab_map = lambda i: (i % spb, 0)
    const2 = lambda i: (0, 0)
    tok = lambda w: pl.BlockSpec((tm, w), lambda i: (i, 0))
    in_specs = [
        pl.BlockSpec((tb, ts, d), x_map),
        pl.BlockSpec((tb, 1, d), mod_map(0)),
        pl.BlockSpec((tb, 1, d), mod_map(1)),
        pl.BlockSpec((tm, 3 * LANES), tab_map),
        pl.BlockSpec((d, N_IN_PAD), const2),
        pl.BlockSpec((1, LANES), const2),
        pl.BlockSpec((1, Q_LORA), const2),
        pl.BlockSpec((1, KV_LORA), const2),
        pl.BlockSpec((Q_LORA, MLA_H * HEAD_PAD), const2),
    ]
    out_specs = [tok(512), tok(512), tok(512), tok(LANES), tok(512), tok(512), tok(512),
                 tok(KV_LORA), tok(MLA_ROPE)]
    out_shape = [jax.ShapeDtypeStruct((n, w), dt)
                 for w, dt in ((512, BF16), (512, BF16), (512, BF16), (LANES, F32), (512, BF16), (512, BF16),
                               (512, BF16), (KV_LORA, F32), (MLA_ROPE, F32))]
    if not absorb:
        out_specs[8] = pl.BlockSpec((1, MLA_ROPE, tm), lambda i: (i // spb, 0, i % spb))
        out_shape[8] = jax.ShapeDtypeStruct((b, MLA_ROPE, s), F32)
    if absorb:
        in_specs += [pl.BlockSpec((MLA_H, HEAD_PAD, KV_LORA), lambda i: (0, 0, 0)),
                     pl.BlockSpec((MLA_W, KV_LORA), const2)]
        out_specs += [tok(MLA_H * KV_LORA), tok(MLA_H * HEAD_PAD), tok(HEAD_PAD)]
        out_shape += [jax.ShapeDtypeStruct((n, MLA_H * KV_LORA), BF16),
                      jax.ShapeDtypeStruct((n, MLA_H * HEAD_PAD), BF16),
                      jax.ShapeDtypeStruct((n, HEAD_PAD), BF16)]
        wk = wts["wukt"]
    else:
        head_map = lambda i: (i // spb, 0, i % spb, 0)
        in_specs += [pl.BlockSpec((KV_LORA, MLA_H * HEAD_PAD), const2),
                     pl.BlockSpec((MLA_W, KV_LORA), const2)]
        out_specs += [pl.BlockSpec((1, MLA_H, tm, HEAD_PAD), head_map),
                      pl.BlockSpec((1, MLA_H, tm, HEAD_PAD), head_map),
                      pl.BlockSpec((1, MLA_W, tm), lambda i: (i // spb, 0, i % spb))]
        out_shape += [jax.ShapeDtypeStruct((b, MLA_H, s, HEAD_PAD), BF16),
                      jax.ShapeDtypeStruct((b, MLA_H, s, HEAD_PAD), BF16),
                      jax.ShapeDtypeStruct((b, MLA_W, s), BF16)]
        wk = wts["wk_pad"]
    return pl.pallas_call(
        functools.partial(_front_kernel, absorb),
        grid=(n // tm,),
        in_specs=in_specs,
        out_specs=out_specs,
        out_shape=out_shape,
        compiler_params=pltpu.CompilerParams(dimension_semantics=("arbitrary",),
                                             vmem_limit_bytes=VMEM_LIMIT),
        name="front_sample" if absorb else "front_prompt",
    )(x, mod, mod, tab, wts["w_in"], wts["b_gate"], wts["q_norm"], wts["kv_norm"],
      wts["wqa"], wk, wts["w_uv_t"])


def _mlstm_kernel(q_ref, k_ref, v_ref, g_ref, c0_ref, n0_ref, m0_ref,
                  h_ref, c_ref, n_ref, m_ref):
    tb, L, _ = q_ref.shape
    ci = pl.program_id(1)

    @pl.when(ci == 0)
    def _():
        c_ref[...] = c0_ref[...]
        n_ref[...] = n0_ref[...]
        m_ref[...] = m0_ref[...]

    row = lax.broadcasted_iota(jnp.int32, (L, L), 0)
    col = lax.broadcasted_iota(jnp.int32, (L, L), 1)
    causal = col <= row
    tril = causal.astype(F32)
    sel = (lax.broadcasted_iota(jnp.int32, (8, LANES), 0)
           == lax.broadcasted_iota(jnp.int32, (8, LANES), 1)).astype(F32)

    chains = [(t, hh) for t in range(tb) for hh in range(ML_H)]
    sl = lambda hh: slice(hh * ML_DK, (hh + 1) * ML_DK)
    gates = {}
    for t in range(tb):
        g = g_ref[t]
        fcum = _mm_exact(tril, g)
        g_rows = _nt_exact(sel, g)
        f_rows = _nt_exact(sel, fcum)
        gates[t] = (g, fcum, g_rows, f_rows)

    qk, qc, c_prevs = {}, {}, {}
    for t, hh in chains:
        qb = q_ref[t, :, sl(hh)]
        c_prevs[t, hh] = c_ref[t, hh]
        qk[t, hh] = _nt(qb, k_ref[t, :, sl(hh)])
        qc[t, hh] = _mm(qb, c_prevs[t, hh].astype(BF16))

    stab = {}
    for t, hh in chains:
        g, fcum, g_rows, f_rows = gates[t]
        m_prev = m_ref[t, hh][:, :1]
        f_col = fcum[:, ML_H + hh:ML_H + hh + 1]
        f_row = f_rows[ML_H + hh:ML_H + hh + 1, :]
        ig_row = g_rows[hh:hh + 1, :]
        dmat = jnp.where(causal, f_col - f_row + ig_row, -jnp.inf)
        m_inter = jnp.broadcast_to(f_col, (L, LANES)) + m_prev
        m_t = jnp.maximum(m_inter, jnp.max(dmat, axis=-1, keepdims=True))
        stab[t, hh] = (jnp.exp(dmat - m_t[:, :L]), jnp.exp(m_inter - m_t), m_t)

    svs = {}
    ones_cols = jnp.ones((L, LANES), BF16)
    for t, hh in chains:
        sq = (qk[t, hh] * stab[t, hh][0]).astype(BF16)
        svs[t, hh] = _mm(sq, jnp.concatenate([v_ref[t, :, sl(hh)], ones_cols], axis=1))

    for t, hh in chains:
        _, a, m_t = stab[t, hh]
        n_rep = jnp.broadcast_to(n_ref[t, hh], (LANES, ML_DK)).astype(BF16)
        qn = _nt(q_ref[t, :, sl(hh)], n_rep)
        num = svs[t, hh][:, :ML_DV] + a * qc[t, hh]
        den = svs[t, hh][:, ML_DV:] + a * qn
        h_ref[t, :, sl(hh)] = (num / jnp.maximum(jnp.abs(den), jnp.exp(-m_t))).astype(h_ref.dtype)

    for t, hh in chains:
        w, a, m_t = stab[t, hh]
        wl_row = w[L - 1:L, :]
        al = a[L - 1:L, :]
        kb = k_ref[t, :, sl(hh)]
        kw_t = (kb.astype(F32).T * wl_row).astype(BF16)
        c_ref[t, hh] = al[:, :1] * c_prevs[t, hh] + _mm(kw_t, v_ref[t, :, sl(hh)])
        n_ref[t, hh] = al * n_ref[t, hh] + _mm(w[L - 8:L, :].astype(BF16), kb)[7:8, :]
        m_ref[t, hh] = m_t[L - 1:L, :]


def _mlstm(q, k, v, g, c0, n0, m0, chunk, tb):
    b, s, w = q.shape
    nc = s // chunk
    seq = lambda ww: pl.BlockSpec((tb, chunk, ww), lambda i, c: (i, c, 0))
    st_c = pl.BlockSpec((tb, ML_H, ML_DK, ML_DV), lambda i, c: (i, 0, 0, 0))
    st_v = pl.BlockSpec((tb, ML_H, 1, LANES), lambda i, c: (i, 0, 0, 0))
    return pl.pallas_call(
        _mlstm_kernel,
        grid=(b // tb, nc),
        in_specs=[seq(w), seq(w), seq(w), seq(LANES), st_c, st_v, st_v],
        out_specs=[seq(w), st_c, st_v, st_v],
        out_shape=[jax.ShapeDtypeStruct((b, s, w), BF16),
                   jax.ShapeDtypeStruct((b, ML_H, ML_DK, ML_DV), F32),
                   jax.ShapeDtypeStruct((b, ML_H, 1, LANES), F32),
                   jax.ShapeDtypeStruct((b, ML_H, 1, LANES), F32)],
        compiler_params=pltpu.CompilerParams(dimension_semantics=("arbitrary", "arbitrary"),
                                             vmem_limit_bytes=VMEM_LIMIT),
        name="mlstm",
    )(q, k, v, g, c0, n0, m0)


def _attn_kernel(qi_ref, ki_ref, last_ref, q_ref, k_ref, vt_ref, o_ref, m_sc, l_sc, acc_sc):
    step = pl.program_id(2)
    qi = qi_ref[step]
    ki = ki_ref[step]
    tq = q_ref.shape[2]
    tk = k_ref.shape[2]

    @pl.when(ki == 0)
    def _():
        m_sc[...] = jnp.full(m_sc.shape, -jnp.inf, F32)
        l_sc[...] = jnp.zeros(l_sc.shape, F32)
        acc_sc[...] = jnp.zeros(acc_sc.shape, F32)

    def tile(diag):
        chains = [(hh, c0) for c0 in range(0, min(tk, diag + tq), ATT_KC) for hh in range(2)]
        scores = {}
        for hh, c0 in chains:
            st = _nt(k_ref[0, hh, c0:c0 + ATT_KC, :], q_ref[0, hh])
            if c0 >= diag:
                keys = c0 - diag + lax.broadcasted_iota(jnp.int32, (ATT_KC, tq), 0)
                qrys = lax.broadcasted_iota(jnp.int32, (ATT_KC, tq), 1)
                st = jnp.where(keys <= qrys, st, -jnp.inf)
            scores[hh, c0] = st
        parts = ([], [])
        ones_rows = jnp.ones((16, ATT_KC), BF16)
        for hh, c0 in chains:
            st = scores[hh, c0]
            m_g = jnp.max(st, axis=0, keepdims=True)
            if c0 >= diag:
                m_g = jnp.maximum(m_g, MASKED_MAX_FLOOR)
            pt = jnp.exp2(st - m_g).astype(BF16)
            rows = slice(hh * MLA_VD, (hh + 1) * MLA_VD)
            v_ones = jnp.concatenate([vt_ref[0, rows, c0:c0 + ATT_KC], ones_rows], axis=0)
            o_g = _mm(v_ones, pt)
            parts[hh].append((m_g, o_g[MLA_VD:MLA_VD + 1, :], o_g[:MLA_VD, :]))
        for hh in range(2):
            rows = slice(hh * MLA_VD, (hh + 1) * MLA_VD)
            m_prev = m_sc[hh]
            m_new = functools.reduce(jnp.maximum, [m_g for m_g, _, _ in parts[hh]], m_prev)
            alpha = jnp.exp2(m_prev - m_new)
            l_new = alpha * l_sc[hh]
            acc = alpha * acc_sc[rows, :]
            for m_g, l_g, o_g in parts[hh]:
                w_g = jnp.exp2(m_g - m_new)
                l_new = l_new + w_g * l_g
                acc = acc + w_g * o_g
            m_sc[hh] = m_new
            l_sc[hh] = l_new
            acc_sc[rows, :] = acc

    first_query = qi * tq - ki * tk
    for diag in range(0, tk, tq):
        pl.when(first_query == diag)(functools.partial(tile, diag))
    pl.when(first_query >= tk)(functools.partial(tile, tk))

    @pl.when(last_ref[step] == 1)
    def _():
        out_t = jnp.concatenate([acc_sc[0:MLA_VD, :] / l_sc[0], acc_sc[MLA_VD:, :] / l_sc[1]], axis=0)
        o_ref[0] = out_t.T.astype(o_ref.dtype)


def _attn_prompt(qh, kh, vt):
    b, nh, s, _ = qh.shape
    tq, tk = ATT_TQ, ATT_TK
    assert tk % tq == 0 and tq % ATT_KC == 0 and s % tk == 0
    qi_l, ki_l, last_l = [], [], []
    for qi in range(s // tq):
        nk = ((qi + 1) * tq + tk - 1) // tk
        for ki in range(nk):
            qi_l.append(qi)
            ki_l.append(ki)
            last_l.append(int(ki == nk - 1))
    nsteps = len(qi_l)
    sched = [jnp.asarray(np.asarray(a, np.int32)) for a in (qi_l, ki_l, last_l)]
    grid_spec = pltpu.PrefetchScalarGridSpec(
        num_scalar_prefetch=3,
        grid=(b, nh // 2, nsteps),
        in_specs=[pl.BlockSpec((1, 2, tq, HEAD_PAD), lambda bi, hp, st, qi, ki, la: (bi, hp, qi[st], 0)),
                  pl.BlockSpec((1, 2, tk, HEAD_PAD), lambda bi, hp, st, qi, ki, la: (bi, hp, ki[st], 0)),
                  pl.BlockSpec((1, 2 * MLA_VD, tk), lambda bi, hp, st, qi, ki, la: (bi, hp, ki[st]))],
        out_specs=pl.BlockSpec((1, tq, LANES), lambda bi, hp, st, qi, ki, la: (bi, qi[st], hp)),
        scratch_shapes=[pltpu.VMEM((2, 1, tq), F32), pltpu.VMEM((2, 1, tq), F32),
                        pltpu.VMEM((2 * MLA_VD, tq), F32)],
    )
    return pl.pallas_call(
        _attn_kernel,
        grid_spec=grid_spec,
        out_shape=jax.ShapeDtypeStruct((b, s, nh * MLA_VD), BF16),
        compiler_params=pltpu.CompilerParams(
            dimension_semantics=("arbitrary", "arbitrary", "arbitrary"),
            vmem_limit_bytes=VMEM_LIMIT),
        name="attn_prompt",
    )(*sched, qh, kh, vt)


def _dec_page_copies(pt_ref, pool_ckv, pool_kr, cbuf, kbuf, sems, step, slot):
    copies = []
    for i in range(DEC_PG):
        page = pt_ref[step * DEC_PG + i]
        keys = pl.ds(i * PAGE_SIZE, PAGE_SIZE)
        copies.append(pltpu.make_async_copy(pool_ckv.at[page], cbuf.at[slot, keys, :], sems.at[slot, 0]))
        copies.append(pltpu.make_async_copy(pool_kr.at[page], kbuf.at[slot, i], sems.at[slot, 1]))
    return copies


def _dec_wait_slot(cbuf, kbuf, sems, slot):
    pltpu.make_async_copy(cbuf.at[slot], cbuf.at[slot], sems.at[slot, 0]).wait()
    pltpu.make_async_copy(kbuf.at[slot], kbuf.at[slot], sems.at[slot, 1]).wait()


def _dec_kernel(pt_ref, qlat_ref, qh_ref, ckvn_ref, krn_ref, pool_ckv, pool_kr,
                o_ref, m_sc, l_sc, acc_sc, padc_sc, padk_sc, cbuf, kbuf, sems):
    j = pl.program_id(1)
    nj = pl.num_programs(1)
    step = pl.program_id(0) * nj + j
    last_step = pl.num_programs(0) * nj - 1
    slot = lax.rem(step, DEC_SLOTS)
    qlat = jnp.concatenate([qlat_ref[0, :, hh * KV_LORA:(hh + 1) * KV_LORA] for hh in range(MLA_H)], axis=0)
    qh = jnp.concatenate([qh_ref[0, :, hh * HEAD_PAD:(hh + 1) * HEAD_PAD] for hh in range(MLA_H)], axis=0)
    ds = ckvn_ref.shape[1]
    copies = functools.partial(_dec_page_copies, pt_ref, pool_ckv, pool_kr, cbuf, kbuf, sems)

    @pl.when(step == 0)
    def _():
        for s0 in range(DEC_SLOTS):
            for c in copies(s0, s0):
                c.start()

    _dec_wait_slot(cbuf, kbuf, sems, slot)
    next_copies = copies(step + DEC_SLOTS, slot)

    @pl.when(j == 0)
    def _():
        padc_sc[...] = jnp.zeros(padc_sc.shape, BF16)
        padk_sc[...] = jnp.zeros(padk_sc.shape, BF16)
        padc_sc[0:ds, :] = ckvn_ref[0].astype(BF16)
        padk_sc[0:ds, :] = krn_ref[0]
        cn = padc_sc[...]
        s = _nt(qlat, cn) + _nt(qh, padk_sc[...])
        tok = lax.broadcasted_iota(jnp.int32, (MLA_H, ds, s.shape[1]), 1).reshape(s.shape)
        key = lax.broadcasted_iota(jnp.int32, s.shape, 1)
        s = jnp.where(key <= tok, s, -jnp.inf)
        m = jnp.max(s, axis=-1, keepdims=True)
        p = jnp.exp2(s - m)
        m_sc[...] = m
        l_sc[...] = jnp.sum(p, axis=-1, keepdims=True)
        acc_sc[...] = _mm(p.astype(BF16), cn)

    qr = qh[:, MLA_NOPE:MLA_NOPE + MLA_ROPE]
    pages = cbuf[slot].astype(BF16)
    krt = jnp.concatenate([kbuf[slot, i] for i in range(DEC_PG)], axis=1).astype(BF16)
    gk = pages.shape[0] // DEC_GROUPS
    groups = [pages[g * gk:(g + 1) * gk] for g in range(DEC_GROUPS)]
    scores, probs, parts = {}, {}, []

    def score(g):
        scores[g] = _nt(qlat, groups[g]) + _mm(qr, krt[:, g * gk:(g + 1) * gk])

    def soft(g):
        m_g = jnp.max(scores[g], axis=-1, keepdims=True)
        p = jnp.exp2(scores[g] - m_g)
        probs[g] = (m_g, jnp.sum(p, axis=-1, keepdims=True), p.astype(BF16))

    def value(g):
        m_g, l_g, pb = probs[g]
        parts.append((m_g, l_g, _mm(pb, groups[g])))

    per_group = len(next_copies) // DEC_GROUPS
    for g in range(DEC_GROUPS):
        score(g)
    soft(0)
    for g in range(DEC_GROUPS):
        if g + 1 < DEC_GROUPS:
            soft(g + 1)
        value(g)
        for c in next_copies[g * per_group:(g + 1) * per_group]:
            c.start()
    m_prev = m_sc[...]
    m_new = functools.reduce(jnp.maximum, [m_g for m_g, _, _ in parts], m_prev)
    alpha = jnp.exp2(m_prev - m_new)
    l_new = alpha * l_sc[...]
    acc = alpha * acc_sc[...]
    for m_g, l_g, o_g in parts:
        w_g = jnp.exp2(m_g - m_new)
        l_new = l_new + w_g * l_g
        acc = acc + w_g * o_g
    m_sc[...] = m_new
    l_sc[...] = l_new
    acc_sc[...] = acc

    @pl.when(j == nj - 1)
    def _():
        out = acc_sc[...] / l_sc[...]
        for hh in range(MLA_H):
            o_ref[0, :, hh * KV_LORA:(hh + 1) * KV_LORA] = out[hh * ds:(hh + 1) * ds, :]

    @pl.when(step == last_step)
    def _():
        for s0 in range(DEC_SLOTS):
            _dec_wait_slot(cbuf, kbuf, sems, s0)


def _attn_sample(qlat, qh, ckvn, krn, pool_ckv, pool_kr, page_table):
    b, ds, _ = qlat.shape
    nq = ds * MLA_H
    n_pages = page_table.shape[1]
    pg = DEC_PG
    pt = page_table.reshape(-1)
    pt = jnp.concatenate([pt, pt[:DEC_SLOTS * pg]])

    fixed = lambda w, r: pl.BlockSpec((1, r, w), lambda bi, j, pt_ref: (bi, 0, 0))
    grid_spec = pltpu.PrefetchScalarGridSpec(
        num_scalar_prefetch=1,
        grid=(b, n_pages // pg),
        in_specs=[fixed(MLA_H * KV_LORA, ds), fixed(MLA_H * HEAD_PAD, ds), fixed(KV_LORA, ds), fixed(HEAD_PAD, ds),
                  pl.BlockSpec(memory_space=pl.ANY), pl.BlockSpec(memory_space=pl.ANY)],
        out_specs=fixed(MLA_H * KV_LORA, ds),
        scratch_shapes=[pltpu.VMEM((nq, 1), F32), pltpu.VMEM((nq, 1), F32),
                        pltpu.VMEM((nq, KV_LORA), F32),
                        pltpu.VMEM((PAGE_SIZE, KV_LORA), BF16), pltpu.VMEM((PAGE_SIZE, HEAD_PAD), BF16),
                        pltpu.VMEM((DEC_SLOTS, pg * PAGE_SIZE, KV_LORA), F32),
                        pltpu.VMEM((DEC_SLOTS, pg, MLA_ROPE, PAGE_SIZE), F32),
                        pltpu.SemaphoreType.DMA((DEC_SLOTS, 2))],
    )
    return pl.pallas_call(
        _dec_kernel,
        grid_spec=grid_spec,
        out_shape=jax.ShapeDtypeStruct((b, ds, MLA_H * KV_LORA), F32),
        compiler_params=pltpu.CompilerParams(dimension_semantics=("arbitrary", "arbitrary"),
                                             vmem_limit_bytes=VMEM_LIMIT),
        name="attn_sample",
    )(pt, qlat, qh, ckvn, krn, pool_ckv, pool_kr)


def _sigmoid(x):
    return 0.5 * jnp.tanh(0.5 * x) + 0.5


def _silu(x):
    h = 0.5 * x
    return h * jnp.tanh(h) + h


def _back_kernel(absorb, x_ref, gate_ref, hml_ref, mo_ref, mz_ref, o_ref, az_ref, gn_ref, wuv_ref,
                 wout_ref, lng_ref, lnb_ref, y_ref):
    tb, ts, d = x_ref.shape
    tm = tb * ts
    hm = hml_ref[...].astype(F32) * _sigmoid(mo_ref[...].astype(F32))
    parts = []
    for hh in range(ML_H):
        t = hm[:, hh * ML_DV:(hh + 1) * ML_DV]
        mu = jnp.mean(t, axis=-1, keepdims=True)
        tc = t - mu
        var = jnp.mean(tc * tc, axis=-1, keepdims=True)
        parts.append(tc * lax.rsqrt(var + EPS))
    mz = mz_ref[...].astype(F32)
    y_ml = jnp.concatenate(parts, axis=1) * gn_ref[...] * _silu(mz)
    if absorb:
        olat = o_ref[...].astype(BF16)
        o_mla = jnp.concatenate(
            [_mm(olat[:, p * 2 * KV_LORA:(p + 1) * 2 * KV_LORA], wuv_ref[p]) for p in range(MLA_H // 2)],
            axis=1)
    else:
        o_mla = o_ref[...].astype(F32)
    az = az_ref[...].astype(F32)
    y_mla = o_mla * _silu(az)
    out = _mm(y_ml.astype(BF16), wout_ref[0:ML_W, :]) + _mm(y_mla.astype(BF16), wout_ref[ML_W:, :])
    z = ALPHA * x_ref[...] + gate_ref[...] * out.reshape(tb, ts, d)
    mu = jnp.mean(z, axis=-1, keepdims=True)
    zc = z - mu
    var = jnp.mean(zc * zc, axis=-1, keepdims=True)
    y_ref[...] = zc * lax.rsqrt(var + EPS) * lng_ref[...] + lnb_ref[...]


def _back(absorb, x, mod, hml, mo, mz, o, az, wts):
    b, s, d = x.shape
    n = b * s
    tm = BACK_TM
    if absorb:
        tb, ts = tm // s, s
        x_map = lambda i: (i, 0, 0)
        gate_map = lambda i: (i, 0, 2)
    else:
        tb, ts = 1, tm
        spb = s // tm
        x_map = lambda i: (i // spb, i % spb, 0)
        gate_map = lambda i: (i // spb, 0, 2)
    const2 = lambda i: (0, 0)
    tok = lambda w: pl.BlockSpec((tm, w), lambda i: (i, 0))
    ow = o.shape[1]
    return pl.pallas_call(
        functools.partial(_back_kernel, absorb),
        grid=(n // tm,),
        in_specs=[pl.BlockSpec((tb, ts, d), x_map),
                  pl.BlockSpec((tb, 1, d), gate_map),
                  tok(ML_W), tok(ML_W), tok(ML_W), tok(ow), tok(MLA_W),
                  pl.BlockSpec((1, ML_W), const2),
                  pl.BlockSpec((MLA_H // 2, 2 * KV_LORA, LANES), lambda i: (0, 0, 0)),
                  pl.BlockSpec((d, d), const2),
                  pl.BlockSpec((1, d), const2),
                  pl.BlockSpec((1, d), const2)],
        out_specs=pl.BlockSpec((tb, ts, d), x_map),
        out_shape=jax.ShapeDtypeStruct((b, s, d), F32),
        compiler_params=pltpu.CompilerParams(dimension_semantics=("arbitrary",),
                                             vmem_limit_bytes=VMEM_LIMIT),
        name="back_sample" if absorb else "back_prompt",
    )(x, mod, hml, mo, mz, o, az, wts["gn"], wts["wuv_pair"], wts["w_out"], wts["ln_g"], wts["ln_b"])


def _prep_weights(l, w_in, ml_b_i, ml_b_f, ml_gn, mla_q_norm, mla_kv_norm, mla_w_uq, mla_w_uk,
                  mla_w_uv, w_out, ln_g, ln_b):
    assert w_in.shape[2] == N_IN
    tail = HEAD_PAD - MLA_NOPE - MLA_ROPE
    w_in_p = _wprep(jnp.swapaxes(w_in[l], 0, 1))
    b_gate = jnp.concatenate([ml_b_i[l], ml_b_f[l], jnp.zeros((LANES - 2 * ML_H,), F32)]).reshape(1, LANES)
    uq = mla_w_uq[l].reshape(Q_LORA, MLA_H, MLA_NOPE + MLA_ROPE)
    nope, r1, r2 = uq[..., :MLA_NOPE], uq[..., MLA_NOPE:MLA_NOPE + ROPE_HALF], uq[..., MLA_NOPE + ROPE_HALF:]
    zq = lambda c: jnp.zeros((Q_LORA, MLA_H, c), F32)
    wqa = jnp.concatenate([nope, r1, r2, zq(tail)], axis=-1).reshape(Q_LORA, MLA_H * HEAD_PAD).astype(BF16)
    uk = mla_w_uk[l].reshape(KV_LORA, MLA_H, MLA_NOPE)
    wk_pad = jnp.concatenate([uk, jnp.zeros((KV_LORA, MLA_H, HEAD_PAD - MLA_NOPE), F32)],
                             axis=-1).reshape(KV_LORA, MLA_H * HEAD_PAD).astype(BF16)
    wukt = jnp.concatenate([jnp.transpose(uk, (1, 2, 0)),
                            jnp.zeros((MLA_H, HEAD_PAD - MLA_NOPE, KV_LORA), F32)], axis=1).astype(BF16)
    uv = mla_w_uv[l].reshape(KV_LORA, MLA_H // 2, 2, MLA_VD)
    zv = jnp.zeros((KV_LORA, MLA_H // 2, MLA_VD), F32)
    wuv_pair = jnp.concatenate(
        [jnp.concatenate([uv[:, :, 0], zv], axis=-1), jnp.concatenate([zv, uv[:, :, 1]], axis=-1)],
        axis=0)
    wuv_pair = jnp.transpose(wuv_pair, (1, 0, 2)).astype(BF16)
    return dict(w_in=w_in_p, b_gate=b_gate, q_norm=mla_q_norm[l].reshape(1, -1),
                kv_norm=mla_kv_norm[l].reshape(1, -1), wqa=wqa, wk_pad=wk_pad, wukt=wukt,
                w_uv_t=mla_w_uv[l].T.astype(BF16), wuv_pair=wuv_pair, gn=ml_gn[l].reshape(1, -1),
                w_out=w_out[l].astype(BF16), ln_g=ln_g[l].reshape(1, -1), ln_b=ln_b[l].reshape(1, -1))


def _rope_tables(pos):
    f32 = np.float32
    inv = ROPE_THETA ** (-np.arange(ROPE_HALF, dtype=np.float64) / ROPE_HALF)
    ang = pos.astype(np.float64)[:, None] * inv[None, :]
    n = pos.shape[0]
    tail = HEAD_PAD - MLA_NOPE - MLA_ROPE
    cos, sin = np.cos(ang).astype(f32), np.sin(ang).astype(f32)
    zeros = lambda c: np.zeros((n, c), f32)
    cm = np.concatenate([np.ones((n, MLA_NOPE), f32), cos, cos, zeros(tail)], 1)
    s_lo = np.concatenate([zeros(MLA_NOPE), -sin, zeros(ROPE_HALF), zeros(tail)], 1)
    s_hi = np.concatenate([zeros(MLA_NOPE), zeros(ROPE_HALF), sin, zeros(tail)], 1)
    return np.concatenate([cm, s_lo, s_hi], 1)


def kernel(x_prompt, x_sample, c_prompt, c_sample, cache_ckv, cache_krope, state_C, state_n, state_m,
           page_table, w_ada, b_ada, w_in, ml_b_i, ml_b_f, ml_gn, mla_q_norm, mla_kv_norm,
           mla_w_uq, mla_w_uk, mla_w_uv, w_out, ln_g, ln_b):
    bp, sp, d = x_prompt.shape
    bs, ss, _ = x_sample.shape
    past = page_table.shape[1] * PAGE_SIZE
    tab_p = jnp.asarray(_rope_tables(np.arange(sp)))
    tab_s = jnp.asarray(np.tile(_rope_tables(past + np.arange(ss)), (FRONT_TM // ss, 1)))
    pad = (-(bp + bs)) % 8
    c_all = jnp.concatenate([c_prompt, c_sample, jnp.zeros((pad, d), F32)], axis=0)

    xp, xs = x_prompt, x_sample
    outs = [[] for _ in range(10)]
    for l in range(DEPTH):
        wts = _prep_weights(l, w_in, ml_b_i, ml_b_f, ml_gn, mla_q_norm, mla_kv_norm, mla_w_uq, mla_w_uk,
                            mla_w_uv, w_out, ln_g, ln_b)
        mod = _ada(c_all, w_ada[l], b_ada[l])
        mod_p = mod[:bp].reshape(bp, 1, 3 * d)
        mod_s = mod[bp:bp + bs].reshape(bs, 1, 3 * d)

        mq, mk, mv, g, mo, mz, az, ckv, kr, qh, kh, vt = _front(False, xp, mod_p, tab_p, wts)
        seq3 = lambda t: t.reshape(bp, sp, -1)
        c0 = jnp.zeros((bp, ML_H, ML_DK, ML_DV), F32)
        n0 = jnp.zeros((bp, ML_H, 1, ML_DK), F32)
        m0 = jnp.full((bp, ML_H, 1, LANES), -jnp.inf, F32)
        hml, c_p, n_p, m_p = _mlstm(seq3(mq), seq3(mk), seq3(mv), seq3(g), c0, n0, m0, ML_CHUNK, bp)
        o_p = _attn_prompt(qh, kh, vt)
        yp = _back(False, xp, mod_p, hml.reshape(bp * sp, -1), mo, mz, o_p.reshape(bp * sp, -1), az, wts)
        outs[0].append(ckv.reshape(bp, sp, KV_LORA))
        outs[1].append(jnp.swapaxes(kr, 1, 2))
        outs[2].append(c_p)
        outs[3].append(n_p[:, :, 0, :])
        outs[4].append(m_p[:, :, 0, 0])

        mq, mk, mv, g, mo, mz, az, ckv, kr, qlat, qh, krp = _front(True, xs, mod_s, tab_s, wts)
        seq3 = lambda t: t.reshape(bs, ss, -1)
        n0 = state_n[l].astype(F32).reshape(bs, ML_H, 1, ML_DK)
        m0 = jnp.broadcast_to(state_m[l].astype(F32)[:, :, None, None], (bs, ML_H, 1, LANES))
        hml, c_s, n_s, m_s = _mlstm(seq3(mq), seq3(mk), seq3(mv), seq3(g), state_C[l].astype(F32), n0, m0, ss, 8)
        o_s = _attn_sample(seq3(qlat), seq3(qh), seq3(ckv), seq3(krp), cache_ckv[l],
                           jnp.swapaxes(cache_krope[l], 1, 2), page_table)
        ys = _back(True, xs, mod_s, hml.reshape(bs * ss, -1), mo, mz,
                   o_s.reshape(bs * ss, MLA_H * KV_LORA), az, wts)
        outs[5].append(ckv.reshape(bs, ss, KV_LORA).astype(cache_ckv.dtype))
        outs[6].append(kr.reshape(bs, ss, MLA_ROPE).astype(cache_krope.dtype))
        outs[7].append(c_s.astype(state_C.dtype))
        outs[8].append(n_s[:, :, 0, :].astype(state_n.dtype))
        outs[9].append(m_s[:, :, 0, 0].astype(state_m.dtype))
        xp, xs = yp, ys
    return (xp, xs) + tuple(jnp.stack(o) for o in outs)
```
